```python
import math
import jax, jax.numpy as jnp
from jax import lax
import numpy as np

D_MODEL = 1024
BATCH = 1
SEQ = 16384
DEPTH = 2
DEC_BATCH = 32
DEC_SEQ = 64
PAST_LEN = 4096

CHUNK = 64
CONV_WIDTH = 4
HEAD_DIM = 64
GDN_WIDTH = D_MODEL // 4
GDN_HEADS = GDN_WIDTH // HEAD_DIM
SSM_WIDTH = D_MODEL // 2
SSM_HEADS = SSM_WIDTH // HEAD_DIM
SSM_GROUPS = 2
SSM_STATE = 128
RWKV_WIDTH = D_MODEL - GDN_WIDTH - SSM_WIDTH
RWKV_HEADS = RWKV_WIDTH // HEAD_DIM
RWKV_DECAY_LORA = 64
RWKV_ICLR_LORA = 64
RWKV_GATE_LORA = 128
MIX_WIDTH = GDN_WIDTH + SSM_WIDTH + RWKV_WIDTH
FFN_HIDDEN = -(-8 * D_MODEL // (3 * 256)) * 256
GDN_CONV_CH = 3 * GDN_WIDTH
GDN_COLS = GDN_CONV_CH + GDN_WIDTH + 2 * GDN_HEADS
SSM_BC = SSM_GROUPS * SSM_STATE
SSM_CONV_CH = SSM_WIDTH + 2 * SSM_BC
SSM_COLS = SSM_WIDTH + SSM_CONV_CH + SSM_HEADS
RWKV_COLS = 3 * RWKV_WIDTH + RWKV_DECAY_LORA + RWKV_ICLR_LORA + RWKV_GATE_LORA
IN_COLS = GDN_COLS + SSM_COLS + RWKV_COLS
NORM_EPS = 1e-6
RWKV_GN_EPS = 64e-5

kernel_name = 'hybrid_gdn_ssd_rwkv7_stream_step'


def rms_norm(x, w):
    xf = x.astype(jnp.float32)
    y = xf * lax.rsqrt(jnp.mean(xf * xf, axis=-1, keepdims=True) + NORM_EPS)
    return (y * w.astype(jnp.float32)).astype(x.dtype)


def l2_normalize(x):
    return x * lax.rsqrt(jnp.sum(x * x, axis=-1, keepdims=True) + NORM_EPS)


def causal_conv(x, prev, w):
    t = x.shape[1]
    xp = jnp.concatenate([prev.astype(x.dtype), x], axis=1)
    w = w.astype(x.dtype)
    y = xp[:, 0:t] * w[0]
    for j in range(1, CONV_WIDTH):
        y = y + xp[:, j:j + t] * w[j]
    return y, xp[:, t:]


def gated_delta_chunked(q, k, v, g, beta, s0, chunk):
    bsz, t, h, _ = q.shape
    dv = v.shape[-1]
    n = t // chunk

    def blocks(a):
        return jnp.moveaxis(a.reshape((bsz, n, chunk, h) + a.shape[3:]), 3, 2)

    qc, kc, vc = blocks(q), blocks(k), blocks(v)
    gc, bc = blocks(g), blocks(beta)
    gcum = jnp.cumsum(gc, axis=-1)
    causal = jnp.tril(jnp.ones((chunk, chunk), dtype=bool))
    strict = jnp.tril(jnp.ones((chunk, chunk), dtype=bool), -1)
    diff = gcum[..., :, None] - gcum[..., None, :]
    decay = jnp.where(causal, jnp.exp(jnp.where(causal, diff, 0.0)), 0.0)
    k_beta = kc * bc[..., None]
    v_beta = vc * bc[..., None]
    lower = jnp.where(strict, jnp.einsum('bnhik,bnhjk->bnhij', k_beta, kc) * decay, 0.0)
    unit_lower = lower + jnp.eye(chunk, dtype=lower.dtype)
    u = lax.linalg.triangular_solve(unit_lower, v_beta, left_side=True, lower=True)
    w = lax.linalg.triangular_solve(unit_lower, k_beta * jnp.exp(gcum)[..., None],
                                    left_side=True, lower=True)
    attn = jnp.where(causal, jnp.einsum('bnhik,bnhjk->bnhij', qc, kc) * decay, 0.0)
    q_dec = qc * jnp.exp(gcum)[..., None]
    k_dec = kc * jnp.exp(gcum[..., -1:] - gcum)[..., None]
    blk_dec = jnp.exp(gcum[..., -1])

    def step(s, inp):
        u_i, w_i, a_i, qd_i, kd_i, bd_i = inp
        v_new = u_i - jnp.einsum('bhck,bhkv->bhcv', w_i, s)
        o_i = jnp.einsum('bhck,bhkv->bhcv', qd_i, s) + jnp.einsum('bhcj,bhjv->bhcv', a_i, v_new)
        s = s * bd_i[..., None, None] + jnp.einsum('bhck,bhcv->bhkv', kd_i, v_new)
        return s, o_i

    xs = tuple(jnp.moveaxis(a, 1, 0) for a in (u, w, attn, q_dec, k_dec, blk_dec))
    s_final, o = lax.scan(step, s0, xs)
    o = jnp.transpose(o, (1, 0, 3, 2, 4)).reshape(bsz, t, h, dv)
    return o, s_final


def ssd_chunked(xdt, adt, bm, cm, s0, chunk):
    bsz, t, h, p = xdt.shape
    nst = bm.shape[-1]
    nc = t // chunk
    xc = xdt.reshape(bsz, nc, chunk, h, p)
    bc = bm.reshape(bsz, nc, chunk, h, nst)
    cc = cm.reshape(bsz, nc, chunk, h, nst)
    ac = jnp.moveaxis(adt.reshape(bsz, nc, chunk, h), 3, 2)
    acum = jnp.cumsum(ac, axis=-1)
    causal = jnp.tril(jnp.ones((chunk, chunk), dtype=bool))
    seg = acum[..., :, None] - acum[..., None, :]
    lmat = jnp.where(causal, jnp.exp(jnp.where(causal, seg, 0.0)), 0.0)
    scores = jnp.einsum('bclhn,bcshn->bchls', cc, bc) * lmat
    y_diag = jnp.einsum('bchls,bcshp->bclhp', scores, xc)
    decay_to_end = jnp.exp(acum[..., -1:] - acum)
    chunk_states = jnp.einsum('bclhn,bchl,bclhp->bchpn', bc, decay_to_end, xc)
    chunk_decay = jnp.exp(acum[..., -1])

    def step(s, inp):
        cs, cd = inp
        return s * cd[..., None, None] + cs, s

    s_final, s_in = lax.scan(step, s0, (jnp.moveaxis(chunk_states, 1, 0),
                                        jnp.moveaxis(chunk_decay, 1, 0)))
    s_in = jnp.moveaxis(s_in, 0, 1)
    y_off = jnp.einsum('bclhn,bchpn,bchl->bclhp', cc, s_in, jnp.exp(acum))
    return (y_diag + y_off).reshape(bsz, t, h, p), s_final


def rwkv7_scan(r, w, k, v, a, b, s0):
    def step(s, inp):
        r_t, w_t, k_t, v_t, a_t, b_t = inp
        sa = jnp.einsum('bhvk,bhk->bhv', s, a_t)
        s = s * w_t[:, :, None, :] + sa[..., None] * b_t[:, :, None, :] + v_t[..., None] * k_t[:, :, None, :]
        return s, jnp.einsum('bhvk,bhk->bhv', s, r_t)

    xs = tuple(jnp.moveaxis(z, 1, 0) for z in (r, w, k, v, a, b))
    s_final, ys = lax.scan(step, s0, xs)
    return jnp.moveaxis(ys, 0, 1), s_final


def gdn_mixer(p, s0, conv_prev, lp, chunk):
    bsz, t, _ = p.shape
    h, d = GDN_HEADS, HEAD_DIM
    c0 = GDN_CONV_CH
    c1 = c0 + GDN_WIDTH
    c2 = c1 + GDN_HEADS
    qkv, conv_new = causal_conv(p[..., :c0], conv_prev, lp['gdn_conv_w'])
    qkv = jax.nn.silu(qkv).reshape(bsz, t, 3, h, d)
    q = l2_normalize(qkv[:, :, 0]) * (d ** -0.5)
    k = l2_normalize(qkv[:, :, 1])
    v = qkv[:, :, 2]
    z = p[..., c0:c1].reshape(bsz, t, h, d)
    g = -jnp.exp(lp['gdn_A_log'].astype(jnp.float32)) * jax.nn.softplus(p[..., c1:c2] + lp['gdn_dt_bias'])
    beta = jax.nn.sigmoid(p[..., c2:c2 + h])
    o, s = gated_delta_chunked(q, k, v, g, beta, s0.astype(jnp.float32), chunk)
    o = rms_norm(o, lp['gdn_norm_w']) * jax.nn.silu(z)
    return o.reshape(bsz, t, GDN_WIDTH), s, conv_new


def ssd_mixer(p, s0, conv_prev, lp, chunk):
    bsz, t, _ = p.shape
    h, d, g, n = SSM_HEADS, HEAD_DIM, SSM_GROUPS, SSM_STATE
    c0 = SSM_WIDTH
    c1 = c0 + SSM_CONV_CH
    z = p[..., :c0]
    xbc, conv_new = causal_conv(p[..., c0:c1], conv_prev, lp['ssm_conv_w'])
    xbc = jax.nn.silu(xbc + lp['ssm_conv_b'])
    xs = xbc[..., :SSM_WIDTH].reshape(bsz, t, h, d)
    bm = jnp.repeat(xbc[..., SSM_WIDTH:SSM_WIDTH + SSM_BC].reshape(bsz, t, g, n), h // g, axis=2)
    cm = jnp.repeat(xbc[..., SSM_WIDTH + SSM_BC:].reshape(bsz, t, g, n), h // g, axis=2)
    dt = jax.nn.softplus(p[..., c1:c1 + h] + lp['ssm_dt_bias'])
    a = -jnp.exp(lp['ssm_A_log'].astype(jnp.float32))
    y, s = ssd_chunked(xs * dt[..., None], dt * a, bm, cm, s0.astype(jnp.float32), chunk)
    y = y + lp['ssm_D'].astype(jnp.float32)[:, None] * xs
    y = y.reshape(bsz, t, SSM_WIDTH) * jax.nn.silu(z)
    yg = y.reshape(bsz, t, g, SSM_WIDTH // g)
    yg = yg * lax.rsqrt(jnp.mean(yg * yg, axis=-1, keepdims=True) + NORM_EPS)
    y = yg.reshape(bsz, t, SSM_WIDTH) * lp['ssm_norm_w'].astype(jnp.float32)
    return y, s, conv_new


def rwkv7_mixer(p, s0, shift_prev, lp):
    bsz, t, _ = p.shape
    h, d, wd = RWKV_HEADS, HEAD_DIM, RWKV_WIDTH
    prev = jnp.concatenate([shift_prev.astype(p.dtype), p[:, :-1]], axis=1)
    xm = p + (prev - p) * lp['rwkv_mu']
    c0 = 3 * wd
    c1 = c0 + RWKV_DECAY_LORA
    c2 = c1 + RWKV_ICLR_LORA
    r = xm[..., :wd]
    k = xm[..., wd:2 * wd]
    v = xm[..., 2 * wd:c0]
    w_log = -jax.nn.softplus(-(lp['rwkv_w0'] + jnp.matmul(jnp.tanh(xm[..., c0:c1]), lp['rwkv_w_up']))) - 0.5
    decay = jnp.exp(-jnp.exp(w_log))
    iclr = jax.nn.sigmoid(lp['rwkv_a0'] + jnp.matmul(xm[..., c1:c2], lp['rwkv_a_up']))
    gate = jnp.matmul(jax.nn.sigmoid(xm[..., c2:]), lp['rwkv_g_up'])

    def heads(z):
        return z.reshape(bsz, t, h, d)

    kk = l2_normalize(heads(k * lp['rwkv_k_k']))
    k = k * (1.0 + (iclr - 1.0) * lp['rwkv_k_a'])
    r_h, k_h, v_h = heads(r), heads(k), heads(v)
    y, s = rwkv7_scan(r_h, heads(decay), k_h, v_h, -kk, kk * heads(iclr), s0.astype(jnp.float32))
    mean = jnp.mean(y, axis=-1, keepdims=True)
    var = jnp.mean(jnp.square(y - mean), axis=-1, keepdims=True)
    y = ((y - mean) * lax.rsqrt(var + RWKV_GN_EPS)).reshape(bsz, t, wd)
    y = y * lp['rwkv_ln_w'] + lp['rwkv_ln_b']
    bonus = jnp.sum(r_h * k_h * lp['rwkv_r_k'], axis=-1, keepdims=True) * v_h
    y = (y + bonus.reshape(bsz, t, wd)) * gate
    return y, s, p[:, -1:]


def layer(x, st, lp):
    gdn_s, gdn_conv, ssm_s, ssm_conv, rwkv_s, rwkv_shift = st
    chunk = min(CHUNK, x.shape[1])
    hn = rms_norm(x, lp['norm1_w'])
    proj = jnp.matmul(hn, lp['w_in']).astype(jnp.float32)
    pa = proj[..., :GDN_COLS]
    pb = proj[..., GDN_COLS:GDN_COLS + SSM_COLS]
    pc = proj[..., GDN_COLS + SSM_COLS:]
    ya, gdn_s_new, gdn_conv_new = gdn_mixer(pa, gdn_s, gdn_conv, lp, chunk)
    yb, ssm_s_new, ssm_conv_new = ssd_mixer(pb, ssm_s, ssm_conv, lp, chunk)
    yc, rwkv_s_new, rwkv_shift_new = rwkv7_mixer(pc, rwkv_s, rwkv_shift, lp)
    mix = jnp.concatenate([ya, yb, yc], axis=-1).astype(x.dtype)
    x = x + jnp.matmul(mix, lp['w_out'])
    h2 = rms_norm(x, lp['norm2_w'])
    ff = jax.nn.silu(jnp.matmul(h2, lp['ffn_w_gate'])) * jnp.matmul(h2, lp['ffn_w_up'])
    x = x + jnp.matmul(ff, lp['ffn_w_down'])
    return x, (gdn_s_new, gdn_conv_new, ssm_s_new, ssm_conv_new, rwkv_s_new, rwkv_shift_new)


def trunk(x, states, params, final_norm_w):
    outs = [[] for _ in states]
    for l in range(DEPTH):
        lp = {name: w[l] for name, w in params.items()}
        x, new = layer(x, tuple(s[l] for s in states), lp)
        for o, nw in zip(outs, new):
            o.append(nw.astype(x.dtype))
    y = rms_norm(x, final_norm_w)
    return y, tuple(jnp.stack(o) for o in outs)


def setup_inputs(seed: int = 0) -> dict:
    key = jax.random.key(seed)
    ks = iter(jax.random.split(key, 48))

    def nrm(shape, scale):
        return scale * jax.random.normal(next(ks), shape, jnp.float32)

    def unif(shape, lo, hi):
        return jax.random.uniform(next(ks), shape, jnp.float32, lo, hi)

    def dt_bias(shape):
        dt = jnp.exp(unif(shape, math.log(1e-3), math.log(1e-1)))
        return dt + jnp.log(-jnp.expm1(-dt))

    L, D, W1 = DEPTH, D_MODEL, CONV_WIDTH - 1
    return {
        'x_prompt': nrm((BATCH, SEQ, D), 1.0),
        'x_sample': nrm((DEC_BATCH, DEC_SEQ, D), 1.0),
        'state_gdn': nrm((L, DEC_BATCH, GDN_HEADS, HEAD_DIM, HEAD_DIM), 0.3),
        'state_gdn_conv': nrm((L, DEC_BATCH, W1, GDN_CONV_CH), 1.0),
        'state_ssm': nrm((L, DEC_BATCH, SSM_HEADS, HEAD_DIM, SSM_STATE), 0.3),
        'state_ssm_conv': nrm((L, DEC_BATCH, W1, SSM_CONV_CH), 1.0),
        'state_rwkv': nrm((L, DEC_BATCH, RWKV_HEADS, HEAD_DIM, HEAD_DIM), 0.3),
        'state_rwkv_shift': nrm((L, DEC_BATCH, 1, RWKV_COLS), 1.0),
        'norm1_w': 1.0 + nrm((L, D), 0.02),
        'w_in': nrm((L, D, IN_COLS), D ** -0.5),
        'gdn_conv_w': nrm((L, CONV_WIDTH, GDN_CONV_CH), CONV_WIDTH ** -0.5),
        'gdn_A_log': jnp.log(unif((L, GDN_HEADS), 1.0, 16.0)),
        'gdn_dt_bias': dt_bias((L, GDN_HEADS)),
        'gdn_norm_w': 1.0 + nrm((L, HEAD_DIM), 0.02),
        'ssm_conv_w': nrm((L, CONV_WIDTH, SSM_CONV_CH), CONV_WIDTH ** -0.5),
        'ssm_conv_b': nrm((L, SSM_CONV_CH), 0.02),
        'ssm_A_log': jnp.log(unif((L, SSM_HEADS), 1.0, 16.0)),
        'ssm_dt_bias': dt_bias((L, SSM_HEADS)),
        'ssm_D': 1.0 + nrm((L, SSM_HEADS), 0.1),
        'ssm_norm_w': 1.0 + nrm((L, SSM_WIDTH), 0.02),
        'rwkv_mu': unif((L, RWKV_COLS), 0.0, 1.0),
        'rwkv_w0': unif((L, RWKV_WIDTH), -6.5, -1.5),
        'rwkv_w_up': nrm((L, RWKV_DECAY_LORA, RWKV_WIDTH), 0.5 * RWKV_DECAY_LORA ** -0.5),
        'rwkv_a0': nrm((L, RWKV_WIDTH), 0.1),
        'rwkv_a_up': nrm((L, RWKV_ICLR_LORA, RWKV_WIDTH), 0.5 * RWKV_ICLR_LORA ** -0.5),
        'rwkv_g_up': nrm((L, RWKV_GATE_LORA, RWKV_WIDTH), RWKV_GATE_LORA ** -0.5),
        'rwkv_k_k': 0.85 + nrm((L, RWKV_WIDTH), 0.02),
        'rwkv_k_a': 1.0 + nrm((L, RWKV_WIDTH), 0.02),
        'rwkv_r_k': nrm((L, RWKV_HEADS, HEAD_DIM), 0.1),
        'rwkv_ln_w': 1.0 + nrm((L, RWKV_WIDTH), 0.02),
        'rwkv_ln_b': nrm((L, RWKV_WIDTH), 0.02),
        'w_out': nrm((L, MIX_WIDTH, D), MIX_WIDTH ** -0.5),
        'norm2_w': 1.0 + nrm((L, D), 0.02),
        'ffn_w_gate': nrm((L, D, FFN_HIDDEN), D ** -0.5),
        'ffn_w_up': nrm((L, D, FFN_HIDDEN), D ** -0.5),
        'ffn_w_down': nrm((L, FFN_HIDDEN, D), FFN_HIDDEN ** -0.5),
        'final_norm_w': 1.0 + nrm((D,), 0.02),
    }


def reference(x_prompt, x_sample, state_gdn, state_gdn_conv, state_ssm, state_ssm_conv,
              state_rwkv, state_rwkv_shift,
              norm1_w, w_in, gdn_conv_w, gdn_A_log, gdn_dt_bias, gdn_norm_w,
              ssm_conv_w, ssm_conv_b, ssm_A_log, ssm_dt_bias, ssm_D, ssm_norm_w,
              rwkv_mu, rwkv_w0, rwkv_w_up, rwkv_a0, rwkv_a_up, rwkv_g_up,
              rwkv_k_k, rwkv_k_a, rwkv_r_k, rwkv_ln_w, rwkv_ln_b,
              w_out, norm2_w, ffn_w_gate, ffn_w_up, ffn_w_down, final_norm_w):
    params = {
        'norm1_w': norm1_w, 'w_in': w_in,
        'gdn_conv_w': gdn_conv_w, 'gdn_A_log': gdn_A_log, 'gdn_dt_bias': gdn_dt_bias,
        'gdn_norm_w': gdn_norm_w,
        'ssm_conv_w': ssm_conv_w, 'ssm_conv_b': ssm_conv_b, 'ssm_A_log': ssm_A_log,
        'ssm_dt_bias': ssm_dt_bias, 'ssm_D': ssm_D, 'ssm_norm_w': ssm_norm_w,
        'rwkv_mu': rwkv_mu, 'rwkv_w0': rwkv_w0, 'rwkv_w_up': rwkv_w_up, 'rwkv_a0': rwkv_a0,
        'rwkv_a_up': rwkv_a_up, 'rwkv_g_up': rwkv_g_up, 'rwkv_k_k': rwkv_k_k,
        'rwkv_k_a': rwkv_k_a, 'rwkv_r_k': rwkv_r_k, 'rwkv_ln_w': rwkv_ln_w, 'rwkv_ln_b': rwkv_ln_b,
        'w_out': w_out, 'norm2_w': norm2_w,
        'ffn_w_gate': ffn_w_gate, 'ffn_w_up': ffn_w_up, 'ffn_w_down': ffn_w_down,
    }
    nb = x_prompt.shape[0]
    f32 = jnp.float32
    zero_states = (
        jnp.zeros((DEPTH, nb, GDN_HEADS, HEAD_DIM, HEAD_DIM), f32),
        jnp.zeros((DEPTH, nb, CONV_WIDTH - 1, GDN_CONV_CH), f32),
        jnp.zeros((DEPTH, nb, SSM_HEADS, HEAD_DIM, SSM_STATE), f32),
        jnp.zeros((DEPTH, nb, CONV_WIDTH - 1, SSM_CONV_CH), f32),
        jnp.zeros((DEPTH, nb, RWKV_HEADS, HEAD_DIM, HEAD_DIM), f32),
        jnp.zeros((DEPTH, nb, 1, RWKV_COLS), f32),
    )
    y_prompt, p_states = trunk(x_prompt, zero_states, params, final_norm_w)
    p_gdn, p_gdn_conv, p_ssm, p_ssm_conv, p_rwkv, p_rwkv_shift = p_states
    cache_states = (state_gdn, state_gdn_conv, state_ssm, state_ssm_conv, state_rwkv, state_rwkv_shift)
    y_sample, s_states = trunk(x_sample, cache_states, params, final_norm_w)
    s_gdn, s_gdn_conv, s_ssm, s_ssm_conv, s_rwkv, s_rwkv_shift = s_states
    return (y_prompt, y_sample,
            p_gdn, p_gdn_conv, p_ssm, p_ssm_conv, p_rwkv, p_rwkv_shift,
            s_gdn, s_gdn_conv, s_ssm, s_ssm_conv, s_rwkv, s_rwkv_shift)
```

```python
import functools
import math
from typing import NamedTuple

import jax
import jax.numpy as jnp
from jax import lax
from jax.experimental import pallas as pl
from jax.experimental.pallas import tpu as pltpu

F32 = jnp.float32
BF16 = jnp.bfloat16

CHUNK = 64
NORM_EPS = 1e-6
RWKV_GN_EPS = 64e-5
LANE = 128
HIST = 8
VMEM_LIMIT_BYTES = 56 * 1024 * 1024
PROJ_ROWS = 256
FFN_ROWS = 512
FFN_COLS = 256


class Cfg(NamedTuple):
    d_model: int
    head_dim: int
    conv_w: int
    gdn_heads: int
    ssm_heads: int
    ssm_groups: int
    ssm_state: int
    rwkv_heads: int
    lora_w: int
    lora_a: int
    lora_g: int

    @property
    def gdn_width(self):
        return self.gdn_heads * self.head_dim

    @property
    def ssm_width(self):
        return self.ssm_heads * self.head_dim

    @property
    def ssm_bc(self):
        return self.ssm_groups * self.ssm_state

    @property
    def rwkv_width(self):
        return self.rwkv_heads * self.head_dim

    @property
    def rwkv_cols(self):
        return 3 * self.rwkv_width + self.lora_w + self.lora_a + self.lora_g

    @property
    def o_gdn_qkv(self):
        return 0

    @property
    def o_gdn_z(self):
        return 3 * self.gdn_width

    @property
    def o_ssm_z(self):
        return self.o_gdn_z + self.gdn_width

    @property
    def o_ssm_xbc(self):
        return self.o_ssm_z + self.ssm_width

    @property
    def o_rwkv(self):
        return self.o_ssm_xbc + self.ssm_width + 2 * self.ssm_bc

    @property
    def o_small(self):
        return self.o_rwkv + self.rwkv_cols

    @property
    def proj_cols(self):
        return self.o_small + LANE

    @property
    def mix_width(self):
        return self.gdn_width + self.ssm_width + self.rwkv_width


def _rms(x, w):
    return x * lax.rsqrt(jnp.mean(x * x, axis=-1, keepdims=True) + NORM_EPS) * w


def _softplus(x):
    return jnp.maximum(x, 0.0) + jnp.log1p(jnp.exp(-jnp.abs(x)))


def _silu(x):
    return x * jax.nn.sigmoid(x)


def _mm(a, b):
    return jnp.dot(a.astype(BF16), b.astype(BF16), preferred_element_type=F32)


def _mm_nt(a, b):
    return lax.dot_general(a.astype(BF16), b.astype(BF16), (((1,), (1,)), ((), ())),
                           preferred_element_type=F32)


def _mm_tn(a, b):
    return lax.dot_general(a.astype(BF16), b.astype(BF16), (((0,), (0,)), ((), ())),
                           preferred_element_type=F32)


def _mm_hi(a, b):
    return jnp.dot(a, b, precision=lax.Precision.HIGHEST, preferred_element_type=F32)


def _inv_one_minus(n, eye_f):
    c = n.shape[0]
    t = eye_f + n
    p = n
    for _ in range(int(math.log2(c)) - 1):
        p = _mm_hi(p, p)
        t = t + _mm_hi(t, p)
    return t


def _col_and_row(col, eye):
    c = col.shape[0]
    cb = jnp.broadcast_to(col, (c, c))
    row = jnp.sum(jnp.where(eye, cb, 0.0), axis=0, keepdims=True)
    return cb, row


def _conv_chunk(buf_ref, x, w_ref, conv_w):
    c = x.shape[0]
    buf_ref[HIST:HIST + c, :] = x
    y = x * w_ref[conv_w - 1:conv_w, :]
    for j in range(conv_w - 1):
        lo = HIST - (conv_w - 1) + j
        y = y + buf_ref[lo:lo + c, :] * w_ref[j:j + 1, :]
    tail = buf_ref[HIST + c - (conv_w - 1):HIST + c, :]
    buf_ref[HIST - (conv_w - 1):HIST, :] = tail
    return y


def _proj_body(x_ref, nw_ref, w_ref, o_ref):
    h = _rms(x_ref[...], nw_ref[...])
    o_ref[...] = jnp.dot(h.astype(BF16), w_ref[...], preferred_element_type=F32)


def _proj_call(x, norm_w, w_bf16):
    n, d = x.shape
    cols = w_bf16.shape[1]
    rows = PROJ_ROWS
    assert n % rows == 0
    return pl.pallas_call(
        _proj_body,
        grid=(n // rows,),
        in_specs=[
            pl.BlockSpec((rows, d), lambda i: (i, 0)),
            pl.BlockSpec((1, d), lambda i: (0, 0)),
            pl.BlockSpec((d, cols), lambda i: (0, 0)),
        ],
        out_specs=pl.BlockSpec((rows, cols), lambda i: (i, 0)),
        out_shape=jax.ShapeDtypeStruct((n, cols), F32),
        compiler_params=pltpu.CompilerParams(
            dimension_semantics=("arbitrary",), vmem_limit_bytes=VMEM_LIMIT_BYTES),
        name="norm_proj",
    )(x, norm_w.reshape(1, d), w_bf16)


def _ffn_body(x_ref, mix_ref, wo_ref, n2_ref, wg_ref, wu_ref, wd_ref, fn_ref, o_ref, *, final):
    x = x_ref[...] + jnp.dot(mix_ref[...], wo_ref[...], preferred_element_type=F32)
    h2 = _rms(x, n2_ref[...]).astype(BF16)
    hidden = wg_ref.shape[1]
    acc = x
    for c0 in range(0, hidden, FFN_COLS):
        g = jnp.dot(h2, wg_ref[:, c0:c0 + FFN_COLS], preferred_element_type=F32)
        u = jnp.dot(h2, wu_ref[:, c0:c0 + FFN_COLS], preferred_element_type=F32)
        ff = (_silu(g) * u).astype(BF16)
        acc = acc + jnp.dot(ff, wd_ref[c0:c0 + FFN_COLS, :], preferred_element_type=F32)
    if final:
        acc = _rms(acc, fn_ref[...])
    o_ref[...] = acc


def _ffn_call(x, mix, wo, n2, wg, wu, wd, fn, final):
    n, d = x.shape
    hidden = wg.shape[1]
    rows = FFN_ROWS
    assert n % rows == 0 and hidden % FFN_COLS == 0
    const = lambda i: (0, 0)
    return pl.pallas_call(
        functools.partial(_ffn_body, final=final),
        grid=(n // rows,),
        in_specs=[
            pl.BlockSpec((rows, d), lambda i: (i, 0)),
            pl.BlockSpec((rows, mix.shape[1]), lambda i: (i, 0)),
            pl.BlockSpec(wo.shape, const),
            pl.BlockSpec((1, d), const),
            pl.BlockSpec(wg.shape, const),
            pl.BlockSpec(wu.shape, const),
            pl.BlockSpec(wd.shape, const),
            pl.BlockSpec((1, d), const),
        ],
        out_specs=pl.BlockSpec((rows, d), lambda i: (i, 0)),
        out_shape=jax.ShapeDtypeStruct((n, d), F32),
        compiler_params=pltpu.CompilerParams(
            dimension_semantics=("arbitrary",), vmem_limit_bytes=VMEM_LIMIT_BYTES),
        name="outproj_ffn",
    )(x, mix, wo, n2.reshape(1, d), wg, wu, wd, fn.reshape(1, d))


def _gdn_chunk(cfg, qkv, z, gc, beta, eg_all, s_ref, norm_w, masks):
    eye, eye_f, incl, strict = masks
    hd, nh = cfg.head_dim, cfg.gdn_heads
    c = qkv.shape[0]
    outs = []
    for h in range(nh):
        q = qkv[:, h * hd:(h + 1) * hd]
        k = qkv[:, (nh + h) * hd:(nh + h + 1) * hd]
        v = qkv[:, (2 * nh + h) * hd:(2 * nh + h + 1) * hd]
        q = q * lax.rsqrt(jnp.sum(q * q, axis=-1, keepdims=True) + NORM_EPS) * (hd ** -0.5)
        k = k * lax.rsqrt(jnp.sum(k * k, axis=-1, keepdims=True) + NORM_EPS)
        gcol = gc[:, h:h + 1]
        gcb, grow = _col_and_row(gcol, eye)
        dec = jnp.exp(jnp.minimum(gcb - grow, 0.0))
        b = beta[:, h:h + 1]
        kb = k * b
        vb = v * b
        lower = jnp.where(strict, _mm_nt(kb, k) * dec, 0.0)
        t = _inv_one_minus(-lower, eye_f)
        eg = eg_all[:, h:h + 1]
        u = _mm_hi(t, vb)
        w = _mm_hi(t, kb * eg)
        attn = jnp.where(incl, _mm_nt(q, k) * dec, 0.0)
        g_last = gcol[c - 1:c, :]
        s = s_ref[h]
        v_new = u - _mm(w, s)
        o = _mm(q * eg, s) + _mm(attn, v_new)
        s_ref[h] = s * jnp.exp(g_last) + _mm_tn(k * jnp.exp(g_last - gcol), v_new)
        o = _rms(o, norm_w) * _silu(z[:, h * hd:(h + 1) * hd])
        outs.append(o)
    return outs


def _ssd_chunk(cfg, xbc, z, ac, dt, s_ref, d_row, norm_w, masks):
    eye, _, incl, _ = masks
    hd, nh, ng, ns = cfg.head_dim, cfg.ssm_heads, cfg.ssm_groups, cfg.ssm_state
    width = cfg.ssm_width
    c = xbc.shape[0]
    ys = []
    for g in range(ng):
        bm = xbc[:, width + g * ns:width + (g + 1) * ns]
        cm = xbc[:, width + cfg.ssm_bc + g * ns:width + cfg.ssm_bc + (g + 1) * ns]
        cb = _mm_nt(cm, bm)
        for h in range(g * (nh // ng), (g + 1) * (nh // ng)):
            acol = ac[:, h:h + 1]
            acb, arow = _col_and_row(acol, eye)
            lmat = jnp.exp(jnp.minimum(acb - arow, 0.0))
            x = xbc[:, h * hd:(h + 1) * hd]
            xdt = x * dt[:, h:h + 1]
            y_diag = _mm(jnp.where(incl, cb * lmat, 0.0), xdt)
            a_last = acol[c - 1:c, :]
            s = s_ref[h]
            y_off = _mm_nt(cm, s) * jnp.exp(acol)
            s_ref[h] = s * jnp.exp(a_last) + _mm_tn(xdt * jnp.exp(a_last - acol), bm)
            ys.append(y_diag + y_off + d_row[:, h:h + 1] * x)
    gw = width // ng
    outs = []
    for g in range(ng):
        yg = jnp.concatenate(ys[g * (nh // ng):(g + 1) * (nh // ng)], axis=-1)
        yg = yg * _silu(z[:, g * gw:(g + 1) * gw])
        yg = yg * lax.rsqrt(jnp.mean(yg * yg, axis=-1, keepdims=True) + NORM_EPS)
        outs.append(yg * norm_w[:, g * gw:(g + 1) * gw])
    return outs


def _rwkv_chunk(cfg, xm, s_ref, prm, tri_f, masks):
    _, eye_f, incl, strict = masks
    (w0, w_up, a0, a_up, g_up, k_k, k_a, r_k, ln_w, ln_b) = prm
    hd, nh, wd = cfg.head_dim, cfg.rwkv_heads, cfg.rwkv_width
    c = xm.shape[0]
    c0 = 3 * wd
    c1 = c0 + cfg.lora_w
    c2 = c1 + cfg.lora_a
    r = xm[:, :wd]
    k = xm[:, wd:2 * wd]
    v = xm[:, 2 * wd:c0]
    w_log = -_softplus(-(w0 + _mm(jnp.tanh(xm[:, c0:c1]), w_up))) - 0.5
    logw = -jnp.exp(w_log)
    iclr = jax.nn.sigmoid(a0 + _mm(xm[:, c1:c2], a_up))
    gate = _mm(jax.nn.sigmoid(xm[:, c2:]), g_up)
    kk_raw = k * k_k
    k2 = k * (1.0 + (iclr - 1.0) * k_a)
    cum = _mm_hi(tri_f, logw)
    pm = jnp.exp(cum)
    pinv = jnp.exp(-cum)
    pprev = jnp.exp(cum - logw)
    pc = pm[c - 1:c, :]
    rt = r * pm
    kt = k2 * pinv
    outs = []
    for h in range(nh):
        sl = slice(h * hd, (h + 1) * hd)
        kkh = kk_raw[:, sl]
        kkh = kkh * lax.rsqrt(jnp.sum(kkh * kkh, axis=-1, keepdims=True) + NORM_EPS)
        at = -kkh * pprev[:, sl]
        bt = kkh * iclr[:, sl] * pinv[:, sl]
        kth = kt[:, sl]
        rth = rt[:, sl]
        vh = v[:, sl]
        pch = pc[:, sl]
        s = s_ref[h]
        a_ab = jnp.where(strict, _mm_nt(at, bt), 0.0)
        a_ak = jnp.where(strict, _mm_nt(at, kth), 0.0)
        t = _inv_one_minus(a_ab, eye_f)
        u = _mm_hi(t, _mm_nt(at, s) + _mm(a_ak, vh))
        y = (_mm_nt(rth, s) + _mm(jnp.where(incl, _mm_nt(rth, bt), 0.0), u)
             + _mm(jnp.where(incl, _mm_nt(rth, kth), 0.0), vh))
        s_ref[h] = s * pch + _mm_tn(u, bt * pch) + _mm_tn(vh, kth * pch)
        mean = jnp.mean(y, axis=-1, keepdims=True)
        yc = y - mean
        var = jnp.mean(yc * yc, axis=-1, keepdims=True)
        yn = yc * lax.rsqrt(var + RWKV_GN_EPS) * ln_w[:, sl] + ln_b[:, sl]
        bonus = jnp.sum(r[:, sl] * k2[:, sl] * r_k[:, sl], axis=-1, keepdims=True) * vh
        outs.append((yn + bonus) * gate[:, sl])
    return outs


def _mixer_body(seq_ref, first_ref,
                p_ref, gdn0_ref, gdnc0_ref, ssm0_ref, ssmc0_ref, rwkv0_ref, shift0_ref,
                gconv_w_ref, alog_ref, dtb_ref, gnorm_ref,
                sconv_w_ref, sconv_b_ref, sd_ref, snorm_ref,
                mu_ref, w0_ref, wup_ref, a0_ref, aup_ref, gup_ref, kk_ref, ka_ref, rk_ref, lnw_ref, lnb_ref,
                mix_ref, gdn_out_ref, ssm_out_ref, rwkv_out_ref,
                gdn_s, ssm_s, rwkv_s, gbuf, sbuf, rbuf, *, cfg):
    del seq_ref
    i = pl.program_id(0)
    c = p_ref.shape[0]
    cw = cfg.conv_w

    @pl.when(first_ref[i] == 1)
    def _load_state():
        gdn_s[...] = gdn0_ref[0]
        ssm_s[...] = ssm0_ref[0]
        rwkv_s[...] = rwkv0_ref[0]
        gbuf[HIST - (cw - 1):HIST, :] = gdnc0_ref[0]
        sbuf[HIST - (cw - 1):HIST, :] = ssmc0_ref[0]
        rbuf[HIST - 1:HIST, :] = shift0_ref[0]

    ri = lax.broadcasted_iota(jnp.int32, (c, c), 0)
    ci = lax.broadcasted_iota(jnp.int32, (c, c), 1)
    eye = ri == ci
    incl = ri >= ci
    strict = ri > ci
    eye_f = eye.astype(F32)
    tri_f = incl.astype(F32)
    masks = (eye, eye_f, incl, strict)

    gh, sh = cfg.gdn_heads, cfg.ssm_heads
    small = p_ref[:, cfg.o_small:cfg.o_small + LANE]
    sp = _softplus(small + dtb_ref[...])
    cum = _mm_hi(tri_f, sp * (-jnp.exp(alog_ref[...])))
    gc = cum[:, 0:gh]
    ac = cum[:, 2 * gh:2 * gh + sh]
    beta = jax.nn.sigmoid(small[:, gh:2 * gh])
    dt = sp[:, 2 * gh:2 * gh + sh]

    qkv = _silu(_conv_chunk(gbuf, p_ref[:, cfg.o_gdn_qkv:cfg.o_gdn_qkv + 3 * cfg.gdn_width], gconv_w_ref, cw))
    z = p_ref[:, cfg.o_gdn_z:cfg.o_gdn_z + cfg.gdn_width]
    outs = _gdn_chunk(cfg, qkv, z, gc, beta, jnp.exp(gc), gdn_s, gnorm_ref[...], masks)

    xbc = _conv_chunk(sbuf, p_ref[:, cfg.o_ssm_xbc:cfg.o_ssm_xbc + cfg.ssm_width + 2 * cfg.ssm_bc], sconv_w_ref, cw)
    xbc = _silu(xbc + sconv_b_ref[...])
    z = p_ref[:, cfg.o_ssm_z:cfg.o_ssm_z + cfg.ssm_width]
    outs += _ssd_chunk(cfg, xbc, z, ac, dt, ssm_s, sd_ref[...], snorm_ref[...], masks)

    pr = p_ref[:, cfg.o_rwkv:cfg.o_rwkv + cfg.rwkv_cols]
    rbuf[HIST:HIST + c, :] = pr
    prev = rbuf[HIST - 1:HIST - 1 + c, :]
    rbuf[HIST - 1:HIST, :] = pr[c - 1:c, :]
    xm = pr + (prev - pr) * mu_ref[...]
    prm = (w0_ref[...], wup_ref[...], a0_ref[...], aup_ref[...], gup_ref[...], kk_ref[...], ka_ref[...],
           rk_ref[...], lnw_ref[...], lnb_ref[...])
    outs += _rwkv_chunk(cfg, xm, rwkv_s, prm, tri_f, masks)

    mix_ref[...] = jnp.concatenate(outs, axis=-1).astype(mix_ref.dtype)
    gdn_out_ref[0] = gdn_s[...]
    ssm_out_ref[0] = ssm_s[...]
    rwkv_out_ref[0] = rwkv_s[...]


def _mixer_call(cfg, seq_id, first, proj, init, prm):
    n = proj.shape[0]
    nchunks = n // CHUNK
    gdn0, gdnc0, ssm0, ssmc0, rwkv0, shift0 = init
    nseq = gdn0.shape[0]

    def chunk_map(i, seq, fst):
        return (i, 0)

    def seq_map4(i, seq, fst):
        return (seq[i], 0, 0, 0)

    def seq_map3(i, seq, fst):
        return (seq[i], 0, 0)

    def const2(i, seq, fst):
        return (0, 0)

    in_specs = [
        pl.BlockSpec((CHUNK, proj.shape[1]), chunk_map),
        pl.BlockSpec((1,) + gdn0.shape[1:], seq_map4),
        pl.BlockSpec((1,) + gdnc0.shape[1:], seq_map3),
        pl.BlockSpec((1,) + ssm0.shape[1:], seq_map4),
        pl.BlockSpec((1,) + ssmc0.shape[1:], seq_map3),
        pl.BlockSpec((1,) + rwkv0.shape[1:], seq_map4),
        pl.BlockSpec((1,) + shift0.shape[1:], seq_map3),
    ] + [pl.BlockSpec(p.shape, const2) for p in prm]
    out_specs = [
        pl.BlockSpec((CHUNK, cfg.mix_width), chunk_map),
        pl.BlockSpec((1,) + gdn0.shape[1:], seq_map4),
        pl.BlockSpec((1,) + ssm0.shape[1:], seq_map4),
        pl.BlockSpec((1,) + rwkv0.shape[1:], seq_map4),
    ]
    out_shape = [
        jax.ShapeDtypeStruct((n, cfg.mix_width), BF16),
        jax.ShapeDtypeStruct(gdn0.shape, F32),
        jax.ShapeDtypeStruct(ssm0.shape, F32),
        jax.ShapeDtypeStruct(rwkv0.shape, F32),
    ]
    scratch = [
        pltpu.VMEM(gdn0.shape[1:], F32),
        pltpu.VMEM(ssm0.shape[1:], F32),
        pltpu.VMEM(rwkv0.shape[1:], F32),
        pltpu.VMEM((HIST + CHUNK, gdnc0.shape[2]), F32),
        pltpu.VMEM((HIST + CHUNK, ssmc0.shape[2]), F32),
        pltpu.VMEM((HIST + CHUNK, shift0.shape[2]), F32),
    ]
    del nseq
    return pl.pallas_call(
        functools.partial(_mixer_body, cfg=cfg),
        grid_spec=pltpu.PrefetchScalarGridSpec(
            num_scalar_prefetch=2, grid=(nchunks,), in_specs=in_specs, out_specs=out_specs,
            scratch_shapes=scratch),
        out_shape=out_shape,
        compiler_params=pltpu.CompilerParams(
            dimension_semantics=("arbitrary",), vmem_limit_bytes=VMEM_LIMIT_BYTES),
        name="mixers",
    )(seq_id, first, proj, gdn0, gdnc0, ssm0, ssmc0, rwkv0, shift0, *prm)


def _pad_lanes(v, width=LANE):
    v = v.reshape(1, -1)
    return jnp.pad(v, ((0, 0), (0, width - v.shape[1])))


def kernel(x_prompt, x_sample, state_gdn, state_gdn_conv, state_ssm, state_ssm_conv, state_rwkv, state_rwkv_shift, norm1_w, w_in, gdn_conv_w, gdn_A_log, gdn_dt_bias, gdn_norm_w, ssm_conv_w, ssm_conv_b, ssm_A_log, ssm_dt_bias, ssm_D, ssm_norm_w, rwkv_mu, rwkv_w0, rwkv_w_up, rwkv_a0, rwkv_a_up, rwkv_g_up, rwkv_k_k, rwkv_k_a, rwkv_r_k, rwkv_ln_w, rwkv_ln_b, w_out, norm2_w, ffn_w_gate, ffn_w_up, ffn_w_down, final_norm_w):
    depth = w_in.shape[0]
    nbp, tp, d = x_prompt.shape
    nbs, ts, _ = x_sample.shape
    hd = state_gdn.shape[-1]
    ssm_width = state_ssm.shape[2] * hd
    cfg = Cfg(
        d_model=d, head_dim=hd, conv_w=gdn_conv_w.shape[1],
        gdn_heads=state_gdn.shape[2], ssm_heads=state_ssm.shape[2],
        ssm_groups=(ssm_conv_w.shape[2] - ssm_width) // (2 * state_ssm.shape[-1]),
        ssm_state=state_ssm.shape[-1], rwkv_heads=state_rwkv.shape[2],
        lora_w=rwkv_w_up.shape[1], lora_a=rwkv_a_up.shape[1], lora_g=rwkv_g_up.shape[1])
    assert tp % CHUNK == 0 and ts % CHUNK == 0 and tp >= cfg.conv_w and ts >= cfg.conv_w
    assert 2 * cfg.gdn_heads + cfg.ssm_heads <= LANE
    gw, sw = cfg.gdn_width, cfg.ssm_width
    gdn_cols = 4 * gw + 2 * cfg.gdn_heads
    ssm_cols = 2 * sw + 2 * cfg.ssm_bc + cfg.ssm_heads
    assert w_in.shape[2] == gdn_cols + ssm_cols + cfg.rwkv_cols

    x = jnp.concatenate([x_prompt.reshape(nbp * tp, d), x_sample.reshape(nbs * ts, d)], axis=0)
    n = x.shape[0]
    assert n % FFN_ROWS == 0 and n % PROJ_ROWS == 0
    seq_len = [tp] * nbp + [ts] * nbs
    seq_id, first = [], []
    for s, length in enumerate(seq_len):
        for j in range(length // CHUNK):
            seq_id.append(s)
            first.append(1 if j == 0 else 0)
    seq_id = jnp.asarray(seq_id, jnp.int32)
    first = jnp.asarray(first, jnp.int32)

    def with_prompt_zeros(st):
        return jnp.concatenate([jnp.zeros((nbp,) + st.shape[1:], F32), st.astype(F32)], axis=0)

    ga = 4 * gw
    sa = gdn_cols + 2 * sw + 2 * cfg.ssm_bc
    small_w = jnp.concatenate([w_in[:, :, ga:ga + 2 * cfg.gdn_heads], w_in[:, :, sa:sa + cfg.ssm_heads]], axis=2)
    small_w = jnp.pad(small_w, ((0, 0), (0, 0), (0, LANE - small_w.shape[2])))
    w_proj = jnp.concatenate(
        [w_in[:, :, :ga], w_in[:, :, gdn_cols:gdn_cols + 2 * sw + 2 * cfg.ssm_bc],
         w_in[:, :, gdn_cols + ssm_cols:], small_w], axis=2).astype(BF16)
    assert w_proj.shape[2] == cfg.proj_cols

    def small_row(l, gdn_first, gdn_second, ssm_part):
        return _pad_lanes(jnp.concatenate([gdn_first[l], gdn_second, ssm_part[l]]))

    zeros_g = jnp.zeros((cfg.gdn_heads,), F32)
    wo_b, wg_b, wu_b, wd_b = (w.astype(BF16) for w in (w_out, ffn_w_gate, ffn_w_up, ffn_w_down))

    new_states = [[] for _ in range(6)]
    for l in range(depth):
        proj = _proj_call(x, norm1_w[l], w_proj[l])
        init = tuple(with_prompt_zeros(st[l]) for st in
                     (state_gdn, state_gdn_conv, state_ssm, state_ssm_conv, state_rwkv, state_rwkv_shift))
        prm = (
            gdn_conv_w[l], small_row(l, gdn_A_log, zeros_g, ssm_A_log), small_row(l, gdn_dt_bias, zeros_g, ssm_dt_bias),
            gdn_norm_w[l].reshape(1, -1),
            ssm_conv_w[l], ssm_conv_b[l].reshape(1, -1), ssm_D[l].reshape(1, -1), ssm_norm_w[l].reshape(1, -1),
            rwkv_mu[l].reshape(1, -1), rwkv_w0[l].reshape(1, -1), rwkv_w_up[l], rwkv_a0[l].reshape(1, -1),
            rwkv_a_up[l], rwkv_g_up[l], rwkv_k_k[l].reshape(1, -1), rwkv_k_a[l].reshape(1, -1),
            rwkv_r_k[l].reshape(1, -1), rwkv_ln_w[l].reshape(1, -1), rwkv_ln_b[l].reshape(1, -1),
        )
        mix, gdn_s, ssm_s, rwkv_s = _mixer_call(cfg, seq_id, first, proj, init, prm)
        x = _ffn_call(x, mix, wo_b[l], norm2_w[l], wg_b[l], wu_b[l], wd_b[l], final_norm_w, final=(l == depth - 1))

        pp = proj[:nbp * tp].reshape(nbp, tp, -1)
        ps = proj[nbp * tp:].reshape(nbs, ts, -1)
        k1 = cfg.conv_w - 1
        new_states[0].append(gdn_s)
        new_states[1].append((pp[:, tp - k1:, cfg.o_gdn_qkv:cfg.o_gdn_qkv + 3 * gw],
                              ps[:, ts - k1:, cfg.o_gdn_qkv:cfg.o_gdn_qkv + 3 * gw]))
        new_states[2].append(ssm_s)
        new_states[3].append((pp[:, tp - k1:, cfg.o_ssm_xbc:cfg.o_ssm_xbc + sw + 2 * cfg.ssm_bc],
                              ps[:, ts - k1:, cfg.o_ssm_xbc:cfg.o_ssm_xbc + sw + 2 * cfg.ssm_bc]))
        new_states[4].append(rwkv_s)
        new_states[5].append((pp[:, tp - 1:, cfg.o_rwkv:cfg.o_rwkv + cfg.rwkv_cols],
                              ps[:, ts - 1:, cfg.o_rwkv:cfg.o_rwkv + cfg.rwkv_cols]))

    y_prompt = x[:nbp * tp].reshape(nbp, tp, d)
    y_sample = x[nbp * tp:].reshape(nbs, ts, d)
    p_out, s_out = [], []
    for idx in range(6):
        if idx % 2 == 0:
            stacked = jnp.stack(new_states[idx])
            p_out.append(stacked[:, :nbp])
            s_out.append(stacked[:, nbp:])
        else:
            p_out.append(jnp.stack([a for a, _ in new_states[idx]]))
            s_out.append(jnp.stack([b for _, b in new_states[idx]]))
    return (y_prompt, y_sample, *p_out, *s_out)
```

```python
import functools
import math
from typing import NamedTuple

import jax
import jax.numpy as jnp
from jax import lax
from jax.experimental import pallas as pl
from jax.experimental.pallas import tpu as pltpu

F32 = jnp.float32
BF16 = jnp.bfloat16

CHUNK = 64
NORM_EPS = 1e-6
RWKV_GN_EPS = 64e-5
LANE = 128
HIST = 8
VMEM_LIMIT_BYTES = 56 * 1024 * 1024
PROJ_ROWS = 256
FFN_ROWS = 512
FFN_COLS = 256


class Cfg(NamedTuple):
    d_model: int
    head_dim: int
    conv_w: int
    gdn_heads: int
    ssm_heads: int
    ssm_groups: int
    ssm_state: int
    rwkv_heads: int
    lora_w: int
    lora_a: int
    lora_g: int

    @property
    def gdn_width(self):
        return self.gdn_heads * self.head_dim

    @property
    def ssm_width(self):
        return self.ssm_heads * self.head_dim

    @property
    def ssm_bc(self):
        return self.ssm_groups * self.ssm_state

    @property
    def rwkv_width(self):
        return self.rwkv_heads * self.head_dim

    @property
    def rwkv_cols(self):
        return 3 * self.rwkv_width + self.lora_w + self.lora_a + self.lora_g

    @property
    def o_gdn_qkv(self):
        return 0

    @property
    def o_gdn_z(self):
        return 3 * self.gdn_width

    @property
    def o_ssm_z(self):
        return self.o_gdn_z + self.gdn_width

    @property
    def o_ssm_xbc(self):
        return self.o_ssm_z + self.ssm_width

    @property
    def o_rwkv(self):
        return self.o_ssm_xbc + self.ssm_width + 2 * self.ssm_bc

    @property
    def o_small(self):
        return self.o_rwkv + self.rwkv_cols

    @property
    def proj_cols(self):
        return self.o_small + LANE

    @property
    def mix_width(self):
        return self.gdn_width + self.ssm_width + self.rwkv_width


def _rms(x, w):
    return x * lax.rsqrt(jnp.mean(x * x, axis=-1, keepdims=True) + NORM_EPS) * w


def _softplus(x):
    return jnp.maximum(x, 0.0) + jnp.log1p(jnp.exp(-jnp.abs(x)))


def _silu(x):
    return x * jax.nn.sigmoid(x)


def _mm(a, b):
    return jnp.dot(a.astype(BF16), b.astype(BF16), preferred_element_type=F32)


def _mm_nt(a, b):
    return lax.dot_general(a.astype(BF16), b.astype(BF16), (((1,), (1,)), ((), ())),
                           preferred_element_type=F32)


def _mm_tn(a, b):
    return lax.dot_general(a.astype(BF16), b.astype(BF16), (((0,), (0,)), ((), ())),
                           preferred_element_type=F32)


def _cumsum_rows(tri, x):
    hi = x.astype(BF16)
    lo = (x - hi.astype(F32)).astype(BF16)
    return jnp.dot(tri, hi, preferred_element_type=F32) + jnp.dot(tri, lo, preferred_element_type=F32)


def _inv_one_minus_steps(n, eye_f):
    c = n.shape[0]
    t = eye_f + n
    p = _mm(n, n)
    yield
    for _ in range(int(math.log2(c)) - 2):
        step = _mm(t, p)
        p_next = _mm(p, p)
        yield
        t = t + step
        p = p_next
    step = _mm(t, p)
    yield
    return t + step


def _run_interleaved(tasks):
    tasks = list(tasks)
    while tasks:
        alive = []
        for task in tasks:
            try:
                spawned = next(task)
            except StopIteration:
                continue
            alive.append(task)
            if spawned:
                alive.extend(spawned)
        tasks = alive


def _col_and_row(col, eye):
    c = col.shape[0]
    cb = jnp.broadcast_to(col, (c, c))
    row = jnp.sum(jnp.where(eye, cb, 0.0), axis=0, keepdims=True)
    return cb, row


def _conv_chunk(buf_ref, x, w_ref, conv_w):
    c = x.shape[0]
    buf_ref[HIST:HIST + c, :] = x
    y = x * w_ref[conv_w - 1:conv_w, :]
    for j in range(conv_w - 1):
        lo = HIST - (conv_w - 1) + j
        y = y + buf_ref[lo:lo + c, :] * w_ref[j:j + 1, :]
    tail = buf_ref[HIST + c - (conv_w - 1):HIST + c, :]
    buf_ref[HIST - (conv_w - 1):HIST, :] = tail
    return y


def _proj_body(x_ref, nw_ref, w_ref, o_ref):
    h = _rms(x_ref[...], nw_ref[...])
    o_ref[...] = jnp.dot(h.astype(BF16), w_ref[...], preferred_element_type=F32)


def _proj_call(x, norm_w, w_bf16):
    n, d = x.shape
    cols = w_bf16.shape[1]
    rows = PROJ_ROWS
    assert n % rows == 0
    return pl.pallas_call(
        _proj_body,
        grid=(n // rows,),
        in_specs=[
            pl.BlockSpec((rows, d), lambda i: (i, 0)),
            pl.BlockSpec((1, d), lambda i: (0, 0)),
            pl.BlockSpec((d, cols), lambda i: (0, 0)),
        ],
        out_specs=pl.BlockSpec((rows, cols), lambda i: (i, 0)),
        out_shape=jax.ShapeDtypeStruct((n, cols), F32),
        compiler_params=pltpu.CompilerParams(
            dimension_semantics=("arbitrary",), vmem_limit_bytes=VMEM_LIMIT_BYTES),
        name="norm_proj",
    )(x, norm_w.reshape(1, d), w_bf16)


def _ffn_body(x_ref, mix_ref, wo_ref, n2_ref, wg_ref, wu_ref, wd_ref, fn_ref, o_ref, *, final):
    x = x_ref[...] + jnp.dot(mix_ref[...], wo_ref[...], preferred_element_type=F32)
    h2 = _rms(x, n2_ref[...]).astype(BF16)
    hidden = wg_ref.shape[1]
    acc = x
    for c0 in range(0, hidden, FFN_COLS):
        g = jnp.dot(h2, wg_ref[:, c0:c0 + FFN_COLS], preferred_element_type=F32)
        u = jnp.dot(h2, wu_ref[:, c0:c0 + FFN_COLS], preferred_element_type=F32)
        ff = (_silu(g) * u).astype(BF16)
        acc = acc + jnp.dot(ff, wd_ref[c0:c0 + FFN_COLS, :], preferred_element_type=F32)
    if final:
        acc = _rms(acc, fn_ref[...])
    o_ref[...] = acc


def _ffn_call(x, mix, wo, n2, wg, wu, wd, fn, final):
    n, d = x.shape
    hidden = wg.shape[1]
    rows = FFN_ROWS
    assert n % rows == 0 and hidden % FFN_COLS == 0
    const = lambda i: (0, 0)
    return pl.pallas_call(
        functools.partial(_ffn_body, final=final),
        grid=(n // rows,),
        in_specs=[
            pl.BlockSpec((rows, d), lambda i: (i, 0)),
            pl.BlockSpec((rows, mix.shape[1]), lambda i: (i, 0)),
            pl.BlockSpec(wo.shape, const),
            pl.BlockSpec((1, d), const),
            pl.BlockSpec(wg.shape, const),
            pl.BlockSpec(wu.shape, const),
            pl.BlockSpec(wd.shape, const),
            pl.BlockSpec((1, d), const),
        ],
        out_specs=pl.BlockSpec((rows, d), lambda i: (i, 0)),
        out_shape=jax.ShapeDtypeStruct((n, d), F32),
        compiler_params=pltpu.CompilerParams(
            dimension_semantics=("arbitrary",), vmem_limit_bytes=VMEM_LIMIT_BYTES),
        name="outproj_ffn",
    )(x, mix, wo, n2.reshape(1, d), wg, wu, wd, fn.reshape(1, d))


def _gdn_head(cfg, h, qkv, z, gc, beta, eg_all, s_ref, norm_w, masks, outs):
    eye, eye_f, incl, strict = masks
    hd, nh = cfg.head_dim, cfg.gdn_heads
    c = qkv.shape[0]
    q = qkv[:, h * hd:(h + 1) * hd]
    k = qkv[:, (nh + h) * hd:(nh + h + 1) * hd]
    v = qkv[:, (2 * nh + h) * hd:(2 * nh + h + 1) * hd]
    q = q * lax.rsqrt(jnp.sum(q * q, axis=-1, keepdims=True) + NORM_EPS) * (hd ** -0.5)
    k = k * lax.rsqrt(jnp.sum(k * k, axis=-1, keepdims=True) + NORM_EPS)
    b = beta[:, h:h + 1]
    kb = k * b
    vb = v * b
    kq = _mm_nt(jnp.concatenate([kb, q], axis=0), k)
    yield
    gcol = gc[:, h:h + 1]
    gcb, grow = _col_and_row(gcol, eye)
    dec = jnp.exp(jnp.minimum(gcb - grow, 0.0))
    lower = jnp.where(strict, kq[:c] * dec, 0.0)
    attn = jnp.where(incl, kq[c:] * dec, 0.0)
    t = yield from _inv_one_minus_steps(-lower, eye_f)
    eg = eg_all[:, h:h + 1]
    u = _mm(t, vb)
    w = _mm(t, kb * eg)
    yield
    s = s_ref[h]
    wq_s = _mm(jnp.concatenate([w, q * eg], axis=0), s)
    yield
    g_last = gcol[c - 1:c, :]
    v_new = u - wq_s[:c]
    o = wq_s[c:] + _mm(attn, v_new)
    s_ref[h] = s * jnp.exp(g_last) + _mm_tn(k * jnp.exp(g_last - gcol), v_new)
    yield
    outs[h] = _rms(o, norm_w) * _silu(z[:, h * hd:(h + 1) * hd])


def _ssd_head(cfg, h, delay, xbc, ac, dt, s_ref, d_row, masks, cb_cache, ys):
    eye, _, incl, _ = masks
    hd, ns, width = cfg.head_dim, cfg.ssm_state, cfg.ssm_width
    c = xbc.shape[0]
    for _ in range(delay):
        yield
    g = h // (cfg.ssm_heads // cfg.ssm_groups)
    bm = xbc[:, width + g * ns:width + (g + 1) * ns]
    cm = xbc[:, width + cfg.ssm_bc + g * ns:width + cfg.ssm_bc + (g + 1) * ns]
    if g not in cb_cache:
        cb_cache[g] = _mm_nt(cm, bm)
    acol = ac[:, h:h + 1]
    a_last = acol[c - 1:c, :]
    x = xbc[:, h * hd:(h + 1) * hd]
    xdt = x * dt[:, h:h + 1]
    s = s_ref[h]
    y_off = _mm_nt(cm, s)
    s_ref[h] = s * jnp.exp(a_last) + _mm_tn(xdt * jnp.exp(a_last - acol), bm)
    yield
    acb, arow = _col_and_row(acol, eye)
    lmat = jnp.exp(jnp.minimum(acb - arow, 0.0))
    y_diag = _mm(jnp.where(incl, cb_cache[g] * lmat, 0.0), xdt)
    yield
    ys[h] = y_diag + y_off * jnp.exp(acol) + d_row[:, h:h + 1] * x


def _rwkv_head(cfg, h, shared, s_ref, prm, masks, incl2, outs):
    _, eye_f, _, strict = masks
    r, k2, v, kk_raw, iclr, gate, pprev, pinv, pc, rt, kt = shared
    (_, _, _, _, _, _, _, r_k, ln_w, ln_b) = prm
    hd = cfg.head_dim
    c = r.shape[0]
    sl = slice(h * hd, (h + 1) * hd)
    kkh = kk_raw[:, sl]
    kkh = kkh * lax.rsqrt(jnp.sum(kkh * kkh, axis=-1, keepdims=True) + NORM_EPS)
    at = -kkh * pprev[:, sl]
    bt = kkh * iclr[:, sl] * pinv[:, sl]
    vh = v[:, sl]
    pch = pc[:, sl]
    s = s_ref[h]
    ar = jnp.concatenate([at, rt[:, sl]], axis=0)
    bk = jnp.concatenate([bt, kt[:, sl]], axis=0)
    cross = _mm_nt(ar, bk)
    ar_s = _mm_nt(ar, s)
    yield
    a_ab = jnp.where(strict, cross[:c, :c], 0.0)
    a_ak = jnp.where(strict, cross[:c, c:], 0.0)
    aakv = _mm(a_ak, vh)
    t = yield from _inv_one_minus_steps(a_ab, eye_f)
    u = _mm(t, ar_s[:c] + aakv)
    yield
    uv = jnp.concatenate([u, vh], axis=0)
    y = ar_s[c:] + _mm(jnp.where(incl2, cross[c:], 0.0), uv)
    s_ref[h] = s * pch + _mm_tn(uv, bk * pch)
    yield
    mean = jnp.mean(y, axis=-1, keepdims=True)
    yc = y - mean
    var = jnp.mean(yc * yc, axis=-1, keepdims=True)
    yn = yc * lax.rsqrt(var + RWKV_GN_EPS) * ln_w[:, sl] + ln_b[:, sl]
    bonus = jnp.sum(r[:, sl] * k2[:, sl] * r_k[:, sl], axis=-1, keepdims=True) * vh
    outs[h] = (yn + bonus) * gate[:, sl]


def _rwkv_front(cfg, xm, s_ref, prm, tri, masks, outs):
    (w0, w_up, a0, a_up, g_up, k_k, k_a, _, _, _) = prm
    wd = cfg.rwkv_width
    c = xm.shape[0]
    c0 = 3 * wd
    c1 = c0 + cfg.lora_w
    c2 = c1 + cfg.lora_a
    r = xm[:, :wd]
    k = xm[:, wd:2 * wd]
    v = xm[:, 2 * wd:c0]
    lora_w = _mm(jnp.tanh(xm[:, c0:c1]), w_up)
    lora_a = _mm(xm[:, c1:c2], a_up)
    gate = _mm(jax.nn.sigmoid(xm[:, c2:]), g_up)
    yield
    w_log = -_softplus(-(w0 + lora_w)) - 0.5
    logw = -jnp.exp(w_log)
    cum = _cumsum_rows(tri, logw)
    yield
    iclr = jax.nn.sigmoid(a0 + lora_a)
    k2 = k * (1.0 + (iclr - 1.0) * k_a)
    pm = jnp.exp(cum)
    pinv = jnp.exp(-cum)
    shared = (r, k2, v, k * k_k, iclr, gate, jnp.exp(cum - logw), pinv, pm[c - 1:c, :], r * pm, k2 * pinv)
    ri2 = lax.broadcasted_iota(jnp.int32, (c, 2 * c), 0)
    ci2 = lax.broadcasted_iota(jnp.int32, (c, 2 * c), 1)
    incl2 = ri2 >= jnp.where(ci2 >= c, ci2 - c, ci2)
    yield [_rwkv_head(cfg, h, shared, s_ref, prm, masks, incl2, outs) for h in range(cfg.rwkv_heads)]


def _mixer_body(seq_ref, first_ref,
                p_ref, gdn0_ref, gdnc0_ref, ssm0_ref, ssmc0_ref, rwkv0_ref, shift0_ref,
                gconv_w_ref, alog_ref, dtb_ref, gnorm_ref,
                sconv_w_ref, sconv_b_ref, sd_ref, snorm_ref,
                mu_ref, w0_ref, wup_ref, a0_ref, aup_ref, gup_ref, kk_ref, ka_ref, rk_ref, lnw_ref, lnb_ref,
                mix_ref, gdn_out_ref, gdnc_out_ref, ssm_out_ref, ssmc_out_ref, rwkv_out_ref, shift_out_ref,
                gdn_s, ssm_s, rwkv_s, gbuf, sbuf, rbuf, *, cfg):
    del seq_ref
    i = pl.program_id(0)
    c = p_ref.shape[0]
    cw = cfg.conv_w

    @pl.when(first_ref[i] == 1)
    def _load_state():
        gdn_s[...] = gdn0_ref[0]
        ssm_s[...] = ssm0_ref[0]
        rwkv_s[...] = rwkv0_ref[0]
        gbuf[HIST - (cw - 1):HIST, :] = gdnc0_ref[0]
        sbuf[HIST - (cw - 1):HIST, :] = ssmc0_ref[0]
        rbuf[HIST - 1:HIST, :] = shift0_ref[0]

    ri = lax.broadcasted_iota(jnp.int32, (c, c), 0)
    ci = lax.broadcasted_iota(jnp.int32, (c, c), 1)
    eye = ri == ci
    incl = ri >= ci
    strict = ri > ci
    eye_f = eye.astype(F32)
    tri = incl.astype(BF16)
    masks = (eye, eye_f, incl, strict)

    gh, sh = cfg.gdn_heads, cfg.ssm_heads
    small = p_ref[:, cfg.o_small:cfg.o_small + LANE]
    sp = _softplus(small + dtb_ref[...])
    cum = _cumsum_rows(tri, sp * (-jnp.exp(alog_ref[...])))
    gc = cum[:, 0:gh]
    ac = cum[:, 2 * gh:2 * gh + sh]
    beta = jax.nn.sigmoid(small[:, gh:2 * gh])
    dt = sp[:, 2 * gh:2 * gh + sh]
    eg = jnp.exp(gc)

    pr = p_ref[:, cfg.o_rwkv:cfg.o_rwkv + cfg.rwkv_cols]
    rbuf[HIST:HIST + c, :] = pr
    prev = rbuf[HIST - 1:HIST - 1 + c, :]
    rbuf[HIST - 1:HIST, :] = pr[c - 1:c, :]
    xm = pr + (prev - pr) * mu_ref[...]
    prm = (w0_ref[...], wup_ref[...], a0_ref[...], aup_ref[...], gup_ref[...], kk_ref[...], ka_ref[...],
           rk_ref[...], lnw_ref[...], lnb_ref[...])

    qkv = _silu(_conv_chunk(gbuf, p_ref[:, cfg.o_gdn_qkv:cfg.o_gdn_qkv + 3 * cfg.gdn_width], gconv_w_ref, cw))
    xbc = _conv_chunk(sbuf, p_ref[:, cfg.o_ssm_xbc:cfg.o_ssm_xbc + cfg.ssm_width + 2 * cfg.ssm_bc], sconv_w_ref, cw)
    xbc = _silu(xbc + sconv_b_ref[...])
    gz = p_ref[:, cfg.o_gdn_z:cfg.o_gdn_z + cfg.gdn_width]
    sz = p_ref[:, cfg.o_ssm_z:cfg.o_ssm_z + cfg.ssm_width]
    gnorm = gnorm_ref[...]
    sd = sd_ref[...]

    gdn_o = [None] * gh
    ssd_y = [None] * sh
    rwkv_o = [None] * cfg.rwkv_heads
    cb_cache = {}
    tasks = [_rwkv_front(cfg, xm, rwkv_s, prm, tri, masks, rwkv_o)]
    tasks += [_gdn_head(cfg, h, qkv, gz, gc, beta, eg, gdn_s, gnorm, masks, gdn_o) for h in range(gh)]
    tasks += [_ssd_head(cfg, h, h, xbc, ac, dt, ssm_s, sd, masks, cb_cache, ssd_y) for h in range(sh)]
    _run_interleaved(tasks)

    ng = cfg.ssm_groups
    gw = cfg.ssm_width // ng
    snorm = snorm_ref[...]
    ssd_o = []
    for g in range(ng):
        yg = jnp.concatenate(ssd_y[g * (sh // ng):(g + 1) * (sh // ng)], axis=-1)
        yg = yg * _silu(sz[:, g * gw:(g + 1) * gw])
        yg = yg * lax.rsqrt(jnp.mean(yg * yg, axis=-1, keepdims=True) + NORM_EPS)
        ssd_o.append(yg * snorm[:, g * gw:(g + 1) * gw])

    mix_ref[...] = jnp.concatenate(gdn_o + ssd_o + rwkv_o, axis=-1).astype(mix_ref.dtype)
    gdn_out_ref[0] = gdn_s[...]
    ssm_out_ref[0] = ssm_s[...]
    rwkv_out_ref[0] = rwkv_s[...]
    gdnc_out_ref[0] = gbuf[HIST - (cw - 1):HIST, :]
    ssmc_out_ref[0] = sbuf[HIST - (cw - 1):HIST, :]
    shift_out_ref[0] = rbuf[HIST - 1:HIST, :]


def _mixer_call(cfg, seq_id, first, proj, init, prm):
    n = proj.shape[0]
    nchunks = n // CHUNK
    gdn0, gdnc0, ssm0, ssmc0, rwkv0, shift0 = init

    def chunk_map(i, seq, fst):
        return (i, 0)

    def seq_map4(i, seq, fst):
        return (seq[i], 0, 0, 0)

    def seq_map3(i, seq, fst):
        return (seq[i], 0, 0)

    def const2(i, seq, fst):
        return (0, 0)

    def seq_spec(st):
        return pl.BlockSpec((1,) + st.shape[1:], seq_map4 if st.ndim == 4 else seq_map3)

    in_specs = ([pl.BlockSpec((CHUNK, proj.shape[1]), chunk_map)] + [seq_spec(st) for st in init]
                + [pl.BlockSpec(p.shape, const2) for p in prm])
    out_specs = [pl.BlockSpec((CHUNK, cfg.mix_width), chunk_map)] + [seq_spec(st) for st in init]
    out_shape = [jax.ShapeDtypeStruct((n, cfg.mix_width), BF16)] + [
        jax.ShapeDtypeStruct(st.shape, F32) for st in init]
    scratch = [
        pltpu.VMEM(gdn0.shape[1:], F32),
        pltpu.VMEM(ssm0.shape[1:], F32),
        pltpu.VMEM(rwkv0.shape[1:], F32),
        pltpu.VMEM((HIST + CHUNK, gdnc0.shape[2]), F32),
        pltpu.VMEM((HIST + CHUNK, ssmc0.shape[2]), F32),
        pltpu.VMEM((HIST + CHUNK, shift0.shape[2]), F32),
    ]
    return pl.pallas_call(
        functools.partial(_mixer_body, cfg=cfg),
        grid_spec=pltpu.PrefetchScalarGridSpec(
            num_scalar_prefetch=2, grid=(nchunks,), in_specs=in_specs, out_specs=out_specs,
            scratch_shapes=scratch),
        out_shape=out_shape,
        compiler_params=pltpu.CompilerParams(
            dimension_semantics=("arbitrary",), vmem_limit_bytes=VMEM_LIMIT_BYTES),
        name="mixers",
    )(seq_id, first, proj, *init, *prm)


def _pad_lanes(v, width=LANE):
    v = v.reshape(1, -1)
    return jnp.pad(v, ((0, 0), (0, width - v.shape[1])))


def kernel(x_prompt, x_sample, state_gdn, state_gdn_conv, state_ssm, state_ssm_conv, state_rwkv, state_rwkv_shift, norm1_w, w_in, gdn_conv_w, gdn_A_log, gdn_dt_bias, gdn_norm_w, ssm_conv_w, ssm_conv_b, ssm_A_log, ssm_dt_bias, ssm_D, ssm_norm_w, rwkv_mu, rwkv_w0, rwkv_w_up, rwkv_a0, rwkv_a_up, rwkv_g_up, rwkv_k_k, rwkv_k_a, rwkv_r_k, rwkv_ln_w, rwkv_ln_b, w_out, norm2_w, ffn_w_gate, ffn_w_up, ffn_w_down, final_norm_w):
    depth = w_in.shape[0]
    nbp, tp, d = x_prompt.shape
    nbs, ts, _ = x_sample.shape
    hd = state_gdn.shape[-1]
    ssm_width = state_ssm.shape[2] * hd
    cfg = Cfg(
        d_model=d, head_dim=hd, conv_w=gdn_conv_w.shape[1],
        gdn_heads=state_gdn.shape[2], ssm_heads=state_ssm.shape[2],
        ssm_groups=(ssm_conv_w.shape[2] - ssm_width) // (2 * state_ssm.shape[-1]),
        ssm_state=state_ssm.shape[-1], rwkv_heads=state_rwkv.shape[2],
        lora_w=rwkv_w_up.shape[1], lora_a=rwkv_a_up.shape[1], lora_g=rwkv_g_up.shape[1])
    assert tp % CHUNK == 0 and ts % CHUNK == 0 and tp >= cfg.conv_w and ts >= cfg.conv_w
    assert 2 * cfg.gdn_heads + cfg.ssm_heads <= LANE
    gw, sw = cfg.gdn_width, cfg.ssm_width
    gdn_cols = 4 * gw + 2 * cfg.gdn_heads
    ssm_cols = 2 * sw + 2 * cfg.ssm_bc + cfg.ssm_heads
    assert w_in.shape[2] == gdn_cols + ssm_cols + cfg.rwkv_cols

    x = jnp.concatenate([x_prompt.reshape(nbp * tp, d), x_sample.reshape(nbs * ts, d)], axis=0)
    n = x.shape[0]
    assert n % FFN_ROWS == 0 and n % PROJ_ROWS == 0
    seq_len = [tp] * nbp + [ts] * nbs
    seq_id, first = [], []
    for s, length in enumerate(seq_len):
        for j in range(length // CHUNK):
            seq_id.append(s)
            first.append(1 if j == 0 else 0)
    seq_id = jnp.asarray(seq_id, jnp.int32)
    first = jnp.asarray(first, jnp.int32)

    def with_prompt_zeros(st):
        return jnp.concatenate([jnp.zeros((nbp,) + st.shape[1:], F32), st.astype(F32)], axis=0)

    ga = 4 * gw
    sa = gdn_cols + 2 * sw + 2 * cfg.ssm_bc
    small_w = jnp.concatenate([w_in[:, :, ga:ga + 2 * cfg.gdn_heads], w_in[:, :, sa:sa + cfg.ssm_heads]], axis=2)
    small_w = jnp.pad(small_w, ((0, 0), (0, 0), (0, LANE - small_w.shape[2])))
    w_proj = jnp.concatenate(
        [w_in[:, :, :ga], w_in[:, :, gdn_cols:gdn_cols + 2 * sw + 2 * cfg.ssm_bc],
         w_in[:, :, gdn_cols + ssm_cols:], small_w], axis=2).astype(BF16)
    assert w_proj.shape[2] == cfg.proj_cols

    def small_row(l, gdn_first, gdn_second, ssm_part):
        return _pad_lanes(jnp.concatenate([gdn_first[l], gdn_second, ssm_part[l]]))

    zeros_g = jnp.zeros((cfg.gdn_heads,), F32)
    wo_b, wg_b, wu_b, wd_b = (w.astype(BF16) for w in (w_out, ffn_w_gate, ffn_w_up, ffn_w_down))

    new_states = [[] for _ in range(6)]
    for l in range(depth):
        proj = _proj_call(x, norm1_w[l], w_proj[l])
        init = tuple(with_prompt_zeros(st[l]) for st in
                     (state_gdn, state_gdn_conv, state_ssm, state_ssm_conv, state_rwkv, state_rwkv_shift))
        prm = (
            gdn_conv_w[l], small_row(l, gdn_A_log, zeros_g, ssm_A_log), small_row(l, gdn_dt_bias, zeros_g, ssm_dt_bias),
            gdn_norm_w[l].reshape(1, -1),
            ssm_conv_w[l], ssm_conv_b[l].reshape(1, -1), ssm_D[l].reshape(1, -1), ssm_norm_w[l].reshape(1, -1),
            rwkv_mu[l].reshape(1, -1), rwkv_w0[l].reshape(1, -1), rwkv_w_up[l], rwkv_a0[l].reshape(1, -1),
            rwkv_a_up[l], rwkv_g_up[l], rwkv_k_k[l].reshape(1, -1), rwkv_k_a[l].reshape(1, -1),
            rwkv_r_k[l].reshape(1, -1), rwkv_ln_w[l].reshape(1, -1), rwkv_ln_b[l].reshape(1, -1),
        )
        mix, *states = _mixer_call(cfg, seq_id, first, proj, init, prm)
        x = _ffn_call(x, mix, wo_b[l], norm2_w[l], wg_b[l], wu_b[l], wd_b[l], final_norm_w, final=(l == depth - 1))
        for acc, st in zip(new_states, states):
            acc.append(st)

    y_prompt = x[:nbp * tp].reshape(nbp, tp, d)
    y_sample = x[nbp * tp:].reshape(nbs, ts, d)
    stacked = [jnp.stack(st) for st in new_states]
    p_out = [st[:, :nbp] for st in stacked]
    s_out = [st[:, nbp:] for st in stacked]
    return (y_prompt, y_sample, *p_out, *s_out)
```

```python
import functools
import math
from typing import NamedTuple

import jax
import jax.numpy as jnp
from jax import lax
from jax.experimental import pallas as pl
from jax.experimental.pallas import tpu as pltpu

F32 = jnp.float32
BF16 = jnp.bfloat16

CHUNK = 64
NORM_EPS = 1e-6
RWKV_GN_EPS = 64e-5
LANE = 128
HIST = 8
VMEM_LIMIT_BYTES = 56 * 1024 * 1024
PROJ_ROWS = 256
FFN_ROWS = 512
FFN_COLS = 256


class Cfg(NamedTuple):
    d_model: int
    head_dim: int
    conv_w: int
    gdn_heads: int
    ssm_heads: int
    ssm_groups: int
    ssm_state: int
    rwkv_heads: int
    lora_w: int
    lora_a: int
    lora_g: int

    @property
    def gdn_width(self):
        return self.gdn_heads * self.head_dim

    @property
    def ssm_width(self):
        return self.ssm_heads * self.head_dim

    @property
    def ssm_bc(self):
        return self.ssm_groups * self.ssm_state

    @property
    def rwkv_width(self):
        return self.rwkv_heads * self.head_dim

    @property
    def rwkv_cols(self):
        return 3 * self.rwkv_width + self.lora_w + self.lora_a + self.lora_g

    @property
    def o_gdn_qkv(self):
        return 0

    @property
    def o_gdn_z(self):
        return 3 * self.gdn_width

    @property
    def o_ssm_z(self):
        return self.o_gdn_z + self.gdn_width

    @property
    def o_ssm_xbc(self):
        return self.o_ssm_z + self.ssm_width

    @property
    def o_rwkv(self):
        return self.o_ssm_xbc + self.ssm_width + 2 * self.ssm_bc

    @property
    def o_small(self):
        return self.o_rwkv + self.rwkv_cols

    @property
    def proj_cols(self):
        return self.o_small + LANE

    @property
    def mix_width(self):
        return self.gdn_width + self.ssm_width + self.rwkv_width


def _rms(x, w):
    return x * lax.rsqrt(jnp.mean(x * x, axis=-1, keepdims=True) + NORM_EPS) * w


def _softplus(x):
    return jnp.maximum(x, 0.0) + jnp.log1p(jnp.exp(-jnp.abs(x)))


def _silu(x):
    return x * jax.nn.sigmoid(x)


def _mm(a, b):
    return jnp.dot(a.astype(BF16), b.astype(BF16), preferred_element_type=F32)


def _mm_nt(a, b):
    return lax.dot_general(a.astype(BF16), b.astype(BF16), (((1,), (1,)), ((), ())),
                           preferred_element_type=F32)


def _mm_tn(a, b):
    return lax.dot_general(a.astype(BF16), b.astype(BF16), (((0,), (0,)), ((), ())),
                           preferred_element_type=F32)


def _cumsum_rows(tri, x):
    hi = x.astype(BF16)
    lo = (x - hi.astype(F32)).astype(BF16)
    return jnp.dot(tri, hi, preferred_element_type=F32) + jnp.dot(tri, lo, preferred_element_type=F32)


def _seg_sum(x, seg):
    hi = x.astype(BF16)
    lo = (x - hi.astype(F32)).astype(BF16)
    return jnp.dot(hi, seg, preferred_element_type=F32) + jnp.dot(lo, seg, preferred_element_type=F32)


def _inv_one_minus_steps(n, eye_f):
    c = n.shape[0]
    t = eye_f + n
    p = _mm(n, n)
    yield
    for _ in range(int(math.log2(c)) - 2):
        step = _mm(t, p)
        p_next = _mm(p, p)
        yield
        t = t + step
        p = p_next
    step = _mm(t, p)
    yield
    return t + step


def _run_interleaved(tasks):
    tasks = list(tasks)
    while tasks:
        alive = []
        for task in tasks:
            try:
                spawned = next(task)
            except StopIteration:
                continue
            alive.append(task)
            if spawned:
                alive.extend(spawned)
        tasks = alive


def _col_and_row(col, eye):
    c = col.shape[0]
    cb = jnp.broadcast_to(col, (c, c))
    row = jnp.sum(jnp.where(eye, cb, 0.0), axis=0, keepdims=True)
    return cb, row


def _conv_chunk(buf_ref, x, w_ref, conv_w):
    c = x.shape[0]
    buf_ref[HIST:HIST + c, :] = x
    y = x * w_ref[conv_w - 1:conv_w, :]
    for j in range(conv_w - 1):
        lo = HIST - (conv_w - 1) + j
        y = y + buf_ref[lo:lo + c, :] * w_ref[j:j + 1, :]
    tail = buf_ref[HIST + c - (conv_w - 1):HIST + c, :]
    buf_ref[HIST - (conv_w - 1):HIST, :] = tail
    return y


def _proj_body(x_ref, nw_ref, w_ref, o_ref):
    h = _rms(x_ref[...], nw_ref[...])
    o_ref[...] = jnp.dot(h.astype(BF16), w_ref[...], preferred_element_type=F32)


def _proj_call(x, norm_w, w_bf16):
    n, d = x.shape
    cols = w_bf16.shape[1]
    rows = PROJ_ROWS
    assert n % rows == 0
    return pl.pallas_call(
        _proj_body,
        grid=(n // rows,),
        in_specs=[
            pl.BlockSpec((rows, d), lambda i: (i, 0)),
            pl.BlockSpec((1, d), lambda i: (0, 0)),
            pl.BlockSpec((d, cols), lambda i: (0, 0)),
        ],
        out_specs=pl.BlockSpec((rows, cols), lambda i: (i, 0)),
        out_shape=jax.ShapeDtypeStruct((n, cols), F32),
        compiler_params=pltpu.CompilerParams(
            dimension_semantics=("arbitrary",), vmem_limit_bytes=VMEM_LIMIT_BYTES),
        name="norm_proj",
    )(x, norm_w.reshape(1, d), w_bf16)


def _ffn_body(x_ref, mix_ref, wo_ref, n2_ref, wg_ref, wu_ref, wd_ref, fn_ref, o_ref, *, final):
    x = x_ref[...] + jnp.dot(mix_ref[...], wo_ref[...], preferred_element_type=F32)
    h2 = _rms(x, n2_ref[...]).astype(BF16)
    hidden = wg_ref.shape[1]
    acc = x
    for c0 in range(0, hidden, FFN_COLS):
        g = jnp.dot(h2, wg_ref[:, c0:c0 + FFN_COLS], preferred_element_type=F32)
        u = jnp.dot(h2, wu_ref[:, c0:c0 + FFN_COLS], preferred_element_type=F32)
        ff = (_silu(g) * u).astype(BF16)
        acc = acc + jnp.dot(ff, wd_ref[c0:c0 + FFN_COLS, :], preferred_element_type=F32)
    if final:
        acc = _rms(acc, fn_ref[...])
    o_ref[...] = acc


def _ffn_call(x, mix, wo, n2, wg, wu, wd, fn, final):
    n, d = x.shape
    hidden = wg.shape[1]
    rows = FFN_ROWS
    assert n % rows == 0 and hidden % FFN_COLS == 0
    const = lambda i: (0, 0)
    return pl.pallas_call(
        functools.partial(_ffn_body, final=final),
        grid=(n // rows,),
        in_specs=[
            pl.BlockSpec((rows, d), lambda i: (i, 0)),
            pl.BlockSpec((rows, mix.shape[1]), lambda i: (i, 0)),
            pl.BlockSpec(wo.shape, const),
            pl.BlockSpec((1, d), const),
            pl.BlockSpec(wg.shape, const),
            pl.BlockSpec(wu.shape, const),
            pl.BlockSpec(wd.shape, const),
            pl.BlockSpec((1, d), const),
        ],
        out_specs=pl.BlockSpec((rows, d), lambda i: (i, 0)),
        out_shape=jax.ShapeDtypeStruct((n, d), F32),
        compiler_params=pltpu.CompilerParams(
            dimension_semantics=("arbitrary",), vmem_limit_bytes=VMEM_LIMIT_BYTES),
        name="outproj_ffn",
    )(x, mix, wo, n2.reshape(1, d), wg, wu, wd, fn.reshape(1, d))


def _gdn_head(cfg, h, q_all, k_all, kt_all, v_all, z, gc, gc_rows, beta, eg_all, s_ref, norm_w, masks, outs):
    _, eye_f, incl, strict = masks
    hd = cfg.head_dim
    c = q_all.shape[0]
    sl = slice(h * hd, (h + 1) * hd)
    q = q_all[:, sl]
    k = k_all[:, sl]
    kt = kt_all[sl, :]
    b = beta[:, h:h + 1]
    kb = k * b
    vb = v_all[:, sl] * b
    kq = _mm(jnp.concatenate([kb, q], axis=0), kt)
    yield
    gcol = gc[:, h:h + 1]
    grow = gc_rows[h:h + 1, :]
    dec = jnp.exp(jnp.minimum(gcol - grow, 0.0))
    lower = jnp.where(strict, kq[:c] * dec, 0.0)
    attn = jnp.where(incl, kq[c:] * dec, 0.0)
    t = yield from _inv_one_minus_steps(-lower, eye_f)
    eg = eg_all[:, h:h + 1]
    u = _mm(t, vb)
    w = _mm(t, kb * eg)
    yield
    s = s_ref[h]
    wq_s = _mm(jnp.concatenate([w, q * eg], axis=0), s)
    yield
    g_last = gcol[c - 1:c, :]
    v_new = u - wq_s[:c]
    o = wq_s[c:] + _mm(attn, v_new)
    s_ref[h] = s * jnp.exp(g_last) + _mm(kt * jnp.exp(g_last - grow), v_new)
    yield
    outs[h] = _rms(o, norm_w) * _silu(z[:, sl])


def _ssd_head(cfg, h, delay, xbc, xt_all, ac, ac_rows, dt, dt_rows, s_ref, d_row, masks, cb_cache, ys):
    _, _, incl, _ = masks
    hd, ns, width = cfg.head_dim, cfg.ssm_state, cfg.ssm_width
    c = xbc.shape[0]
    for _ in range(delay):
        yield
    g = h // (cfg.ssm_heads // cfg.ssm_groups)
    bm = xbc[:, width + g * ns:width + (g + 1) * ns]
    cm = xbc[:, width + cfg.ssm_bc + g * ns:width + cfg.ssm_bc + (g + 1) * ns]
    if g not in cb_cache:
        cb_cache[g] = _mm_nt(cm, bm)
    acol = ac[:, h:h + 1]
    arow = ac_rows[h:h + 1, :]
    a_last = acol[c - 1:c, :]
    x = xbc[:, h * hd:(h + 1) * hd]
    xdt = x * dt[:, h:h + 1]
    s = s_ref[h]
    y_off = _mm_nt(cm, s)
    xt_dec = xt_all[h * hd:(h + 1) * hd, :] * (dt_rows[h:h + 1, :] * jnp.exp(a_last - arow))
    s_ref[h] = s * jnp.exp(a_last) + _mm(xt_dec, bm)
    yield
    lmat = jnp.exp(jnp.minimum(acol - arow, 0.0))
    y_diag = _mm(jnp.where(incl, cb_cache[g] * lmat, 0.0), xdt)
    yield
    ys[h] = y_diag + y_off * jnp.exp(acol) + d_row[:, h:h + 1] * x


def _rwkv_head(cfg, h, shared, s_ref, prm, masks, incl2, outs):
    _, eye_f, _, strict = masks
    r, k2, v, vt_all, kk_raw, iclr, gate, pprev, pinv, pc, rt, kt = shared
    (_, _, _, _, _, _, _, r_k, ln_w, ln_b) = prm
    hd = cfg.head_dim
    c = r.shape[0]
    sl = slice(h * hd, (h + 1) * hd)
    kkh = kk_raw[:, sl]
    kkh = kkh * lax.rsqrt(jnp.sum(kkh * kkh, axis=-1, keepdims=True) + NORM_EPS)
    at = -kkh * pprev[:, sl]
    bt = kkh * iclr[:, sl] * pinv[:, sl]
    vh = v[:, sl]
    pch = pc[:, sl]
    s = s_ref[h]
    ar = jnp.concatenate([at, rt[:, sl]], axis=0)
    bk = jnp.concatenate([bt, kt[:, sl]], axis=0)
    cross = _mm_nt(ar, bk)
    ar_s = _mm_nt(ar, s)
    vk = _mm(vt_all[sl, :], kt[:, sl] * pch)
    yield
    a_ab = jnp.where(strict, cross[:c, :c], 0.0)
    a_ak = jnp.where(strict, cross[:c, c:], 0.0)
    aakv = _mm(a_ak, vh)
    t = yield from _inv_one_minus_steps(a_ab, eye_f)
    u = _mm(t, ar_s[:c] + aakv)
    yield
    uv = jnp.concatenate([u, vh], axis=0)
    y = ar_s[c:] + _mm(jnp.where(incl2, cross[c:], 0.0), uv)
    ut = _mm_nt(eye_f, u)
    yield
    s_ref[h] = s * pch + _mm(ut, bt * pch) + vk
    yield
    mean = jnp.mean(y, axis=-1, keepdims=True)
    yc = y - mean
    var = jnp.mean(yc * yc, axis=-1, keepdims=True)
    yn = yc * lax.rsqrt(var + RWKV_GN_EPS) * ln_w[:, sl] + ln_b[:, sl]
    bonus = jnp.sum(r[:, sl] * k2[:, sl] * r_k[:, sl], axis=-1, keepdims=True) * vh
    outs[h] = (yn + bonus) * gate[:, sl]


def _rwkv_front(cfg, xm, s_ref, prm, tri, masks, outs):
    (w0, w_up, a0, a_up, g_up, k_k, k_a, _, _, _) = prm
    wd = cfg.rwkv_width
    c = xm.shape[0]
    c0 = 3 * wd
    c1 = c0 + cfg.lora_w
    c2 = c1 + cfg.lora_a
    r = xm[:, :wd]
    k = xm[:, wd:2 * wd]
    v = xm[:, 2 * wd:c0]
    lora_w = _mm(jnp.tanh(xm[:, c0:c1]), w_up)
    lora_a = _mm(xm[:, c1:c2], a_up)
    gate = _mm(jax.nn.sigmoid(xm[:, c2:]), g_up)
    yield
    w_log = -_softplus(-(w0 + lora_w)) - 0.5
    logw = -jnp.exp(w_log)
    cum = _cumsum_rows(tri, logw)
    yield
    iclr = jax.nn.sigmoid(a0 + lora_a)
    k2 = k * (1.0 + (iclr - 1.0) * k_a)
    pm = jnp.exp(cum)
    pinv = jnp.exp(-cum)
    shared = (r, k2, v, v.T, k * k_k, iclr, gate, jnp.exp(cum - logw), pinv, pm[c - 1:c, :], r * pm, k2 * pinv)
    ri2 = lax.broadcasted_iota(jnp.int32, (c, 2 * c), 0)
    ci2 = lax.broadcasted_iota(jnp.int32, (c, 2 * c), 1)
    incl2 = ri2 >= jnp.where(ci2 >= c, ci2 - c, ci2)
    yield [_rwkv_head(cfg, h, shared, s_ref, prm, masks, incl2, outs) for h in range(cfg.rwkv_heads)]


def _mixer_body(seq_ref, first_ref,
                p_ref, gdn0_ref, gdnc0_ref, ssm0_ref, ssmc0_ref, rwkv0_ref, shift0_ref,
                gconv_w_ref, alog_ref, dtb_ref, gnorm_ref,
                sconv_w_ref, sconv_b_ref, sd_ref, snorm_ref,
                mu_ref, w0_ref, wup_ref, a0_ref, aup_ref, gup_ref, kk_ref, ka_ref, rk_ref, lnw_ref, lnb_ref,
                gseg_ref,
                mix_ref, gdn_out_ref, gdnc_out_ref, ssm_out_ref, ssmc_out_ref, rwkv_out_ref, shift_out_ref,
                gdn_s, ssm_s, rwkv_s, gbuf, sbuf, rbuf, *, cfg):
    del seq_ref
    i = pl.program_id(0)
    c = p_ref.shape[0]
    cw = cfg.conv_w

    @pl.when(first_ref[i] == 1)
    def _load_state():
        gdn_s[...] = gdn0_ref[0]
        ssm_s[...] = ssm0_ref[0]
        rwkv_s[...] = rwkv0_ref[0]
        gbuf[HIST - (cw - 1):HIST, :] = gdnc0_ref[0]
        sbuf[HIST - (cw - 1):HIST, :] = ssmc0_ref[0]
        rbuf[HIST - 1:HIST, :] = shift0_ref[0]

    ri = lax.broadcasted_iota(jnp.int32, (c, c), 0)
    ci = lax.broadcasted_iota(jnp.int32, (c, c), 1)
    eye = ri == ci
    incl = ri >= ci
    strict = ri > ci
    eye_f = eye.astype(F32)
    tri = incl.astype(BF16)
    masks = (eye, eye_f, incl, strict)

    gh, sh = cfg.gdn_heads, cfg.ssm_heads
    small = p_ref[:, cfg.o_small:cfg.o_small + LANE]
    sp = _softplus(small + dtb_ref[...])
    cum = _cumsum_rows(tri, sp * (-jnp.exp(alog_ref[...])))
    gc = cum[:, 0:gh]
    ac = cum[:, 2 * gh:2 * gh + sh]
    beta = jax.nn.sigmoid(small[:, gh:2 * gh])
    dt = sp[:, 2 * gh:2 * gh + sh]
    eg = jnp.exp(gc)

    pr = p_ref[:, cfg.o_rwkv:cfg.o_rwkv + cfg.rwkv_cols]
    rbuf[HIST:HIST + c, :] = pr
    prev = rbuf[HIST - 1:HIST - 1 + c, :]
    rbuf[HIST - 1:HIST, :] = pr[c - 1:c, :]
    xm = pr + (prev - pr) * mu_ref[...]
    prm = (w0_ref[...], wup_ref[...], a0_ref[...], aup_ref[...], gup_ref[...], kk_ref[...], ka_ref[...],
           rk_ref[...], lnw_ref[...], lnb_ref[...])

    qkv = _silu(_conv_chunk(gbuf, p_ref[:, cfg.o_gdn_qkv:cfg.o_gdn_qkv + 3 * cfg.gdn_width], gconv_w_ref, cw))
    xbc = _conv_chunk(sbuf, p_ref[:, cfg.o_ssm_xbc:cfg.o_ssm_xbc + cfg.ssm_width + 2 * cfg.ssm_bc], sconv_w_ref, cw)
    xbc = _silu(xbc + sconv_b_ref[...])
    gz = p_ref[:, cfg.o_gdn_z:cfg.o_gdn_z + cfg.gdn_width]
    sz = p_ref[:, cfg.o_ssm_z:cfg.o_ssm_z + cfg.ssm_width]
    gnorm = gnorm_ref[...]
    sd = sd_ref[...]

    cum_t = cum.T
    gc_rows = cum_t[0:gh, :]
    ac_rows = cum_t[2 * gh:2 * gh + sh, :]
    dt_rows = sp.T[2 * gh:2 * gh + sh, :]
    gwid = cfg.gdn_width
    q_raw, k_raw, v_all = qkv[:, :gwid], qkv[:, gwid:2 * gwid], qkv[:, 2 * gwid:]
    ssq = _seg_sum(jnp.concatenate([q_raw * q_raw, k_raw * k_raw], axis=0), gseg_ref[...])
    q_all = q_raw * lax.rsqrt(ssq[:c] + NORM_EPS) * (cfg.head_dim ** -0.5)
    k_all = k_raw * lax.rsqrt(ssq[c:] + NORM_EPS)
    kt_all = k_all.T
    xt_all = xbc[:, :cfg.ssm_width].T

    gdn_o = [None] * gh
    ssd_y = [None] * sh
    rwkv_o = [None] * cfg.rwkv_heads
    cb_cache = {}
    tasks = [_rwkv_front(cfg, xm, rwkv_s, prm, tri, masks, rwkv_o)]
    tasks += [_gdn_head(cfg, h, q_all, k_all, kt_all, v_all, gz, gc, gc_rows, beta, eg, gdn_s, gnorm, masks, gdn_o)
              for h in range(gh)]
    tasks += [_ssd_head(cfg, h, h, xbc, xt_all, ac, ac_rows, dt, dt_rows, ssm_s, sd, masks, cb_cache, ssd_y)
              for h in range(sh)]
    _run_interleaved(tasks)

    ng = cfg.ssm_groups
    gw = cfg.ssm_width // ng
    snorm = snorm_ref[...]
    ssd_o = []
    for g in range(ng):
        yg = jnp.concatenate(ssd_y[g * (sh // ng):(g + 1) * (sh // ng)], axis=-1)
        yg = yg * _silu(sz[:, g * gw:(g + 1) * gw])
        yg = yg * lax.rsqrt(jnp.mean(yg * yg, axis=-1, keepdims=True) + NORM_EPS)
        ssd_o.append(yg * snorm[:, g * gw:(g + 1) * gw])

    mix_ref[...] = jnp.concatenate(gdn_o + ssd_o + rwkv_o, axis=-1).astype(mix_ref.dtype)
    gdn_out_ref[0] = gdn_s[...]
    ssm_out_ref[0] = ssm_s[...]
    rwkv_out_ref[0] = rwkv_s[...]
    gdnc_out_ref[0] = gbuf[HIST - (cw - 1):HIST, :]
    ssmc_out_ref[0] = sbuf[HIST - (cw - 1):HIST, :]
    shift_out_ref[0] = rbuf[HIST - 1:HIST, :]


def _mixer_call(cfg, seq_id, first, proj, init, prm):
    n = proj.shape[0]
    nchunks = n // CHUNK
    gdn0, gdnc0, ssm0, ssmc0, rwkv0, shift0 = init

    def chunk_map(i, seq, fst):
        return (i, 0)

    def seq_map4(i, seq, fst):
        return (seq[i], 0, 0, 0)

    def seq_map3(i, seq, fst):
        return (seq[i], 0, 0)

    def const2(i, seq, fst):
        return (0, 0)

    def seq_spec(st):
        return pl.BlockSpec((1,) + st.shape[1:], seq_map4 if st.ndim == 4 else seq_map3)

    in_specs = ([pl.BlockSpec((CHUNK, proj.shape[1]), chunk_map)] + [seq_spec(st) for st in init]
                + [pl.BlockSpec(p.shape, const2) for p in prm])
    out_specs = [pl.BlockSpec((CHUNK, cfg.mix_width), chunk_map)] + [seq_spec(st) for st in init]
    out_shape = [jax.ShapeDtypeStruct((n, cfg.mix_width), BF16)] + [
        jax.ShapeDtypeStruct(st.shape, F32) for st in init]
    scratch = [
        pltpu.VMEM(gdn0.shape[1:], F32),
        pltpu.VMEM(ssm0.shape[1:], F32),
        pltpu.VMEM(rwkv0.shape[1:], F32),
        pltpu.VMEM((HIST + CHUNK, gdnc0.shape[2]), F32),
        pltpu.VMEM((HIST + CHUNK, ssmc0.shape[2]), F32),
        pltpu.VMEM((HIST + CHUNK, shift0.shape[2]), F32),
    ]
    return pl.pallas_call(
        functools.partial(_mixer_body, cfg=cfg),
        grid_spec=pltpu.PrefetchScalarGridSpec(
            num_scalar_prefetch=2, grid=(nchunks,), in_specs=in_specs, out_specs=out_specs,
            scratch_shapes=scratch),
        out_shape=out_shape,
        compiler_params=pltpu.CompilerParams(
            dimension_semantics=("arbitrary",), vmem_limit_bytes=VMEM_LIMIT_BYTES),
        name="mixers",
    )(seq_id, first, proj, *init, *prm)


def _pad_lanes(v, width=LANE):
    v = v.reshape(1, -1)
    return jnp.pad(v, ((0, 0), (0, width - v.shape[1])))


def kernel(x_prompt, x_sample, state_gdn, state_gdn_conv, state_ssm, state_ssm_conv, state_rwkv, state_rwkv_shift, norm1_w, w_in, gdn_conv_w, gdn_A_log, gdn_dt_bias, gdn_norm_w, ssm_conv_w, ssm_conv_b, ssm_A_log, ssm_dt_bias, ssm_D, ssm_norm_w, rwkv_mu, rwkv_w0, rwkv_w_up, rwkv_a0, rwkv_a_up, rwkv_g_up, rwkv_k_k, rwkv_k_a, rwkv_r_k, rwkv_ln_w, rwkv_ln_b, w_out, norm2_w, ffn_w_gate, ffn_w_up, ffn_w_down, final_norm_w):
    depth = w_in.shape[0]
    nbp, tp, d = x_prompt.shape
    nbs, ts, _ = x_sample.shape
    hd = state_gdn.shape[-1]
    ssm_width = state_ssm.shape[2] * hd
    cfg = Cfg(
        d_model=d, head_dim=hd, conv_w=gdn_conv_w.shape[1],
        gdn_heads=state_gdn.shape[2], ssm_heads=state_ssm.shape[2],
        ssm_groups=(ssm_conv_w.shape[2] - ssm_width) // (2 * state_ssm.shape[-1]),
        ssm_state=state_ssm.shape[-1], rwkv_heads=state_rwkv.shape[2],
        lora_w=rwkv_w_up.shape[1], lora_a=rwkv_a_up.shape[1], lora_g=rwkv_g_up.shape[1])
    assert tp % CHUNK == 0 and ts % CHUNK == 0 and tp >= cfg.conv_w and ts >= cfg.conv_w
    assert 2 * cfg.gdn_heads + cfg.ssm_heads <= LANE
    gw, sw = cfg.gdn_width, cfg.ssm_width
    gdn_cols = 4 * gw + 2 * cfg.gdn_heads
    ssm_cols = 2 * sw + 2 * cfg.ssm_bc + cfg.ssm_heads
    assert w_in.shape[2] == gdn_cols + ssm_cols + cfg.rwkv_cols

    x = jnp.concatenate([x_prompt.reshape(nbp * tp, d), x_sample.reshape(nbs * ts, d)], axis=0)
    n = x.shape[0]
    assert n % FFN_ROWS == 0 and n % PROJ_ROWS == 0
    seq_len = [tp] * nbp + [ts] * nbs
    seq_id, first = [], []
    for s, length in enumerate(seq_len):
        for j in range(length // CHUNK):
            seq_id.append(s)
            first.append(1 if j == 0 else 0)
    seq_id = jnp.asarray(seq_id, jnp.int32)
    first = jnp.asarray(first, jnp.int32)

    def with_prompt_zeros(st):
        return jnp.concatenate([jnp.zeros((nbp,) + st.shape[1:], F32), st.astype(F32)], axis=0)

    ga = 4 * gw
    sa = gdn_cols + 2 * sw + 2 * cfg.ssm_bc
    small_w = jnp.concatenate([w_in[:, :, ga:ga + 2 * cfg.gdn_heads], w_in[:, :, sa:sa + cfg.ssm_heads]], axis=2)
    small_w = jnp.pad(small_w, ((0, 0), (0, 0), (0, LANE - small_w.shape[2])))
    w_proj = jnp.concatenate(
        [w_in[:, :, :ga], w_in[:, :, gdn_cols:gdn_cols + 2 * sw + 2 * cfg.ssm_bc],
         w_in[:, :, gdn_cols + ssm_cols:], small_w], axis=2).astype(BF16)
    assert w_proj.shape[2] == cfg.proj_cols

    def small_row(l, gdn_first, gdn_second, ssm_part):
        return _pad_lanes(jnp.concatenate([gdn_first[l], gdn_second, ssm_part[l]]))

    zeros_g = jnp.zeros((cfg.gdn_heads,), F32)
    head_of = jnp.arange(gw) // hd
    gdn_seg = (head_of[:, None] == head_of[None, :]).astype(BF16)
    wo_b, wg_b, wu_b, wd_b = (w.astype(BF16) for w in (w_out, ffn_w_gate, ffn_w_up, ffn_w_down))

    new_states = [[] for _ in range(6)]
    for l in range(depth):
        proj = _proj_call(x, norm1_w[l], w_proj[l])
        init = tuple(with_prompt_zeros(st[l]) for st in
                     (state_gdn, state_gdn_conv, state_ssm, state_ssm_conv, state_rwkv, state_rwkv_shift))
        prm = (
            gdn_conv_w[l], small_row(l, gdn_A_log, zeros_g, ssm_A_log), small_row(l, gdn_dt_bias, zeros_g, ssm_dt_bias),
            gdn_norm_w[l].reshape(1, -1),
            ssm_conv_w[l], ssm_conv_b[l].reshape(1, -1), ssm_D[l].reshape(1, -1), ssm_norm_w[l].reshape(1, -1),
            rwkv_mu[l].reshape(1, -1), rwkv_w0[l].reshape(1, -1), rwkv_w_up[l], rwkv_a0[l].reshape(1, -1),
            rwkv_a_up[l], rwkv_g_up[l], rwkv_k_k[l].reshape(1, -1), rwkv_k_a[l].reshape(1, -1),
            rwkv_r_k[l].reshape(1, -1), rwkv_ln_w[l].reshape(1, -1), rwkv_ln_b[l].reshape(1, -1),
            gdn_seg,
        )
        mix, *states = _mixer_call(cfg, seq_id, first, proj, init, prm)
        x = _ffn_call(x, mix, wo_b[l], norm2_w[l], wg_b[l], wu_b[l], wd_b[l], final_norm_w, final=(l == depth - 1))
        for acc, st in zip(new_states, states):
            acc.append(st)

    y_prompt = x[:nbp * tp].reshape(nbp, tp, d)
    y_sample = x[nbp * tp:].reshape(nbs, ts, d)
    stacked = [jnp.stack(st) for st in new_states]
    p_out = [st[:, :nbp] for st in stacked]
    s_out = [st[:, nbp:] for st in stacked]
    return (y_prompt, y_sample, *p_out, *s_out)
```

```python
import functools
import math
from typing import NamedTuple

import jax
import jax.numpy as jnp
from jax import lax
from jax.experimental import pallas as pl
from jax.experimental.pallas import tpu as pltpu

F32 = jnp.float32
BF16 = jnp.bfloat16

CHUNK = 64
SLOTS = 2
SLOT_DELAY = 3
NORM_EPS = 1e-6
RWKV_GN_EPS = 64e-5
LANE = 128
HIST = 8
VMEM_LIMIT_BYTES = 56 * 1024 * 1024
PROJ_ROWS = 256
FFN_ROWS = 512
FFN_COLS = 256


class Cfg(NamedTuple):
    d_model: int
    head_dim: int
    conv_w: int
    gdn_heads: int
    ssm_heads: int
    ssm_groups: int
    ssm_state: int
    rwkv_heads: int
    lora_w: int
    lora_a: int
    lora_g: int

    @property
    def gdn_width(self):
        return self.gdn_heads * self.head_dim

    @property
    def ssm_width(self):
        return self.ssm_heads * self.head_dim

    @property
    def ssm_bc(self):
        return self.ssm_groups * self.ssm_state

    @property
    def rwkv_width(self):
        return self.rwkv_heads * self.head_dim

    @property
    def rwkv_cols(self):
        return 3 * self.rwkv_width + self.lora_w + self.lora_a + self.lora_g

    @property
    def o_gdn_qkv(self):
        return 0

    @property
    def o_gdn_z(self):
        return 3 * self.gdn_width

    @property
    def o_ssm_z(self):
        return self.o_gdn_z + self.gdn_width

    @property
    def o_ssm_xbc(self):
        return self.o_ssm_z + self.ssm_width

    @property
    def o_rwkv(self):
        return self.o_ssm_xbc + self.ssm_width + 2 * self.ssm_bc

    @property
    def o_small(self):
        return self.o_rwkv + self.rwkv_cols

    @property
    def proj_cols(self):
        return self.o_small + LANE

    @property
    def mix_width(self):
        return self.gdn_width + self.ssm_width + self.rwkv_width


def _rms(x, w):
    return x * lax.rsqrt(jnp.mean(x * x, axis=-1, keepdims=True) + NORM_EPS) * w


def _softplus(x):
    return jnp.maximum(x, 0.0) + jnp.log1p(jnp.exp(-jnp.abs(x)))


def _silu(x):
    return x * jax.nn.sigmoid(x)


def _mm(a, b):
    return jnp.dot(a.astype(BF16), b.astype(BF16), preferred_element_type=F32)


def _mm_nt(a, b):
    return lax.dot_general(a.astype(BF16), b.astype(BF16), (((1,), (1,)), ((), ())),
                           preferred_element_type=F32)


def _split_mm(x, m):
    hi = x.astype(BF16)
    lo = (x - hi.astype(F32)).astype(BF16)
    return jnp.dot(hi, m, preferred_element_type=F32) + jnp.dot(lo, m, preferred_element_type=F32)


def _cumsum_rows(tri, x):
    hi = x.astype(BF16)
    lo = (x - hi.astype(F32)).astype(BF16)
    return jnp.dot(tri, hi, preferred_element_type=F32) + jnp.dot(tri, lo, preferred_element_type=F32)


def _inv_one_minus_steps(n, eye_f):
    c = n.shape[0]
    t = eye_f + n
    p = _mm(n, n)
    yield
    for _ in range(int(math.log2(c)) - 2):
        step = _mm(t, p)
        p_next = _mm(p, p)
        yield
        t = t + step
        p = p_next
    step = _mm(t, p)
    yield
    return t + step


def _run_interleaved(tasks):
    tasks = list(tasks)
    while tasks:
        alive = []
        for task in tasks:
            try:
                spawned = next(task)
            except StopIteration:
                continue
            alive.append(task)
            if spawned:
                alive.extend(spawned)
        tasks = alive


class _State:
    def __init__(self, ref, index):
        self.ref, self.index, self.version = ref, index, 0

    def read(self, slot):
        assert self.version == slot, "chunk slot reads a state the previous slot has not written yet"
        return self.ref[self.index]

    def write(self, slot, value):
        assert self.version == slot
        self.ref[self.index] = value
        self.version += 1


def _conv_chunk(buf_ref, x, w_ref, conv_w):
    c = x.shape[0]
    buf_ref[HIST:HIST + c, :] = x
    y = x * w_ref[conv_w - 1:conv_w, :]
    for j in range(conv_w - 1):
        lo = HIST - (conv_w - 1) + j
        y = y + buf_ref[lo:lo + c, :] * w_ref[j:j + 1, :]
    tail = buf_ref[HIST + c - (conv_w - 1):HIST + c, :]
    buf_ref[HIST - (conv_w - 1):HIST, :] = tail
    return y


def _per_set(fn, x, nset):
    rows = x.shape[0] // nset
    return jnp.concatenate([fn(s, x[s * rows:(s + 1) * rows]) for s in range(nset)], axis=0)


def _proj_body(x_ref, nw_ref, w_ref, o_ref):
    h = _rms(x_ref[...], nw_ref[...])
    o_ref[...] = jnp.dot(h.astype(BF16), w_ref[...], preferred_element_type=F32)


def _proj_call(x, norm_w, w_bf16):
    n, d = x.shape
    cols = w_bf16.shape[1]
    rows = PROJ_ROWS
    assert n % rows == 0
    return pl.pallas_call(
        _proj_body,
        grid=(n // rows,),
        in_specs=[
            pl.BlockSpec((rows, d), lambda i: (i, 0)),
            pl.BlockSpec((1, d), lambda i: (0, 0)),
            pl.BlockSpec((d, cols), lambda i: (0, 0)),
        ],
        out_specs=pl.BlockSpec((rows, cols), lambda i: (i, 0)),
        out_shape=jax.ShapeDtypeStruct((n, cols), F32),
        compiler_params=pltpu.CompilerParams(
            dimension_semantics=("arbitrary",), vmem_limit_bytes=VMEM_LIMIT_BYTES),
        name="norm_proj",
    )(x, norm_w.reshape(1, d), w_bf16)


def _ffn_body(x_ref, mix_ref, wo_ref, n2_ref, wg_ref, wu_ref, wd_ref, fn_ref, o_ref, *, final):
    x = x_ref[...] + jnp.dot(mix_ref[...], wo_ref[...], preferred_element_type=F32)
    h2 = _rms(x, n2_ref[...]).astype(BF16)
    hidden = wg_ref.shape[1]
    acc = x
    for c0 in range(0, hidden, FFN_COLS):
        g = jnp.dot(h2, wg_ref[:, c0:c0 + FFN_COLS], preferred_element_type=F32)
        u = jnp.dot(h2, wu_ref[:, c0:c0 + FFN_COLS], preferred_element_type=F32)
        ff = (_silu(g) * u).astype(BF16)
        acc = acc + jnp.dot(ff, wd_ref[c0:c0 + FFN_COLS, :], preferred_element_type=F32)
    if final:
        acc = _rms(acc, fn_ref[...])
    o_ref[...] = acc


def _ffn_call(x, mix, wo, n2, wg, wu, wd, fn, final):
    n, d = x.shape
    hidden = wg.shape[1]
    rows = FFN_ROWS
    assert n % rows == 0 and hidden % FFN_COLS == 0
    const = lambda i: (0, 0)
    return pl.pallas_call(
        functools.partial(_ffn_body, final=final),
        grid=(n // rows,),
        in_specs=[
            pl.BlockSpec((rows, d), lambda i: (i, 0)),
            pl.BlockSpec((rows, mix.shape[1]), lambda i: (i, 0)),
            pl.BlockSpec(wo.shape, const),
            pl.BlockSpec((1, d), const),
            pl.BlockSpec(wg.shape, const),
            pl.BlockSpec(wu.shape, const),
            pl.BlockSpec(wd.shape, const),
            pl.BlockSpec((1, d), const),
        ],
        out_specs=pl.BlockSpec((rows, d), lambda i: (i, 0)),
        out_shape=jax.ShapeDtypeStruct((n, d), F32),
        compiler_params=pltpu.CompilerParams(
            dimension_semantics=("arbitrary",), vmem_limit_bytes=VMEM_LIMIT_BYTES),
        name="outproj_ffn",
    )(x, mix, wo, n2.reshape(1, d), wg, wu, wd, fn.reshape(1, d))


def _gdn_head(cfg, h, slot, delay, arrs, state, norm_w, masks, outs):
    q_all, k_all, kt_all, v_all, z, gc, gc_rows, beta, eg_all = arrs
    _, eye_f, incl, strict = masks
    hd = cfg.head_dim
    c = q_all.shape[0]
    for _ in range(delay):
        yield
    sl = slice(h * hd, (h + 1) * hd)
    q = q_all[:, sl]
    k = k_all[:, sl]
    kt = kt_all[sl, :]
    b = beta[:, h:h + 1]
    kb = k * b
    vb = v_all[:, sl] * b
    kq = _mm(jnp.concatenate([kb, q], axis=0), kt)
    yield
    gcol = gc[:, h:h + 1]
    grow = gc_rows[h:h + 1, :]
    dec = jnp.exp(jnp.minimum(gcol - grow, 0.0))
    lower = jnp.where(strict, kq[:c] * dec, 0.0)
    attn = jnp.where(incl, kq[c:] * dec, 0.0)
    t = yield from _inv_one_minus_steps(-lower, eye_f)
    eg = eg_all[:, h:h + 1]
    u = _mm(t, vb)
    w = _mm(t, kb * eg)
    yield
    s = state.read(slot)
    wq_s = _mm(jnp.concatenate([w, q * eg], axis=0), s)
    yield
    g_last = gcol[c - 1:c, :]
    v_new = u - wq_s[:c]
    o = wq_s[c:] + _mm(attn, v_new)
    state.write(slot, s * jnp.exp(g_last) + _mm(kt * jnp.exp(g_last - grow), v_new))
    yield
    outs[h] = _rms(o, norm_w) * _silu(z[:, sl])


def _ssd_head(cfg, h, slot, delay, arrs, state, d_row, masks, cb_cache, ys):
    xbc, xt_all, ac, ac_rows, dt, dt_rows = arrs
    _, _, incl, _ = masks
    hd, ns, width = cfg.head_dim, cfg.ssm_state, cfg.ssm_width
    c = xbc.shape[0]
    for _ in range(delay):
        yield
    g = h // (cfg.ssm_heads // cfg.ssm_groups)
    bm = xbc[:, width + g * ns:width + (g + 1) * ns]
    cm = xbc[:, width + cfg.ssm_bc + g * ns:width + cfg.ssm_bc + (g + 1) * ns]
    if g not in cb_cache:
        cb_cache[g] = _mm_nt(cm, bm)
    acol = ac[:, h:h + 1]
    arow = ac_rows[h:h + 1, :]
    a_last = acol[c - 1:c, :]
    x = xbc[:, h * hd:(h + 1) * hd]
    xdt = x * dt[:, h:h + 1]
    s = state.read(slot)
    y_off = _mm_nt(cm, s)
    xt_dec = xt_all[h * hd:(h + 1) * hd, :] * (dt_rows[h:h + 1, :] * jnp.exp(a_last - arow))
    state.write(slot, s * jnp.exp(a_last) + _mm(xt_dec, bm))
    yield
    lmat = jnp.exp(jnp.minimum(acol - arow, 0.0))
    y_diag = _mm(jnp.where(incl, cb_cache[g] * lmat, 0.0), xdt)
    yield
    ys[h] = y_diag + y_off * jnp.exp(acol) + d_row[:, h:h + 1] * x


def _rwkv_head(cfg, h, slot, delay, shared, state, prm, masks, incl2, outs):
    _, eye_f, _, strict = masks
    r, k2, v, vt_all, kk_raw, iclr, gate, pprev, pinv, pc, rt, kt = shared
    (_, _, _, _, _, _, _, r_k, ln_w, ln_b) = prm
    hd = cfg.head_dim
    c = r.shape[0]
    for _ in range(delay):
        yield
    sl = slice(h * hd, (h + 1) * hd)
    kkh = kk_raw[:, sl]
    kkh = kkh * lax.rsqrt(jnp.sum(kkh * kkh, axis=-1, keepdims=True) + NORM_EPS)
    at = -kkh * pprev[:, sl]
    bt = kkh * iclr[:, sl] * pinv[:, sl]
    vh = v[:, sl]
    pch = pc[:, sl]
    ar = jnp.concatenate([at, rt[:, sl]], axis=0)
    bk = jnp.concatenate([bt, kt[:, sl]], axis=0)
    cross = _mm_nt(ar, bk)
    vk = _mm(vt_all[sl, :], kt[:, sl] * pch)
    yield
    a_ab = jnp.where(strict, cross[:c, :c], 0.0)
    a_ak = jnp.where(strict, cross[:c, c:], 0.0)
    aakv = _mm(a_ak, vh)
    t = yield from _inv_one_minus_steps(a_ab, eye_f)
    s = state.read(slot)
    ar_s = _mm_nt(ar, s)
    yield
    u = _mm(t, ar_s[:c] + aakv)
    yield
    uv = jnp.concatenate([u, vh], axis=0)
    y = ar_s[c:] + _mm(jnp.where(incl2, cross[c:], 0.0), uv)
    ut = _mm_nt(eye_f, u)
    yield
    state.write(slot, s * pch + _mm(ut, bt * pch) + vk)
    yield
    mean = jnp.mean(y, axis=-1, keepdims=True)
    yc = y - mean
    var = jnp.mean(yc * yc, axis=-1, keepdims=True)
    yn = yc * lax.rsqrt(var + RWKV_GN_EPS) * ln_w[:, sl] + ln_b[:, sl]
    bonus = jnp.sum(r[:, sl] * k2[:, sl] * r_k[:, sl], axis=-1, keepdims=True) * vh
    outs[h] = (yn + bonus) * gate[:, sl]


def _rwkv_front(cfg, xm, states, slot_plan, prm, tri, masks, outs):
    (w0, w_up, a0, a_up, g_up, k_k, k_a, _, _, _) = prm
    wd = cfg.rwkv_width
    c = CHUNK
    c0 = 3 * wd
    c1 = c0 + cfg.lora_w
    c2 = c1 + cfg.lora_a
    r = xm[:, :wd]
    k = xm[:, wd:2 * wd]
    v = xm[:, 2 * wd:c0]
    lora_w = _mm(jnp.tanh(xm[:, c0:c1]), w_up)
    lora_a = _mm(xm[:, c1:c2], a_up)
    gate = _mm(jax.nn.sigmoid(xm[:, c2:]), g_up)
    yield
    w_log = -_softplus(-(w0 + lora_w)) - 0.5
    logw = -jnp.exp(w_log)
    cum = _cumsum_rows(tri, logw)
    yield
    iclr = jax.nn.sigmoid(a0 + lora_a)
    k2 = k * (1.0 + (iclr - 1.0) * k_a)
    pm = jnp.exp(cum)
    pinv = jnp.exp(-cum)
    full = (r, k2, v, k * k_k, iclr, gate, jnp.exp(cum - logw), pinv, r * pm, k2 * pinv)
    ri2 = lax.broadcasted_iota(jnp.int32, (c, 2 * c), 0)
    ci2 = lax.broadcasted_iota(jnp.int32, (c, 2 * c), 1)
    incl2 = ri2 >= jnp.where(ci2 >= c, ci2 - c, ci2)
    tasks = []
    for g, (s, j) in enumerate(slot_plan):
        rs = slice(g * c, (g + 1) * c)
        r_, k2_, v_, kk_, iclr_, gate_, pprev_, pinv_, rt_, kt_ = (a[rs] for a in full)
        shared = (r_, k2_, v_, v_.T, kk_, iclr_, gate_, pprev_, pinv_, pm[(g + 1) * c - 1:(g + 1) * c, :], rt_, kt_)
        tasks += [_rwkv_head(cfg, h, j, j * SLOT_DELAY, shared, states[s][h], prm, masks, incl2, outs[g])
                  for h in range(cfg.rwkv_heads)]
    yield tasks


def _mixer_body(p_ref, gdn0_ref, gdnc0_ref, ssm0_ref, ssmc0_ref, rwkv0_ref, shift0_ref,
                gconv_w_ref, alog_ref, dtb_ref, gnorm_ref,
                sconv_w_ref, sconv_b_ref, sd_ref, snorm_ref,
                mu_ref, w0_ref, wup_ref, a0_ref, aup_ref, gup_ref, kk_ref, ka_ref, rk_ref, lnw_ref, lnb_ref,
                gseg_ref, *rest, cfg, nset, aliased):
    if aliased:
        rest = rest[1:]
    (mix_ref, gdn_out_ref, gdnc_out_ref, ssm_out_ref, ssmc_out_ref, rwkv_out_ref, shift_out_ref,
     gdn_s, ssm_s, rwkv_s, gbuf, sbuf, rbuf) = rest
    c = CHUNK
    rows = p_ref.shape[0]
    nslot = rows // c
    per_set = nslot // nset
    slot_plan = [(s, j) for s in range(nset) for j in range(per_set)]
    cw = cfg.conv_w

    @pl.when(pl.program_id(1) == 0)
    def _load_state():
        gdn_s[...] = gdn0_ref[...]
        ssm_s[...] = ssm0_ref[...]
        rwkv_s[...] = rwkv0_ref[...]
        gbuf[:, HIST - (cw - 1):HIST, :] = gdnc0_ref[...]
        sbuf[:, HIST - (cw - 1):HIST, :] = ssmc0_ref[...]
        rbuf[:, HIST - 1:HIST, :] = shift0_ref[...]

    ri = lax.broadcasted_iota(jnp.int32, (c, c), 0)
    ci = lax.broadcasted_iota(jnp.int32, (c, c), 1)
    eye = ri == ci
    incl = ri >= ci
    strict = ri > ci
    eye_f = eye.astype(F32)
    masks = (eye, eye_f, incl, strict)
    rr = lax.broadcasted_iota(jnp.int32, (rows, rows), 0)
    cc = lax.broadcasted_iota(jnp.int32, (rows, rows), 1)
    same_chunk = functools.reduce(jnp.logical_and, [(rr >= m * c) == (cc >= m * c) for m in range(1, nslot)],
                                  rr >= 0)
    tri = jnp.logical_and(same_chunk, rr >= cc).astype(BF16)

    gh, sh = cfg.gdn_heads, cfg.ssm_heads
    small = p_ref[:, cfg.o_small:cfg.o_small + LANE]
    sp = _softplus(small + dtb_ref[...])
    cum = _cumsum_rows(tri, sp * (-jnp.exp(alog_ref[...])))
    beta = jax.nn.sigmoid(small[:, gh:2 * gh])
    eg = jnp.exp(cum[:, 0:gh])

    def shift(s, pr):
        n = pr.shape[0]
        rbuf[s, HIST:HIST + n, :] = pr
        prev = rbuf[s, HIST - 1:HIST - 1 + n, :]
        rbuf[s, HIST - 1:HIST, :] = pr[n - 1:n, :]
        return pr + (prev - pr) * mu_ref[...]

    xm = _per_set(shift, p_ref[:, cfg.o_rwkv:cfg.o_rwkv + cfg.rwkv_cols], nset)
    prm = (w0_ref[...], wup_ref[...], a0_ref[...], aup_ref[...], gup_ref[...], kk_ref[...], ka_ref[...],
           rk_ref[...], lnw_ref[...], lnb_ref[...])

    qkv = _per_set(lambda s, x: _conv_chunk(gbuf.at[s], x, gconv_w_ref, cw),
                   p_ref[:, cfg.o_gdn_qkv:cfg.o_gdn_qkv + 3 * cfg.gdn_width], nset)
    qkv = _silu(qkv)
    xbc = _per_set(lambda s, x: _conv_chunk(sbuf.at[s], x, sconv_w_ref, cw),
                   p_ref[:, cfg.o_ssm_xbc:cfg.o_ssm_xbc + cfg.ssm_width + 2 * cfg.ssm_bc], nset)
    xbc = _silu(xbc + sconv_b_ref[...])
    gz = p_ref[:, cfg.o_gdn_z:cfg.o_gdn_z + cfg.gdn_width]
    sz = p_ref[:, cfg.o_ssm_z:cfg.o_ssm_z + cfg.ssm_width]
    gnorm = gnorm_ref[...]
    sd = sd_ref[...]

    gwid = cfg.gdn_width
    q_raw, k_raw, v_all = qkv[:, :gwid], qkv[:, gwid:2 * gwid], qkv[:, 2 * gwid:]
    ssq = _split_mm(jnp.concatenate([q_raw * q_raw, k_raw * k_raw], axis=0), gseg_ref[...])
    q_all = q_raw * lax.rsqrt(ssq[:rows] + NORM_EPS) * (cfg.head_dim ** -0.5)
    k_all = k_raw * lax.rsqrt(ssq[rows:] + NORM_EPS)

    gdn_states = [[_State(gdn_s, (s, h)) for h in range(gh)] for s in range(nset)]
    ssm_states = [[_State(ssm_s, (s, h)) for h in range(sh)] for s in range(nset)]
    rwkv_states = [[_State(rwkv_s, (s, h)) for h in range(cfg.rwkv_heads)] for s in range(nset)]
    gdn_o = [[None] * gh for _ in slot_plan]
    ssd_y = [[None] * sh for _ in slot_plan]
    rwkv_o = [[None] * cfg.rwkv_heads for _ in slot_plan]

    tasks = [_rwkv_front(cfg, xm, rwkv_states, slot_plan, prm, tri, masks, rwkv_o)]
    for g, (s, j) in enumerate(slot_plan):
        rs = slice(g * c, (g + 1) * c)
        cum_g, sp_g = cum[rs], sp[rs]
        cum_t, sp_t = cum_g.T, sp_g.T
        garrs = (q_all[rs], k_all[rs], k_all[rs].T, v_all[rs], gz[rs], cum_g[:, 0:gh], cum_t[0:gh, :],
                 beta[rs], eg[rs])
        tasks += [_gdn_head(cfg, h, j, j * SLOT_DELAY, garrs, gdn_states[s][h], gnorm, masks, gdn_o[g])
                  for h in range(gh)]
        sarrs = (xbc[rs], xbc[rs, :cfg.ssm_width].T, cum_g[:, 2 * gh:2 * gh + sh], cum_t[2 * gh:2 * gh + sh, :],
                 sp_g[:, 2 * gh:2 * gh + sh], sp_t[2 * gh:2 * gh + sh, :])
        cb_cache = {}
        tasks += [_ssd_head(cfg, h, j, j * SLOT_DELAY + h, sarrs, ssm_states[s][h], sd, masks, cb_cache, ssd_y[g])
                  for h in range(sh)]
    _run_interleaved(tasks)

    ng = cfg.ssm_groups
    gw = cfg.ssm_width // ng
    snorm = snorm_ref[...]
    for g in range(nslot):
        rs = slice(g * c, (g + 1) * c)
        ssd_o = []
        for grp in range(ng):
            yg = jnp.concatenate(ssd_y[g][grp * (sh // ng):(grp + 1) * (sh // ng)], axis=-1)
            yg = yg * _silu(sz[rs, grp * gw:(grp + 1) * gw])
            yg = yg * lax.rsqrt(jnp.mean(yg * yg, axis=-1, keepdims=True) + NORM_EPS)
            ssd_o.append(yg * snorm[:, grp * gw:(grp + 1) * gw])
        mix_ref[rs, :] = jnp.concatenate(gdn_o[g] + ssd_o + rwkv_o[g], axis=-1).astype(mix_ref.dtype)

    gdn_out_ref[...] = gdn_s[...]
    ssm_out_ref[...] = ssm_s[...]
    rwkv_out_ref[...] = rwkv_s[...]
    gdnc_out_ref[...] = gbuf[:, HIST - (cw - 1):HIST, :]
    ssmc_out_ref[...] = sbuf[:, HIST - (cw - 1):HIST, :]
    shift_out_ref[...] = rbuf[:, HIST - 1:HIST, :]


def _mixer_call(cfg, proj, row0, init, chunks_per_seq, prm, mix_prev):
    nseq = init[0].shape[0]
    if chunks_per_seq % SLOTS == 0:
        nset, per_set = 1, SLOTS
    elif chunks_per_seq == 1 and nseq % SLOTS == 0:
        nset, per_set = SLOTS, 1
    else:
        nset, per_set = 1, 1
    rows = nset * per_set * CHUNK
    steps = chunks_per_seq // per_set
    assert row0 % rows == 0 and nseq % nset == 0
    blk0 = row0 // rows

    def row_map(a, i):
        return (blk0 + a * steps + i, 0)

    def set_map4(a, i):
        return (a, 0, 0, 0)

    def set_map3(a, i):
        return (a, 0, 0)

    def const2(a, i):
        return (0, 0)

    def set_spec(st):
        return pl.BlockSpec((nset,) + st.shape[1:], set_map4 if st.ndim == 4 else set_map3)

    operands = [proj, *init, *prm]
    in_specs = ([pl.BlockSpec((rows, proj.shape[1]), row_map)] + [set_spec(st) for st in init]
                + [pl.BlockSpec(p.shape, const2) for p in prm])
    aliases = {}
    if mix_prev is not None:
        aliases = {len(operands): 0}
        operands.append(mix_prev)
        in_specs.append(pl.BlockSpec(memory_space=pl.ANY))
    out_specs = [pl.BlockSpec((rows, cfg.mix_width), row_map)] + [set_spec(st) for st in init]
    out_shape = [jax.ShapeDtypeStruct((proj.shape[0], cfg.mix_width), BF16)] + [
        jax.ShapeDtypeStruct(st.shape, F32) for st in init]
    gdn0, gdnc0, ssm0, ssmc0, rwkv0, shift0 = init
    scratch = [
        pltpu.VMEM((nset,) + gdn0.shape[1:], F32),
        pltpu.VMEM((nset,) + ssm0.shape[1:], F32),
        pltpu.VMEM((nset,) + rwkv0.shape[1:], F32),
        pltpu.VMEM((nset, HIST + per_set * CHUNK, gdnc0.shape[2]), F32),
        pltpu.VMEM((nset, HIST + per_set * CHUNK, ssmc0.shape[2]), F32),
        pltpu.VMEM((nset, HIST + per_set * CHUNK, shift0.shape[2]), F32),
    ]
    return pl.pallas_call(
        functools.partial(_mixer_body, cfg=cfg, nset=nset, aliased=mix_prev is not None),
        grid=(nseq // nset, steps),
        in_specs=in_specs,
        out_specs=out_specs,
        out_shape=out_shape,
        scratch_shapes=scratch,
        input_output_aliases=aliases,
        compiler_params=pltpu.CompilerParams(
            dimension_semantics=("arbitrary", "arbitrary"), vmem_limit_bytes=VMEM_LIMIT_BYTES),
        name="mixers",
    )(*operands)


def _pad_lanes(v, width=LANE):
    v = v.reshape(1, -1)
    return jnp.pad(v, ((0, 0), (0, width - v.shape[1])))


def kernel(x_prompt, x_sample, state_gdn, state_gdn_conv, state_ssm, state_ssm_conv, state_rwkv, state_rwkv_shift, norm1_w, w_in, gdn_conv_w, gdn_A_log, gdn_dt_bias, gdn_norm_w, ssm_conv_w, ssm_conv_b, ssm_A_log, ssm_dt_bias, ssm_D, ssm_norm_w, rwkv_mu, rwkv_w0, rwkv_w_up, rwkv_a0, rwkv_a_up, rwkv_g_up, rwkv_k_k, rwkv_k_a, rwkv_r_k, rwkv_ln_w, rwkv_ln_b, w_out, norm2_w, ffn_w_gate, ffn_w_up, ffn_w_down, final_norm_w):
    depth = w_in.shape[0]
    nbp, tp, d = x_prompt.shape
    nbs, ts, _ = x_sample.shape
    hd = state_gdn.shape[-1]
    ssm_width = state_ssm.shape[2] * hd
    cfg = Cfg(
        d_model=d, head_dim=hd, conv_w=gdn_conv_w.shape[1],
        gdn_heads=state_gdn.shape[2], ssm_heads=state_ssm.shape[2],
        ssm_groups=(ssm_conv_w.shape[2] - ssm_width) // (2 * state_ssm.shape[-1]),
        ssm_state=state_ssm.shape[-1], rwkv_heads=state_rwkv.shape[2],
        lora_w=rwkv_w_up.shape[1], lora_a=rwkv_a_up.shape[1], lora_g=rwkv_g_up.shape[1])
    assert tp % CHUNK == 0 and ts % CHUNK == 0 and tp >= cfg.conv_w and ts >= cfg.conv_w
    assert 2 * cfg.gdn_heads + cfg.ssm_heads <= LANE
    gw, sw = cfg.gdn_width, cfg.ssm_width
    gdn_cols = 4 * gw + 2 * cfg.gdn_heads
    ssm_cols = 2 * sw + 2 * cfg.ssm_bc + cfg.ssm_heads
    assert w_in.shape[2] == gdn_cols + ssm_cols + cfg.rwkv_cols

    x = jnp.concatenate([x_prompt.reshape(nbp * tp, d), x_sample.reshape(nbs * ts, d)], axis=0)
    n = x.shape[0]
    assert n % FFN_ROWS == 0 and n % PROJ_ROWS == 0

    ga = 4 * gw
    sa = gdn_cols + 2 * sw + 2 * cfg.ssm_bc
    small_w = jnp.concatenate([w_in[:, :, ga:ga + 2 * cfg.gdn_heads], w_in[:, :, sa:sa + cfg.ssm_heads]], axis=2)
    small_w = jnp.pad(small_w, ((0, 0), (0, 0), (0, LANE - small_w.shape[2])))
    w_proj = jnp.concatenate(
        [w_in[:, :, :ga], w_in[:, :, gdn_cols:gdn_cols + 2 * sw + 2 * cfg.ssm_bc],
         w_in[:, :, gdn_cols + ssm_cols:], small_w], axis=2).astype(BF16)
    assert w_proj.shape[2] == cfg.proj_cols

    def small_row(l, gdn_first, gdn_second, ssm_part):
        return _pad_lanes(jnp.concatenate([gdn_first[l], gdn_second, ssm_part[l]]))

    zeros_g = jnp.zeros((cfg.gdn_heads,), F32)
    head_of = jnp.arange(gw) // hd
    gdn_seg = (head_of[:, None] == head_of[None, :]).astype(BF16)
    wo_b, wg_b, wu_b, wd_b = (w.astype(BF16) for w in (w_out, ffn_w_gate, ffn_w_up, ffn_w_down))
    sample_states = (state_gdn, state_gdn_conv, state_ssm, state_ssm_conv, state_rwkv, state_rwkv_shift)

    p_states = [[] for _ in sample_states]
    s_states = [[] for _ in sample_states]
    for l in range(depth):
        proj = _proj_call(x, norm1_w[l], w_proj[l])
        prm = (
            gdn_conv_w[l], small_row(l, gdn_A_log, zeros_g, ssm_A_log), small_row(l, gdn_dt_bias, zeros_g, ssm_dt_bias),
            gdn_norm_w[l].reshape(1, -1),
            ssm_conv_w[l], ssm_conv_b[l].reshape(1, -1), ssm_D[l].reshape(1, -1), ssm_norm_w[l].reshape(1, -1),
            rwkv_mu[l].reshape(1, -1), rwkv_w0[l].reshape(1, -1), rwkv_w_up[l], rwkv_a0[l].reshape(1, -1),
            rwkv_a_up[l], rwkv_g_up[l], rwkv_k_k[l].reshape(1, -1), rwkv_k_a[l].reshape(1, -1),
            rwkv_r_k[l].reshape(1, -1), rwkv_ln_w[l].reshape(1, -1), rwkv_ln_b[l].reshape(1, -1),
            gdn_seg,
        )
        p_init = tuple(jnp.zeros((nbp,) + st.shape[2:], F32) for st in sample_states)
        s_init = tuple(st[l].astype(F32) for st in sample_states)
        mix, *p_new = _mixer_call(cfg, proj, 0, p_init, tp // CHUNK, prm, None)
        mix, *s_new = _mixer_call(cfg, proj, nbp * tp, s_init, ts // CHUNK, prm, mix)
        x = _ffn_call(x, mix, wo_b[l], norm2_w[l], wg_b[l], wu_b[l], wd_b[l], final_norm_w, final=(l == depth - 1))
        for acc, st in zip(p_states, p_new):
            acc.append(st)
        for acc, st in zip(s_states, s_new):
            acc.append(st)

    y_prompt = x[:nbp * tp].reshape(nbp, tp, d)
    y_sample = x[nbp * tp:].reshape(nbs, ts, d)
    return (y_prompt, y_sample, *(jnp.stack(st) for st in p_states), *(jnp.stack(st) for st in s_states))
```

```python
import functools
import math
from typing import NamedTuple

import jax
import jax.numpy as jnp
from jax import lax
from jax.experimental import pallas as pl
from jax.experimental.pallas import tpu as pltpu

F32 = jnp.float32
BF16 = jnp.bfloat16

CHUNK = 64
SLOTS = 2
SLOT_DELAY = 3
NORM_EPS = 1e-6
RWKV_GN_EPS = 64e-5
MASKED_EXPONENT = -1e30
LANE = 128
HIST = 8
VMEM_LIMIT_BYTES = 56 * 1024 * 1024
PROJ_ROWS = 256
FFN_ROWS = 512
FFN_COLS = 256


class Cfg(NamedTuple):
    d_model: int
    head_dim: int
    conv_w: int
    gdn_heads: int
    ssm_heads: int
    ssm_groups: int
    ssm_state: int
    rwkv_heads: int
    lora_w: int
    lora_a: int
    lora_g: int

    @property
    def gdn_width(self):
        return self.gdn_heads * self.head_dim

    @property
    def ssm_width(self):
        return self.ssm_heads * self.head_dim

    @property
    def ssm_bc(self):
        return self.ssm_groups * self.ssm_state

    @property
    def rwkv_width(self):
        return self.rwkv_heads * self.head_dim

    @property
    def rwkv_cols(self):
        return 3 * self.rwkv_width + self.lora_w + self.lora_a + self.lora_g

    @property
    def o_gdn_qkv(self):
        return 0

    @property
    def o_gdn_z(self):
        return 3 * self.gdn_width

    @property
    def o_ssm_z(self):
        return self.o_gdn_z + self.gdn_width

    @property
    def o_ssm_xbc(self):
        return self.o_ssm_z + self.ssm_width

    @property
    def o_rwkv(self):
        return self.o_ssm_xbc + self.ssm_width + 2 * self.ssm_bc

    @property
    def o_small(self):
        return self.o_rwkv + self.rwkv_cols

    @property
    def proj_cols(self):
        return self.o_small + LANE

    @property
    def mix_width(self):
        return self.gdn_width + self.ssm_width + self.rwkv_width


def _rms(x, w):
    return x * lax.rsqrt(jnp.mean(x * x, axis=-1, keepdims=True) + NORM_EPS) * w


def _softplus(x):
    return jnp.maximum(x, 0.0) + jnp.log1p(jnp.exp(-jnp.abs(x)))


def _silu(x):
    return x * jax.nn.sigmoid(x)


def _mm(a, b):
    return jnp.dot(a.astype(BF16), b.astype(BF16), preferred_element_type=F32)


def _mm_nt(a, b):
    return lax.dot_general(a.astype(BF16), b.astype(BF16), (((1,), (1,)), ((), ())),
                           preferred_element_type=F32)


def _split_mm(x, m):
    hi = x.astype(BF16)
    lo = (x - hi.astype(F32)).astype(BF16)
    return jnp.dot(hi, m, preferred_element_type=F32) + jnp.dot(lo, m, preferred_element_type=F32)


def _cumsum_rows(tri, x):
    hi = x.astype(BF16)
    lo = (x - hi.astype(F32)).astype(BF16)
    return jnp.dot(tri, hi, preferred_element_type=F32) + jnp.dot(tri, lo, preferred_element_type=F32)


def _inv_one_minus_steps(n, eye_f):
    c = n.shape[0]
    t = eye_f + n
    p = _mm(n, n)
    yield
    for _ in range(int(math.log2(c)) - 2):
        step = _mm(t, p)
        p_next = _mm(p, p)
        yield
        t = t + step
        p = p_next
    step = _mm(t, p)
    yield
    return t + step


def _run_interleaved(tasks):
    tasks = list(tasks)
    while tasks:
        alive = []
        for task in tasks:
            try:
                spawned = next(task)
            except StopIteration:
                continue
            alive.append(task)
            if spawned:
                alive.extend(spawned)
        tasks = alive


class _State:
    def __init__(self, ref, index):
        self.ref, self.index, self.version = ref, index, 0

    def read(self, slot):
        assert self.version == slot, "chunk slot reads a state the previous slot has not written yet"
        return self.ref[self.index]

    def write(self, slot, value):
        assert self.version == slot
        self.ref[self.index] = value
        self.version += 1


def _conv_chunk(buf_ref, x, w_ref, conv_w):
    c = x.shape[0]
    buf_ref[HIST:HIST + c, :] = x
    y = x * w_ref[conv_w - 1:conv_w, :]
    for j in range(conv_w - 1):
        lo = HIST - (conv_w - 1) + j
        y = y + buf_ref[lo:lo + c, :] * w_ref[j:j + 1, :]
    tail = buf_ref[HIST + c - (conv_w - 1):HIST + c, :]
    buf_ref[HIST - (conv_w - 1):HIST, :] = tail
    return y


def _per_set(fn, x, nset):
    rows = x.shape[0] // nset
    return jnp.concatenate([fn(s, x[s * rows:(s + 1) * rows]) for s in range(nset)], axis=0)


def _tile_bounds(parts, rows):
    bounds, lo = [], 0
    for p in parts:
        assert p.shape[0] % rows == 0
        bounds.append((lo, lo + p.shape[0] // rows))
        lo = bounds[-1][1]
    return bounds


def _part_specs(parts, rows):
    return [pl.BlockSpec((rows, p.shape[1]), lambda i, lo=lo, hi=hi: (jnp.clip(i - lo, 0, hi - lo - 1), 0))
            for p, (lo, hi) in zip(parts, _tile_bounds(parts, rows))]


def _read_part(refs, bounds):
    i = pl.program_id(0)
    x = refs[-1][...]
    for ref, (_, hi) in reversed(list(zip(refs[:-1], bounds[:-1]))):
        x = jnp.where(i < hi, ref[...], x)
    return x


def _write_part(refs, bounds, value):
    i = pl.program_id(0)
    if len(refs) == 1:
        refs[0][...] = value
        return
    for ref, (lo, hi) in zip(refs, bounds):
        @pl.when(jnp.logical_and(i >= lo, i < hi))
        def _(ref=ref):
            ref[...] = value


def _proj_body(*refs, bounds):
    x_refs, (nw_ref, w_ref, o_ref) = refs[:len(bounds)], refs[len(bounds):]
    h = _rms(_read_part(x_refs, bounds), nw_ref[...])
    o_ref[...] = jnp.dot(h.astype(BF16), w_ref[...], preferred_element_type=F32)


def _proj_call(x_parts, norm_w, w_bf16):
    d = x_parts[0].shape[1]
    n = sum(p.shape[0] for p in x_parts)
    cols = w_bf16.shape[1]
    rows = PROJ_ROWS
    return pl.pallas_call(
        functools.partial(_proj_body, bounds=_tile_bounds(x_parts, rows)),
        grid=(n // rows,),
        in_specs=_part_specs(x_parts, rows) + [
            pl.BlockSpec((1, d), lambda i: (0, 0)),
            pl.BlockSpec((d, cols), lambda i: (0, 0)),
        ],
        out_specs=pl.BlockSpec((rows, cols), lambda i: (i, 0)),
        out_shape=jax.ShapeDtypeStruct((n, cols), F32),
        compiler_params=pltpu.CompilerParams(
            dimension_semantics=("arbitrary",), vmem_limit_bytes=VMEM_LIMIT_BYTES),
        name="norm_proj",
    )(*x_parts, norm_w.reshape(1, d), w_bf16)


def _ffn_body(*refs, in_bounds, out_bounds, final):
    x_refs, refs = refs[:len(in_bounds)], refs[len(in_bounds):]
    (mix_ref, wo_ref, n2_ref, wg_ref, wu_ref, wd_ref, fn_ref), o_refs = refs[:7], refs[7:]
    x = _read_part(x_refs, in_bounds) + jnp.dot(mix_ref[...], wo_ref[...], preferred_element_type=F32)
    h2 = _rms(x, n2_ref[...]).astype(BF16)
    hidden = wg_ref.shape[1]
    acc = x
    for c0 in range(0, hidden, FFN_COLS):
        g = jnp.dot(h2, wg_ref[:, c0:c0 + FFN_COLS], preferred_element_type=F32)
        u = jnp.dot(h2, wu_ref[:, c0:c0 + FFN_COLS], preferred_element_type=F32)
        ff = (_silu(g) * u).astype(BF16)
        acc = acc + jnp.dot(ff, wd_ref[c0:c0 + FFN_COLS, :], preferred_element_type=F32)
    if final:
        acc = _rms(acc, fn_ref[...])
    _write_part(o_refs, out_bounds, acc)


def _ffn_call(x_parts, mix, wo, n2, wg, wu, wd, fn, out_rows, final):
    d = x_parts[0].shape[1]
    n = sum(p.shape[0] for p in x_parts)
    hidden = wg.shape[1]
    rows = FFN_ROWS
    assert n % rows == 0 and hidden % FFN_COLS == 0 and sum(out_rows) == n
    const = lambda i: (0, 0)
    out_shape = [jax.ShapeDtypeStruct((r, d), F32) for r in out_rows]
    return pl.pallas_call(
        functools.partial(_ffn_body, in_bounds=_tile_bounds(x_parts, rows),
                          out_bounds=_tile_bounds(out_shape, rows), final=final),
        grid=(n // rows,),
        in_specs=_part_specs(x_parts, rows) + [
            pl.BlockSpec((rows, mix.shape[1]), lambda i: (i, 0)),
            pl.BlockSpec(wo.shape, const),
            pl.BlockSpec((1, d), const),
            pl.BlockSpec(wg.shape, const),
            pl.BlockSpec(wu.shape, const),
            pl.BlockSpec(wd.shape, const),
            pl.BlockSpec((1, d), const),
        ],
        out_specs=_part_specs(out_shape, rows),
        out_shape=out_shape,
        compiler_params=pltpu.CompilerParams(
            dimension_semantics=("arbitrary",), vmem_limit_bytes=VMEM_LIMIT_BYTES),
        name="outproj_ffn",
    )(*x_parts, mix, wo, n2.reshape(1, d), wg, wu, wd, fn.reshape(1, d))


def _gdn_head(cfg, h, slot, delay, arrs, state, norm_w, masks, outs):
    q_all, k_all, kt_all, v_all, z, gc, gc_rows, beta, eg_all = arrs
    eye_f, causal_bias, offdiag_f, _ = masks
    hd = cfg.head_dim
    c = q_all.shape[0]
    for _ in range(delay):
        yield
    sl = slice(h * hd, (h + 1) * hd)
    q = q_all[:, sl]
    k = k_all[:, sl]
    kt = kt_all[sl, :]
    b = beta[:, h:h + 1]
    kb = k * b
    vb = v_all[:, sl] * b
    kq = _mm(jnp.concatenate([kb, q], axis=0), kt)
    yield
    gcol = gc[:, h:h + 1]
    grow = gc_rows[h:h + 1, :]
    dec = jnp.exp(gcol - grow + causal_bias)
    lower = kq[:c] * (dec * offdiag_f)
    attn = kq[c:] * dec
    t = yield from _inv_one_minus_steps(-lower, eye_f)
    eg = eg_all[:, h:h + 1]
    u = _mm(t, vb)
    w = _mm(t, kb * eg)
    yield
    s = state.read(slot)
    wq_s = _mm(jnp.concatenate([w, q * eg], axis=0), s)
    yield
    g_last = gcol[c - 1:c, :]
    v_new = u - wq_s[:c]
    o = wq_s[c:] + _mm(attn, v_new)
    state.write(slot, s * jnp.exp(g_last) + _mm(kt * jnp.exp(g_last - grow), v_new))
    yield
    outs[h] = _rms(o, norm_w) * _silu(z[:, sl])


def _ssd_head(cfg, h, slot, delay, arrs, state, d_row, masks, cb_cache, ys):
    xbc, xt_all, ac, ac_rows, dt, dt_rows = arrs
    _, causal_bias, _, _ = masks
    hd, ns, width = cfg.head_dim, cfg.ssm_state, cfg.ssm_width
    c = xbc.shape[0]
    for _ in range(delay):
        yield
    g = h // (cfg.ssm_heads // cfg.ssm_groups)
    bm = xbc[:, width + g * ns:width + (g + 1) * ns]
    cm = xbc[:, width + cfg.ssm_bc + g * ns:width + cfg.ssm_bc + (g + 1) * ns]
    if g not in cb_cache:
        cb_cache[g] = _mm_nt(cm, bm)
    acol = ac[:, h:h + 1]
    arow = ac_rows[h:h + 1, :]
    a_last = acol[c - 1:c, :]
    x = xbc[:, h * hd:(h + 1) * hd]
    xdt = x * dt[:, h:h + 1]
    s = state.read(slot)
    y_off = _mm_nt(cm, s)
    xt_dec = xt_all[h * hd:(h + 1) * hd, :] * (dt_rows[h:h + 1, :] * jnp.exp(a_last - arow))
    state.write(slot, s * jnp.exp(a_last) + _mm(xt_dec, bm))
    yield
    lmat = jnp.exp(acol - arow + causal_bias)
    y_diag = _mm(cb_cache[g] * lmat, xdt)
    yield
    ys[h] = y_diag + y_off * jnp.exp(acol) + d_row[:, h:h + 1] * x


def _rwkv_head(cfg, h, slot, delay, shared, state, prm, masks, incl2, outs):
    eye_f, _, _, strict2 = masks
    r, k2, v, vt_all, kk_raw, iclr, gate, pprev, pinv, pc, rt, kt = shared
    (_, _, _, _, _, _, _, r_k, ln_w, ln_b) = prm
    hd = cfg.head_dim
    c = r.shape[0]
    for _ in range(delay):
        yield
    sl = slice(h * hd, (h + 1) * hd)
    kkh = kk_raw[:, sl]
    kkh = kkh * lax.rsqrt(jnp.sum(kkh * kkh, axis=-1, keepdims=True) + NORM_EPS)
    at = -kkh * pprev[:, sl]
    bt = kkh * iclr[:, sl] * pinv[:, sl]
    vh = v[:, sl]
    pch = pc[:, sl]
    ar = jnp.concatenate([at, rt[:, sl]], axis=0)
    bk = jnp.concatenate([bt, kt[:, sl]], axis=0)
    cross = _mm_nt(ar, bk)
    vk = _mm(vt_all[sl, :], kt[:, sl] * pch)
    yield
    a_abk = jnp.where(strict2, cross[:c], 0.0)
    a_ab = a_abk[:, :c]
    aakv = _mm(a_abk[:, c:], vh)
    t = yield from _inv_one_minus_steps(a_ab, eye_f)
    s = state.read(slot)
    ar_s = _mm_nt(ar, s)
    yield
    u = _mm(t, ar_s[:c] + aakv)
    yield
    uv = jnp.concatenate([u, vh], axis=0)
    y = ar_s[c:] + _mm(jnp.where(incl2, cross[c:], 0.0), uv)
    ut = _mm_nt(eye_f, u)
    yield
    state.write(slot, s * pch + _mm(ut, bt * pch) + vk)
    yield
    mean = jnp.mean(y, axis=-1, keepdims=True)
    yc = y - mean
    var = jnp.mean(yc * yc, axis=-1, keepdims=True)
    yn = yc * lax.rsqrt(var + RWKV_GN_EPS) * ln_w[:, sl] + ln_b[:, sl]
    bonus = jnp.sum(r[:, sl] * k2[:, sl] * r_k[:, sl], axis=-1, keepdims=True) * vh
    outs[h] = (yn + bonus) * gate[:, sl]


def _rwkv_front(cfg, xm, states, slot_plan, prm, tri, masks, incl2, outs):
    (w0, w_up, a0, a_up, g_up, k_k, k_a, _, _, _) = prm
    wd = cfg.rwkv_width
    c = CHUNK
    c0 = 3 * wd
    c1 = c0 + cfg.lora_w
    c2 = c1 + cfg.lora_a
    r = xm[:, :wd]
    k = xm[:, wd:2 * wd]
    v = xm[:, 2 * wd:c0]
    lora_w = _mm(jnp.tanh(xm[:, c0:c1]), w_up)
    lora_a = _mm(xm[:, c1:c2], a_up)
    gate = _mm(jax.nn.sigmoid(xm[:, c2:]), g_up)
    yield
    w_log = -_softplus(-(w0 + lora_w)) - 0.5
    logw = -jnp.exp(w_log)
    cum = _cumsum_rows(tri, logw)
    yield
    iclr = jax.nn.sigmoid(a0 + lora_a)
    k2 = k * (1.0 + (iclr - 1.0) * k_a)
    pm = jnp.exp(cum)
    pinv = jnp.exp(-cum)
    full = (r, k2, v, k * k_k, iclr, gate, jnp.exp(cum - logw), pinv, r * pm, k2 * pinv)
    tasks = []
    for g, (s, j) in enumerate(slot_plan):
        rs = slice(g * c, (g + 1) * c)
        r_, k2_, v_, kk_, iclr_, gate_, pprev_, pinv_, rt_, kt_ = (a[rs] for a in full)
        shared = (r_, k2_, v_, v_.T, kk_, iclr_, gate_, pprev_, pinv_, pm[(g + 1) * c - 1:(g + 1) * c, :], rt_, kt_)
        tasks += [_rwkv_head(cfg, h, j, j * SLOT_DELAY, shared, states[s][h], prm, masks, incl2, outs[g])
                  for h in range(cfg.rwkv_heads)]
    yield tasks


def _mixer_body(p_ref, gdn0_ref, gdnc0_ref, ssm0_ref, ssmc0_ref, rwkv0_ref, shift0_ref,
                gconv_w_ref, alog_ref, dtb_ref, gnorm_ref,
                sconv_w_ref, sconv_b_ref, sd_ref, snorm_ref,
                mu_ref, w0_ref, wup_ref, a0_ref, aup_ref, gup_ref, kk_ref, ka_ref, rk_ref, lnw_ref, lnb_ref,
                gseg_ref, *rest, cfg, nset, aliased):
    if aliased:
        rest = rest[1:]
    (mix_ref, gdn_out_ref, gdnc_out_ref, ssm_out_ref, ssmc_out_ref, rwkv_out_ref, shift_out_ref,
     gdn_s, ssm_s, rwkv_s, gbuf, sbuf, rbuf) = rest
    c = CHUNK
    rows = p_ref.shape[0]
    nslot = rows // c
    per_set = nslot // nset
    slot_plan = [(s, j) for s in range(nset) for j in range(per_set)]
    cw = cfg.conv_w

    @pl.when(pl.program_id(1) == 0)
    def _load_state():
        gdn_s[...] = gdn0_ref[...]
        ssm_s[...] = ssm0_ref[...]
        rwkv_s[...] = rwkv0_ref[...]
        gbuf[:, HIST - (cw - 1):HIST, :] = gdnc0_ref[...]
        sbuf[:, HIST - (cw - 1):HIST, :] = ssmc0_ref[...]
        rbuf[:, HIST - 1:HIST, :] = shift0_ref[...]

    ri = lax.broadcasted_iota(jnp.int32, (c, c), 0)
    ci = lax.broadcasted_iota(jnp.int32, (c, c), 1)
    eye_f = (ri == ci).astype(F32)
    offdiag_f = 1.0 - eye_f
    causal_bias = jnp.where(ri >= ci, 0.0, MASKED_EXPONENT)
    ri2 = lax.broadcasted_iota(jnp.int32, (c, 2 * c), 0)
    ci2 = lax.broadcasted_iota(jnp.int32, (c, 2 * c), 1)
    ci2 = jnp.where(ci2 >= c, ci2 - c, ci2)
    strict2 = ri2 > ci2
    incl2 = ri2 >= ci2
    masks = (eye_f, causal_bias, offdiag_f, strict2)
    rr = lax.broadcasted_iota(jnp.int32, (rows, rows), 0)
    cc = lax.broadcasted_iota(jnp.int32, (rows, rows), 1)
    same_chunk = functools.reduce(jnp.logical_and, [(rr >= m * c) == (cc >= m * c) for m in range(1, nslot)],
                                  rr >= 0)
    tri = jnp.logical_and(same_chunk, rr >= cc).astype(BF16)

    gh, sh = cfg.gdn_heads, cfg.ssm_heads
    small = p_ref[:, cfg.o_small:cfg.o_small + LANE]
    sp = _softplus(small + dtb_ref[...])
    cum = _cumsum_rows(tri, sp * (-jnp.exp(alog_ref[...])))
    beta = jax.nn.sigmoid(small[:, gh:2 * gh])
    eg = jnp.exp(cum[:, 0:gh])

    def shift(s, pr):
        n = pr.shape[0]
        rbuf[s, HIST:HIST + n, :] = pr
        prev = rbuf[s, HIST - 1:HIST - 1 + n, :]
        rbuf[s, HIST - 1:HIST, :] = pr[n - 1:n, :]
        return pr + (prev - pr) * mu_ref[...]

    xm = _per_set(shift, p_ref[:, cfg.o_rwkv:cfg.o_rwkv + cfg.rwkv_cols], nset)
    prm = (w0_ref[...], wup_ref[...], a0_ref[...], aup_ref[...], gup_ref[...], kk_ref[...], ka_ref[...],
           rk_ref[...], lnw_ref[...], lnb_ref[...])

    qkv = _per_set(lambda s, x: _conv_chunk(gbuf.at[s], x, gconv_w_ref, cw),
                   p_ref[:, cfg.o_gdn_qkv:cfg.o_gdn_qkv + 3 * cfg.gdn_width], nset)
    qkv = _silu(qkv)
    xbc = _per_set(lambda s, x: _conv_chunk(sbuf.at[s], x, sconv_w_ref, cw),
                   p_ref[:, cfg.o_ssm_xbc:cfg.o_ssm_xbc + cfg.ssm_width + 2 * cfg.ssm_bc], nset)
    xbc = _silu(xbc + sconv_b_ref[...])
    gz = p_ref[:, cfg.o_gdn_z:cfg.o_gdn_z + cfg.gdn_width]
    sz = p_ref[:, cfg.o_ssm_z:cfg.o_ssm_z + cfg.ssm_width]
    gnorm = gnorm_ref[...]
    sd = sd_ref[...]

    gwid = cfg.gdn_width
    q_raw, k_raw, v_all = qkv[:, :gwid], qkv[:, gwid:2 * gwid], qkv[:, 2 * gwid:]
    ssq = _split_mm(jnp.concatenate([q_raw * q_raw, k_raw * k_raw], axis=0), gseg_ref[...])
    q_all = q_raw * lax.rsqrt(ssq[:rows] + NORM_EPS) * (cfg.head_dim ** -0.5)
    k_all = k_raw * lax.rsqrt(ssq[rows:] + NORM_EPS)

    gdn_states = [[_State(gdn_s, (s, h)) for h in range(gh)] for s in range(nset)]
    ssm_states = [[_State(ssm_s, (s, h)) for h in range(sh)] for s in range(nset)]
    rwkv_states = [[_State(rwkv_s, (s, h)) for h in range(cfg.rwkv_heads)] for s in range(nset)]
    gdn_o = [[None] * gh for _ in slot_plan]
    ssd_y = [[None] * sh for _ in slot_plan]
    rwkv_o = [[None] * cfg.rwkv_heads for _ in slot_plan]

    tasks = [_rwkv_front(cfg, xm, rwkv_states, slot_plan, prm, tri, masks, incl2, rwkv_o)]
    for g, (s, j) in enumerate(slot_plan):
        rs = slice(g * c, (g + 1) * c)
        cum_g, sp_g = cum[rs], sp[rs]
        cum_t, sp_t = cum_g.T, sp_g.T
        garrs = (q_all[rs], k_all[rs], k_all[rs].T, v_all[rs], gz[rs], cum_g[:, 0:gh], cum_t[0:gh, :],
                 beta[rs], eg[rs])
        tasks += [_gdn_head(cfg, h, j, j * SLOT_DELAY, garrs, gdn_states[s][h], gnorm, masks, gdn_o[g])
                  for h in range(gh)]
        sarrs = (xbc[rs], xbc[rs, :cfg.ssm_width].T, cum_g[:, 2 * gh:2 * gh + sh], cum_t[2 * gh:2 * gh + sh, :],
                 sp_g[:, 2 * gh:2 * gh + sh], sp_t[2 * gh:2 * gh + sh, :])
        cb_cache = {}
        tasks += [_ssd_head(cfg, h, j, j * SLOT_DELAY + h, sarrs, ssm_states[s][h], sd, masks, cb_cache, ssd_y[g])
                  for h in range(sh)]
    _run_interleaved(tasks)

    ng = cfg.ssm_groups
    gw = cfg.ssm_width // ng
    snorm = snorm_ref[...]
    for g in range(nslot):
        rs = slice(g * c, (g + 1) * c)
        ssd_o = []
        for grp in range(ng):
            yg = jnp.concatenate(ssd_y[g][grp * (sh // ng):(grp + 1) * (sh // ng)], axis=-1)
            yg = yg * _silu(sz[rs, grp * gw:(grp + 1) * gw])
            yg = yg * lax.rsqrt(jnp.mean(yg * yg, axis=-1, keepdims=True) + NORM_EPS)
            ssd_o.append(yg * snorm[:, grp * gw:(grp + 1) * gw])
        mix_ref[rs, :] = jnp.concatenate(gdn_o[g] + ssd_o + rwkv_o[g], axis=-1).astype(mix_ref.dtype)

    gdn_out_ref[...] = gdn_s[...]
    ssm_out_ref[...] = ssm_s[...]
    rwkv_out_ref[...] = rwkv_s[...]
    gdnc_out_ref[...] = gbuf[:, HIST - (cw - 1):HIST, :]
    ssmc_out_ref[...] = sbuf[:, HIST - (cw - 1):HIST, :]
    shift_out_ref[...] = rbuf[:, HIST - 1:HIST, :]


def _mixer_call(cfg, proj, row0, init, chunks_per_seq, prm, mix_prev):
    nseq = init[0].shape[0]
    if chunks_per_seq % SLOTS == 0:
        nset, per_set = 1, SLOTS
    elif chunks_per_seq == 1 and nseq % SLOTS == 0:
        nset, per_set = SLOTS, 1
    else:
        nset, per_set = 1, 1
    rows = nset * per_set * CHUNK
    steps = chunks_per_seq // per_set
    assert row0 % rows == 0 and nseq % nset == 0
    blk0 = row0 // rows

    def row_map(a, i):
        return (blk0 + a * steps + i, 0)

    def set_map4(a, i):
        return (a, 0, 0, 0)

    def set_map3(a, i):
        return (a, 0, 0)

    def const2(a, i):
        return (0, 0)

    def set_spec(st):
        return pl.BlockSpec((nset,) + st.shape[1:], set_map4 if st.ndim == 4 else set_map3)

    operands = [proj, *init, *prm]
    in_specs = ([pl.BlockSpec((rows, proj.shape[1]), row_map)] + [set_spec(st) for st in init]
                + [pl.BlockSpec(p.shape, const2) for p in prm])
    aliases = {}
    if mix_prev is not None:
        aliases = {len(operands): 0}
        operands.append(mix_prev)
        in_specs.append(pl.BlockSpec(memory_space=pl.ANY))
    out_specs = [pl.BlockSpec((rows, cfg.mix_width), row_map)] + [set_spec(st) for st in init]
    out_shape = [jax.ShapeDtypeStruct((proj.shape[0], cfg.mix_width), BF16)] + [
        jax.ShapeDtypeStruct(st.shape, F32) for st in init]
    gdn0, gdnc0, ssm0, ssmc0, rwkv0, shift0 = init
    scratch = [
        pltpu.VMEM((nset,) + gdn0.shape[1:], F32),
        pltpu.VMEM((nset,) + ssm0.shape[1:], F32),
        pltpu.VMEM((nset,) + rwkv0.shape[1:], F32),
        pltpu.VMEM((nset, HIST + per_set * CHUNK, gdnc0.shape[2]), F32),
        pltpu.VMEM((nset, HIST + per_set * CHUNK, ssmc0.shape[2]), F32),
        pltpu.VMEM((nset, HIST + per_set * CHUNK, shift0.shape[2]), F32),
    ]
    return pl.pallas_call(
        functools.partial(_mixer_body, cfg=cfg, nset=nset, aliased=mix_prev is not None),
        grid=(nseq // nset, steps),
        in_specs=in_specs,
        out_specs=out_specs,
        out_shape=out_shape,
        scratch_shapes=scratch,
        input_output_aliases=aliases,
        compiler_params=pltpu.CompilerParams(
            dimension_semantics=("arbitrary", "arbitrary"), vmem_limit_bytes=VMEM_LIMIT_BYTES),
        name="mixers",
    )(*operands)


def _pad_lanes(v, width=LANE):
    v = v.reshape(1, -1)
    return jnp.pad(v, ((0, 0), (0, width - v.shape[1])))


def kernel(x_prompt, x_sample, state_gdn, state_gdn_conv, state_ssm, state_ssm_conv, state_rwkv, state_rwkv_shift, norm1_w, w_in, gdn_conv_w, gdn_A_log, gdn_dt_bias, gdn_norm_w, ssm_conv_w, ssm_conv_b, ssm_A_log, ssm_dt_bias, ssm_D, ssm_norm_w, rwkv_mu, rwkv_w0, rwkv_w_up, rwkv_a0, rwkv_a_up, rwkv_g_up, rwkv_k_k, rwkv_k_a, rwkv_r_k, rwkv_ln_w, rwkv_ln_b, w_out, norm2_w, ffn_w_gate, ffn_w_up, ffn_w_down, final_norm_w):
    depth = w_in.shape[0]
    nbp, tp, d = x_prompt.shape
    nbs, ts, _ = x_sample.shape
    hd = state_gdn.shape[-1]
    ssm_width = state_ssm.shape[2] * hd
    cfg = Cfg(
        d_model=d, head_dim=hd, conv_w=gdn_conv_w.shape[1],
        gdn_heads=state_gdn.shape[2], ssm_heads=state_ssm.shape[2],
        ssm_groups=(ssm_conv_w.shape[2] - ssm_width) // (2 * state_ssm.shape[-1]),
        ssm_state=state_ssm.shape[-1], rwkv_heads=state_rwkv.shape[2],
        lora_w=rwkv_w_up.shape[1], lora_a=rwkv_a_up.shape[1], lora_g=rwkv_g_up.shape[1])
    assert tp % CHUNK == 0 and ts % CHUNK == 0 and tp >= cfg.conv_w and ts >= cfg.conv_w
    assert 2 * cfg.gdn_heads + cfg.ssm_heads <= LANE
    gw, sw = cfg.gdn_width, cfg.ssm_width
    gdn_cols = 4 * gw + 2 * cfg.gdn_heads
    ssm_cols = 2 * sw + 2 * cfg.ssm_bc + cfg.ssm_heads
    assert w_in.shape[2] == gdn_cols + ssm_cols + cfg.rwkv_cols

    x_parts = [x_prompt.reshape(nbp * tp, d), x_sample.reshape(nbs * ts, d)]

    ga = 4 * gw
    sa = gdn_cols + 2 * sw + 2 * cfg.ssm_bc
    small_w = jnp.concatenate([w_in[:, :, ga:ga + 2 * cfg.gdn_heads], w_in[:, :, sa:sa + cfg.ssm_heads]], axis=2)
    small_w = jnp.pad(small_w, ((0, 0), (0, 0), (0, LANE - small_w.shape[2])))
    w_proj = jnp.concatenate(
        [w.astype(BF16) for w in (w_in[:, :, :ga], w_in[:, :, gdn_cols:gdn_cols + 2 * sw + 2 * cfg.ssm_bc],
                                  w_in[:, :, gdn_cols + ssm_cols:], small_w)], axis=2)
    assert w_proj.shape[2] == cfg.proj_cols

    def small_row(l, gdn_first, gdn_second, ssm_part):
        return _pad_lanes(jnp.concatenate([gdn_first[l], gdn_second, ssm_part[l]]))

    zeros_g = jnp.zeros((cfg.gdn_heads,), F32)
    head_of = jnp.arange(gw) // hd
    gdn_seg = (head_of[:, None] == head_of[None, :]).astype(BF16)
    wo_b, wg_b, wu_b, wd_b = (w.astype(BF16) for w in (w_out, ffn_w_gate, ffn_w_up, ffn_w_down))
    sample_states = (state_gdn, state_gdn_conv, state_ssm, state_ssm_conv, state_rwkv, state_rwkv_shift)

    p_states = [[] for _ in sample_states]
    s_states = [[] for _ in sample_states]
    for l in range(depth):
        proj = _proj_call(x_parts, norm1_w[l], w_proj[l])
        prm = (
            gdn_conv_w[l], small_row(l, gdn_A_log, zeros_g, ssm_A_log), small_row(l, gdn_dt_bias, zeros_g, ssm_dt_bias),
            gdn_norm_w[l].reshape(1, -1),
            ssm_conv_w[l], ssm_conv_b[l].reshape(1, -1), ssm_D[l].reshape(1, -1), ssm_norm_w[l].reshape(1, -1),
            rwkv_mu[l].reshape(1, -1), rwkv_w0[l].reshape(1, -1), rwkv_w_up[l], rwkv_a0[l].reshape(1, -1),
            rwkv_a_up[l], rwkv_g_up[l], rwkv_k_k[l].reshape(1, -1), rwkv_k_a[l].reshape(1, -1),
            rwkv_r_k[l].reshape(1, -1), rwkv_ln_w[l].reshape(1, -1), rwkv_ln_b[l].reshape(1, -1),
            gdn_seg,
        )
        p_init = tuple(jnp.zeros((nbp,) + st.shape[2:], F32) for st in sample_states)
        s_init = tuple(st[l].astype(F32) for st in sample_states)
        mix, *p_new = _mixer_call(cfg, proj, 0, p_init, tp // CHUNK, prm, None)
        mix, *s_new = _mixer_call(cfg, proj, nbp * tp, s_init, ts // CHUNK, prm, mix)
        final = l == depth - 1
        out_rows = [nbp * tp, nbs * ts] if final else [nbp * tp + nbs * ts]
        x_parts = _ffn_call(x_parts, mix, wo_b[l], norm2_w[l], wg_b[l], wu_b[l], wd_b[l], final_norm_w, out_rows, final)
        for acc, st in zip(p_states, p_new):
            acc.append(st)
        for acc, st in zip(s_states, s_new):
            acc.append(st)

    y_prompt = x_parts[0].reshape(nbp, tp, d)
    y_sample = x_parts[1].reshape(nbs, ts, d)
    return (y_prompt, y_sample, *(jnp.stack(st) for st in p_states), *(jnp.stack(st) for st in s_states))
```

```python
import functools
import math
from typing import NamedTuple

import jax
import jax.numpy as jnp
from jax import lax
from jax.experimental import pallas as pl
from jax.experimental.pallas import tpu as pltpu

F32 = jnp.float32
BF16 = jnp.bfloat16

CHUNK = 64
SLOTS = 2
SLOT_DELAY = 3
NORM_EPS = 1e-6
RWKV_GN_EPS = 64e-5
MASKED_EXPONENT = -1e30
LANE = 128
HIST = 8
VMEM_LIMIT_BYTES = 56 * 1024 * 1024
PROJ_ROWS = 512
FFN_ROWS = 512
FFN_COLS = 256


class Cfg(NamedTuple):
    d_model: int
    head_dim: int
    conv_w: int
    gdn_heads: int
    ssm_heads: int
    ssm_groups: int
    ssm_state: int
    rwkv_heads: int
    lora_w: int
    lora_a: int
    lora_g: int

    @property
    def gdn_width(self):
        return self.gdn_heads * self.head_dim

    @property
    def ssm_width(self):
        return self.ssm_heads * self.head_dim

    @property
    def ssm_bc(self):
        return self.ssm_groups * self.ssm_state

    @property
    def rwkv_width(self):
        return self.rwkv_heads * self.head_dim

    @property
    def rwkv_cols(self):
        return 3 * self.rwkv_width + self.lora_w + self.lora_a + self.lora_g

    @property
    def o_gdn_qkv(self):
        return 0

    @property
    def o_gdn_z(self):
        return 3 * self.gdn_width

    @property
    def o_ssm_z(self):
        return self.o_gdn_z + self.gdn_width

    @property
    def o_ssm_xbc(self):
        return self.o_ssm_z + self.ssm_width

    @property
    def o_rwkv(self):
        return self.o_ssm_xbc + self.ssm_width + 2 * self.ssm_bc

    @property
    def o_small(self):
        return self.o_rwkv + self.rwkv_cols

    @property
    def proj_cols(self):
        return self.o_small + LANE

    @property
    def mix_width(self):
        return self.gdn_width + self.ssm_width + self.rwkv_width


def _rms(x, w):
    return x * lax.rsqrt(jnp.mean(x * x, axis=-1, keepdims=True) + NORM_EPS) * w


def _softplus(x):
    return jnp.maximum(x, 0.0) + jnp.log1p(jnp.exp(-jnp.abs(x)))


def _silu(x):
    return x * jax.nn.sigmoid(x)


def _mm(a, b):
    return jnp.dot(a.astype(BF16), b.astype(BF16), preferred_element_type=F32)


def _mm_nt(a, b):
    return lax.dot_general(a.astype(BF16), b.astype(BF16), (((1,), (1,)), ((), ())),
                           preferred_element_type=F32)


def _split_mm(x, m):
    hi = x.astype(BF16)
    lo = (x - hi.astype(F32)).astype(BF16)
    return jnp.dot(hi, m, preferred_element_type=F32) + jnp.dot(lo, m, preferred_element_type=F32)


def _cumsum_rows(tri, x):
    hi = x.astype(BF16)
    lo = (x - hi.astype(F32)).astype(BF16)
    return jnp.dot(tri, hi, preferred_element_type=F32) + jnp.dot(tri, lo, preferred_element_type=F32)


def _inv_one_minus_steps(n, eye_f):
    c = n.shape[0]
    t = eye_f + n
    p = _mm(n, n)
    yield
    for _ in range(int(math.log2(c)) - 2):
        step = _mm(t, p)
        p_next = _mm(p, p)
        yield
        t = t + step
        p = p_next
    step = _mm(t, p)
    yield
    return t + step


def _run_interleaved(tasks):
    tasks = list(tasks)
    while tasks:
        alive = []
        for task in tasks:
            try:
                spawned = next(task)
            except StopIteration:
                continue
            alive.append(task)
            if spawned:
                alive.extend(spawned)
        tasks = alive


class _State:
    def __init__(self, ref, index):
        self.ref, self.index, self.version = ref, index, 0

    def read(self, slot):
        assert self.version == slot, "chunk slot reads a state the previous slot has not written yet"
        return self.ref[self.index]

    def write(self, slot, value):
        assert self.version == slot
        self.ref[self.index] = value
        self.version += 1


def _conv_chunk(buf_ref, x, w_ref, conv_w):
    c = x.shape[0]
    buf_ref[HIST:HIST + c, :] = x
    y = x * w_ref[conv_w - 1:conv_w, :]
    for j in range(conv_w - 1):
        lo = HIST - (conv_w - 1) + j
        y = y + buf_ref[lo:lo + c, :] * w_ref[j:j + 1, :]
    tail = buf_ref[HIST + c - (conv_w - 1):HIST + c, :]
    buf_ref[HIST - (conv_w - 1):HIST, :] = tail
    return y


def _per_set(fn, x, nset):
    rows = x.shape[0] // nset
    return jnp.concatenate([fn(s, x[s * rows:(s + 1) * rows]) for s in range(nset)], axis=0)


def _tile_bounds(parts, rows):
    bounds, lo = [], 0
    for p in parts:
        assert p.shape[0] % rows == 0
        bounds.append((lo, lo + p.shape[0] // rows))
        lo = bounds[-1][1]
    return bounds


def _part_specs(parts, rows):
    return [pl.BlockSpec((rows, p.shape[1]), lambda i, lo=lo, hi=hi: (jnp.clip(i - lo, 0, hi - lo - 1), 0))
            for p, (lo, hi) in zip(parts, _tile_bounds(parts, rows))]


def _read_part(refs, bounds):
    i = pl.program_id(0)
    x = refs[-1][...]
    for ref, (_, hi) in reversed(list(zip(refs[:-1], bounds[:-1]))):
        x = jnp.where(i < hi, ref[...], x)
    return x


def _write_part(refs, bounds, value):
    i = pl.program_id(0)
    if len(refs) == 1:
        refs[0][...] = value
        return
    for ref, (lo, hi) in zip(refs, bounds):
        @pl.when(jnp.logical_and(i >= lo, i < hi))
        def _(ref=ref):
            ref[...] = value


def _proj_body(*refs, bounds):
    x_refs, (nw_ref, w_ref, o_ref) = refs[:len(bounds)], refs[len(bounds):]
    h = _rms(_read_part(x_refs, bounds), nw_ref[...])
    o_ref[...] = jnp.dot(h.astype(BF16), w_ref[...], preferred_element_type=F32)


def _proj_call(x_parts, norm_w, w_bf16):
    d = x_parts[0].shape[1]
    n = sum(p.shape[0] for p in x_parts)
    cols = w_bf16.shape[1]
    rows = PROJ_ROWS
    return pl.pallas_call(
        functools.partial(_proj_body, bounds=_tile_bounds(x_parts, rows)),
        grid=(n // rows,),
        in_specs=_part_specs(x_parts, rows) + [
            pl.BlockSpec((1, d), lambda i: (0, 0)),
            pl.BlockSpec((d, cols), lambda i: (0, 0)),
        ],
        out_specs=pl.BlockSpec((rows, cols), lambda i: (i, 0)),
        out_shape=jax.ShapeDtypeStruct((n, cols), F32),
        compiler_params=pltpu.CompilerParams(
            dimension_semantics=("arbitrary",), vmem_limit_bytes=VMEM_LIMIT_BYTES),
        name="norm_proj",
    )(*x_parts, norm_w.reshape(1, d), w_bf16)


def _ffn_body(*refs, in_bounds, mix_bounds, out_bounds, final):
    nx, nm = len(in_bounds), len(mix_bounds)
    x_refs, mix_refs, refs = refs[:nx], refs[nx:nx + nm], refs[nx + nm:]
    (wo_ref, n2_ref, wg_ref, wu_ref, wd_ref, fn_ref), o_refs = refs[:6], refs[6:]
    mix = _read_part(mix_refs, mix_bounds)
    x = _read_part(x_refs, in_bounds) + jnp.dot(mix, wo_ref[...], preferred_element_type=F32)
    h2 = _rms(x, n2_ref[...]).astype(BF16)
    hidden = wg_ref.shape[1]
    acc = x
    for c0 in range(0, hidden, FFN_COLS):
        g = jnp.dot(h2, wg_ref[:, c0:c0 + FFN_COLS], preferred_element_type=F32)
        u = jnp.dot(h2, wu_ref[:, c0:c0 + FFN_COLS], preferred_element_type=F32)
        ff = (_silu(g) * u).astype(BF16)
        acc = acc + jnp.dot(ff, wd_ref[c0:c0 + FFN_COLS, :], preferred_element_type=F32)
    if final:
        acc = _rms(acc, fn_ref[...])
    _write_part(o_refs, out_bounds, acc)


def _ffn_call(x_parts, mix_parts, wo, n2, wg, wu, wd, fn, out_rows, final):
    d = x_parts[0].shape[1]
    n = sum(p.shape[0] for p in x_parts)
    hidden = wg.shape[1]
    rows = FFN_ROWS
    assert n % rows == 0 and hidden % FFN_COLS == 0 and sum(out_rows) == n
    const = lambda i: (0, 0)
    out_shape = [jax.ShapeDtypeStruct((r, d), F32) for r in out_rows]
    return pl.pallas_call(
        functools.partial(_ffn_body, in_bounds=_tile_bounds(x_parts, rows),
                          mix_bounds=_tile_bounds(mix_parts, rows),
                          out_bounds=_tile_bounds(out_shape, rows), final=final),
        grid=(n // rows,),
        in_specs=_part_specs(x_parts, rows) + _part_specs(mix_parts, rows) + [
            pl.BlockSpec(wo.shape, const),
            pl.BlockSpec((1, d), const),
            pl.BlockSpec(wg.shape, const),
            pl.BlockSpec(wu.shape, const),
            pl.BlockSpec(wd.shape, const),
            pl.BlockSpec((1, d), const),
        ],
        out_specs=_part_specs(out_shape, rows),
        out_shape=out_shape,
        compiler_params=pltpu.CompilerParams(
            dimension_semantics=("arbitrary",), vmem_limit_bytes=VMEM_LIMIT_BYTES),
        name="outproj_ffn",
    )(*x_parts, *mix_parts, wo, n2.reshape(1, d), wg, wu, wd, fn.reshape(1, d))


def _gdn_head(cfg, h, slot, delay, arrs, state, norm_w, masks, outs):
    q_all, k_all, kt_all, v_all, z, gc, gc_rows, beta, eg_all = arrs
    eye_f, causal_bias, offdiag_f, _ = masks
    hd = cfg.head_dim
    c = q_all.shape[0]
    for _ in range(delay):
        yield
    sl = slice(h * hd, (h + 1) * hd)
    q = q_all[:, sl]
    k = k_all[:, sl]
    kt = kt_all[sl, :]
    b = beta[:, h:h + 1]
    kb = k * b
    vb = v_all[:, sl] * b
    kq = _mm(jnp.concatenate([kb, q], axis=0), kt)
    yield
    gcol = gc[:, h:h + 1]
    grow = gc_rows[h:h + 1, :]
    dec = jnp.exp(gcol - grow + causal_bias)
    lower = kq[:c] * (dec * offdiag_f)
    attn = kq[c:] * dec
    t = yield from _inv_one_minus_steps(-lower, eye_f)
    eg = eg_all[:, h:h + 1]
    u = _mm(t, vb)
    w = _mm(t, kb * eg)
    yield
    s = state.read(slot)
    wq_s = _mm(jnp.concatenate([w, q * eg], axis=0), s)
    yield
    g_last = gcol[c - 1:c, :]
    v_new = u - wq_s[:c]
    o = wq_s[c:] + _mm(attn, v_new)
    state.write(slot, s * jnp.exp(g_last) + _mm(kt * jnp.exp(g_last - grow), v_new))
    yield
    outs[h] = _rms(o, norm_w) * _silu(z[:, sl])


def _ssd_head(cfg, h, slot, delay, arrs, state, d_row, masks, cb_cache, ys):
    xbc, xt_all, ac, ac_rows, dt, dt_rows = arrs
    _, causal_bias, _, _ = masks
    hd, ns, width = cfg.head_dim, cfg.ssm_state, cfg.ssm_width
    c = xbc.shape[0]
    for _ in range(delay):
        yield
    g = h // (cfg.ssm_heads // cfg.ssm_groups)
    bm = xbc[:, width + g * ns:width + (g + 1) * ns]
    cm = xbc[:, width + cfg.ssm_bc + g * ns:width + cfg.ssm_bc + (g + 1) * ns]
    if g not in cb_cache:
        cb_cache[g] = _mm_nt(cm, bm)
    acol = ac[:, h:h + 1]
    arow = ac_rows[h:h + 1, :]
    a_last = acol[c - 1:c, :]
    x = xbc[:, h * hd:(h + 1) * hd]
    xdt = x * dt[:, h:h + 1]
    s = state.read(slot)
    y_off = _mm_nt(cm, s)
    xt_dec = xt_all[h * hd:(h + 1) * hd, :] * (dt_rows[h:h + 1, :] * jnp.exp(a_last - arow))
    state.write(slot, s * jnp.exp(a_last) + _mm(xt_dec, bm))
    yield
    lmat = jnp.exp(acol - arow + causal_bias)
    y_diag = _mm(cb_cache[g] * lmat, xdt)
    yield
    ys[h] = y_diag + y_off * jnp.exp(acol) + d_row[:, h:h + 1] * x


def _rwkv_head(cfg, h, slot, delay, shared, state, prm, masks, incl2, outs):
    eye_f, _, _, strict2 = masks
    r, k2, v, vt_all, kk_raw, iclr, gate, pprev, pinv, pc, rt, kt = shared
    (_, _, _, _, _, _, _, r_k, ln_w, ln_b) = prm
    hd = cfg.head_dim
    c = r.shape[0]
    for _ in range(delay):
        yield
    sl = slice(h * hd, (h + 1) * hd)
    kkh = kk_raw[:, sl]
    kkh = kkh * lax.rsqrt(jnp.sum(kkh * kkh, axis=-1, keepdims=True) + NORM_EPS)
    at = -kkh * pprev[:, sl]
    bt = kkh * iclr[:, sl] * pinv[:, sl]
    vh = v[:, sl]
    pch = pc[:, sl]
    ar = jnp.concatenate([at, rt[:, sl]], axis=0)
    bk = jnp.concatenate([bt, kt[:, sl]], axis=0)
    cross = _mm_nt(ar, bk)
    vk = _mm(vt_all[sl, :], kt[:, sl] * pch)
    yield
    a_abk = jnp.where(strict2, cross[:c], 0.0)
    a_ab = a_abk[:, :c]
    aakv = _mm(a_abk[:, c:], vh)
    t = yield from _inv_one_minus_steps(a_ab, eye_f)
    s = state.read(slot)
    ar_s = _mm_nt(ar, s)
    yield
    u = _mm(t, ar_s[:c] + aakv)
    yield
    uv = jnp.concatenate([u, vh], axis=0)
    y = ar_s[c:] + _mm(jnp.where(incl2, cross[c:], 0.0), uv)
    ut = _mm_nt(eye_f, u)
    yield
    state.write(slot, s * pch + _mm(ut, bt * pch) + vk)
    yield
    mean = jnp.mean(y, axis=-1, keepdims=True)
    yc = y - mean
    var = jnp.mean(yc * yc, axis=-1, keepdims=True)
    yn = yc * lax.rsqrt(var + RWKV_GN_EPS) * ln_w[:, sl] + ln_b[:, sl]
    bonus = jnp.sum(r[:, sl] * k2[:, sl] * r_k[:, sl], axis=-1, keepdims=True) * vh
    outs[h] = (yn + bonus) * gate[:, sl]


def _rwkv_front(cfg, xm, states, slot_plan, prm, tri, masks, incl2, outs):
    (w0, w_up, a0, a_up, g_up, k_k, k_a, _, _, _) = prm
    wd = cfg.rwkv_width
    c = CHUNK
    c0 = 3 * wd
    c1 = c0 + cfg.lora_w
    c2 = c1 + cfg.lora_a
    r = xm[:, :wd]
    k = xm[:, wd:2 * wd]
    v = xm[:, 2 * wd:c0]
    lora_w = _mm(jnp.tanh(xm[:, c0:c1]), w_up)
    lora_a = _mm(xm[:, c1:c2], a_up)
    gate = _mm(jax.nn.sigmoid(xm[:, c2:]), g_up)
    yield
    w_log = -_softplus(-(w0 + lora_w)) - 0.5
    logw = -jnp.exp(w_log)
    cum = _cumsum_rows(tri, logw)
    yield
    iclr = jax.nn.sigmoid(a0 + lora_a)
    k2 = k * (1.0 + (iclr - 1.0) * k_a)
    pm = jnp.exp(cum)
    pinv = jnp.exp(-cum)
    full = (r, k2, v, k * k_k, iclr, gate, jnp.exp(cum - logw), pinv, r * pm, k2 * pinv)
    tasks = []
    for g, (s, j) in enumerate(slot_plan):
        rs = slice(g * c, (g + 1) * c)
        r_, k2_, v_, kk_, iclr_, gate_, pprev_, pinv_, rt_, kt_ = (a[rs] for a in full)
        shared = (r_, k2_, v_, v_.T, kk_, iclr_, gate_, pprev_, pinv_, pm[(g + 1) * c - 1:(g + 1) * c, :], rt_, kt_)
        tasks += [_rwkv_head(cfg, h, j, j * SLOT_DELAY, shared, states[s][h], prm, masks, incl2, outs[g])
                  for h in range(cfg.rwkv_heads)]
    yield tasks


def _mixer_body(p_ref, gdn0_ref, gdnc0_ref, ssm0_ref, ssmc0_ref, rwkv0_ref, shift0_ref,
                gconv_w_ref, alog_ref, dtb_ref, gnorm_ref,
                sconv_w_ref, sconv_b_ref, sd_ref, snorm_ref,
                mu_ref, w0_ref, wup_ref, a0_ref, aup_ref, gup_ref, kk_ref, ka_ref, rk_ref, lnw_ref, lnb_ref,
                gseg_ref,
                mix_ref, gdn_out_ref, gdnc_out_ref, ssm_out_ref, ssmc_out_ref, rwkv_out_ref, shift_out_ref,
                gdn_s, ssm_s, rwkv_s, gbuf, sbuf, rbuf, *, cfg, nset):
    c = CHUNK
    rows = p_ref.shape[0]
    nslot = rows // c
    per_set = nslot // nset
    slot_plan = [(s, j) for s in range(nset) for j in range(per_set)]
    cw = cfg.conv_w

    @pl.when(pl.program_id(1) == 0)
    def _load_state():
        gdn_s[...] = gdn0_ref[...]
        ssm_s[...] = ssm0_ref[...]
        rwkv_s[...] = rwkv0_ref[...]
        gbuf[:, HIST - (cw - 1):HIST, :] = gdnc0_ref[...]
        sbuf[:, HIST - (cw - 1):HIST, :] = ssmc0_ref[...]
        rbuf[:, HIST - 1:HIST, :] = shift0_ref[...]

    ri = lax.broadcasted_iota(jnp.int32, (c, c), 0)
    ci = lax.broadcasted_iota(jnp.int32, (c, c), 1)
    eye_f = (ri == ci).astype(F32)
    offdiag_f = 1.0 - eye_f
    causal_bias = jnp.where(ri >= ci, 0.0, MASKED_EXPONENT)
    ri2 = lax.broadcasted_iota(jnp.int32, (c, 2 * c), 0)
    ci2 = lax.broadcasted_iota(jnp.int32, (c, 2 * c), 1)
    ci2 = jnp.where(ci2 >= c, ci2 - c, ci2)
    strict2 = ri2 > ci2
    incl2 = ri2 >= ci2
    masks = (eye_f, causal_bias, offdiag_f, strict2)
    rr = lax.broadcasted_iota(jnp.int32, (rows, rows), 0)
    cc = lax.broadcasted_iota(jnp.int32, (rows, rows), 1)
    same_chunk = functools.reduce(jnp.logical_and, [(rr >= m * c) == (cc >= m * c) for m in range(1, nslot)],
                                  rr >= 0)
    tri = jnp.logical_and(same_chunk, rr >= cc).astype(BF16)

    gh, sh = cfg.gdn_heads, cfg.ssm_heads
    small = p_ref[:, cfg.o_small:cfg.o_small + LANE]
    sp = _softplus(small + dtb_ref[...])
    cum = _cumsum_rows(tri, sp * (-jnp.exp(alog_ref[...])))
    beta = jax.nn.sigmoid(small[:, gh:2 * gh])
    eg = jnp.exp(cum[:, 0:gh])

    def shift(s, pr):
        n = pr.shape[0]
        rbuf[s, HIST:HIST + n, :] = pr
        prev = rbuf[s, HIST - 1:HIST - 1 + n, :]
        rbuf[s, HIST - 1:HIST, :] = pr[n - 1:n, :]
        return pr + (prev - pr) * mu_ref[...]

    xm = _per_set(shift, p_ref[:, cfg.o_rwkv:cfg.o_rwkv + cfg.rwkv_cols], nset)
    prm = (w0_ref[...], wup_ref[...], a0_ref[...], aup_ref[...], gup_ref[...], kk_ref[...], ka_ref[...],
           rk_ref[...], lnw_ref[...], lnb_ref[...])

    qkv = _per_set(lambda s, x: _conv_chunk(gbuf.at[s], x, gconv_w_ref, cw),
                   p_ref[:, cfg.o_gdn_qkv:cfg.o_gdn_qkv + 3 * cfg.gdn_width], nset)
    qkv = _silu(qkv)
    xbc = _per_set(lambda s, x: _conv_chunk(sbuf.at[s], x, sconv_w_ref, cw),
                   p_ref[:, cfg.o_ssm_xbc:cfg.o_ssm_xbc + cfg.ssm_width + 2 * cfg.ssm_bc], nset)
    xbc = _silu(xbc + sconv_b_ref[...])
    gz = p_ref[:, cfg.o_gdn_z:cfg.o_gdn_z + cfg.gdn_width]
    sz = p_ref[:, cfg.o_ssm_z:cfg.o_ssm_z + cfg.ssm_width]
    gnorm = gnorm_ref[...]
    sd = sd_ref[...]

    gwid = cfg.gdn_width
    q_raw, k_raw, v_all = qkv[:, :gwid], qkv[:, gwid:2 * gwid], qkv[:, 2 * gwid:]
    ssq = _split_mm(jnp.concatenate([q_raw * q_raw, k_raw * k_raw], axis=0), gseg_ref[...])
    q_all = q_raw * lax.rsqrt(ssq[:rows] + NORM_EPS) * (cfg.head_dim ** -0.5)
    k_all = k_raw * lax.rsqrt(ssq[rows:] + NORM_EPS)

    gdn_states = [[_State(gdn_s, (s, h)) for h in range(gh)] for s in range(nset)]
    ssm_states = [[_State(ssm_s, (s, h)) for h in range(sh)] for s in range(nset)]
    rwkv_states = [[_State(rwkv_s, (s, h)) for h in range(cfg.rwkv_heads)] for s in range(nset)]
    gdn_o = [[None] * gh for _ in slot_plan]
    ssd_y = [[None] * sh for _ in slot_plan]
    rwkv_o = [[None] * cfg.rwkv_heads for _ in slot_plan]

    tasks = [_rwkv_front(cfg, xm, rwkv_states, slot_plan, prm, tri, masks, incl2, rwkv_o)]
    for g, (s, j) in enumerate(slot_plan):
        rs = slice(g * c, (g + 1) * c)
        cum_g, sp_g = cum[rs], sp[rs]
        cum_t, sp_t = cum_g.T, sp_g.T
        garrs = (q_all[rs], k_all[rs], k_all[rs].T, v_all[rs], gz[rs], cum_g[:, 0:gh], cum_t[0:gh, :],
                 beta[rs], eg[rs])
        tasks += [_gdn_head(cfg, h, j, j * SLOT_DELAY, garrs, gdn_states[s][h], gnorm, masks, gdn_o[g])
                  for h in range(gh)]
        sarrs = (xbc[rs], xbc[rs, :cfg.ssm_width].T, cum_g[:, 2 * gh:2 * gh + sh], cum_t[2 * gh:2 * gh + sh, :],
                 sp_g[:, 2 * gh:2 * gh + sh], sp_t[2 * gh:2 * gh + sh, :])
        cb_cache = {}
        tasks += [_ssd_head(cfg, h, j, j * SLOT_DELAY + h, sarrs, ssm_states[s][h], sd, masks, cb_cache, ssd_y[g])
                  for h in range(sh)]
    _run_interleaved(tasks)

    ng = cfg.ssm_groups
    gw = cfg.ssm_width // ng
    snorm = snorm_ref[...]
    for g in range(nslot):
        rs = slice(g * c, (g + 1) * c)
        ssd_o = []
        for grp in range(ng):
            yg = jnp.concatenate(ssd_y[g][grp * (sh // ng):(grp + 1) * (sh // ng)], axis=-1)
            yg = yg * _silu(sz[rs, grp * gw:(grp + 1) * gw])
            yg = yg * lax.rsqrt(jnp.mean(yg * yg, axis=-1, keepdims=True) + NORM_EPS)
            ssd_o.append(yg * snorm[:, grp * gw:(grp + 1) * gw])
        mix_ref[rs, :] = jnp.concatenate(gdn_o[g] + ssd_o + rwkv_o[g], axis=-1).astype(mix_ref.dtype)

    gdn_out_ref[...] = gdn_s[...]
    ssm_out_ref[...] = ssm_s[...]
    rwkv_out_ref[...] = rwkv_s[...]
    gdnc_out_ref[...] = gbuf[:, HIST - (cw - 1):HIST, :]
    ssmc_out_ref[...] = sbuf[:, HIST - (cw - 1):HIST, :]
    shift_out_ref[...] = rbuf[:, HIST - 1:HIST, :]


def _mixer_call(cfg, proj, row0, init, chunks_per_seq, prm):
    nseq = init[0].shape[0]
    if chunks_per_seq % SLOTS == 0:
        nset, per_set = 1, SLOTS
    elif chunks_per_seq == 1 and nseq % SLOTS == 0:
        nset, per_set = SLOTS, 1
    else:
        nset, per_set = 1, 1
    rows = nset * per_set * CHUNK
    steps = chunks_per_seq // per_set
    assert row0 % rows == 0 and nseq % nset == 0
    blk0 = row0 // rows

    def row_map(a, i):
        return (blk0 + a * steps + i, 0)

    def out_row_map(a, i):
        return (a * steps + i, 0)

    def set_map4(a, i):
        return (a, 0, 0, 0)

    def set_map3(a, i):
        return (a, 0, 0)

    def const2(a, i):
        return (0, 0)

    def set_spec(st):
        return pl.BlockSpec((nset,) + st.shape[1:], set_map4 if st.ndim == 4 else set_map3)

    operands = [proj, *init, *prm]
    in_specs = ([pl.BlockSpec((rows, proj.shape[1]), row_map)] + [set_spec(st) for st in init]
                + [pl.BlockSpec(p.shape, const2) for p in prm])
    out_specs = [pl.BlockSpec((rows, cfg.mix_width), out_row_map)] + [set_spec(st) for st in init]
    out_shape = [jax.ShapeDtypeStruct((nseq * chunks_per_seq * CHUNK, cfg.mix_width), BF16)] + [
        jax.ShapeDtypeStruct(st.shape, F32) for st in init]
    gdn0, gdnc0, ssm0, ssmc0, rwkv0, shift0 = init
    scratch = [
        pltpu.VMEM((nset,) + gdn0.shape[1:], F32),
        pltpu.VMEM((nset,) + ssm0.shape[1:], F32),
        pltpu.VMEM((nset,) + rwkv0.shape[1:], F32),
        pltpu.VMEM((nset, HIST + per_set * CHUNK, gdnc0.shape[2]), F32),
        pltpu.VMEM((nset, HIST + per_set * CHUNK, ssmc0.shape[2]), F32),
        pltpu.VMEM((nset, HIST + per_set * CHUNK, shift0.shape[2]), F32),
    ]
    return pl.pallas_call(
        functools.partial(_mixer_body, cfg=cfg, nset=nset),
        grid=(nseq // nset, steps),
        in_specs=in_specs,
        out_specs=out_specs,
        out_shape=out_shape,
        scratch_shapes=scratch,
        compiler_params=pltpu.CompilerParams(
            dimension_semantics=("arbitrary", "arbitrary"), vmem_limit_bytes=VMEM_LIMIT_BYTES),
        name="mixers",
    )(*operands)


def _pad_lanes(v, width=LANE):
    v = v.reshape(1, -1)
    return jnp.pad(v, ((0, 0), (0, width - v.shape[1])))


def kernel(x_prompt, x_sample, state_gdn, state_gdn_conv, state_ssm, state_ssm_conv, state_rwkv, state_rwkv_shift, norm1_w, w_in, gdn_conv_w, gdn_A_log, gdn_dt_bias, gdn_norm_w, ssm_conv_w, ssm_conv_b, ssm_A_log, ssm_dt_bias, ssm_D, ssm_norm_w, rwkv_mu, rwkv_w0, rwkv_w_up, rwkv_a0, rwkv_a_up, rwkv_g_up, rwkv_k_k, rwkv_k_a, rwkv_r_k, rwkv_ln_w, rwkv_ln_b, w_out, norm2_w, ffn_w_gate, ffn_w_up, ffn_w_down, final_norm_w):
    depth = w_in.shape[0]
    nbp, tp, d = x_prompt.shape
    nbs, ts, _ = x_sample.shape
    hd = state_gdn.shape[-1]
    ssm_width = state_ssm.shape[2] * hd
    cfg = Cfg(
        d_model=d, head_dim=hd, conv_w=gdn_conv_w.shape[1],
        gdn_heads=state_gdn.shape[2], ssm_heads=state_ssm.shape[2],
        ssm_groups=(ssm_conv_w.shape[2] - ssm_width) // (2 * state_ssm.shape[-1]),
        ssm_state=state_ssm.shape[-1], rwkv_heads=state_rwkv.shape[2],
        lora_w=rwkv_w_up.shape[1], lora_a=rwkv_a_up.shape[1], lora_g=rwkv_g_up.shape[1])
    assert tp % CHUNK == 0 and ts % CHUNK == 0 and tp >= cfg.conv_w and ts >= cfg.conv_w
    assert 2 * cfg.gdn_heads + cfg.ssm_heads <= LANE
    gw, sw = cfg.gdn_width, cfg.ssm_width
    gdn_cols = 4 * gw + 2 * cfg.gdn_heads
    ssm_cols = 2 * sw + 2 * cfg.ssm_bc + cfg.ssm_heads
    assert w_in.shape[2] == gdn_cols + ssm_cols + cfg.rwkv_cols

    x_parts = [x_prompt.reshape(nbp * tp, d), x_sample.reshape(nbs * ts, d)]

    ga = 4 * gw
    sa = gdn_cols + 2 * sw + 2 * cfg.ssm_bc
    small_w = jnp.concatenate([w_in[:, :, ga:ga + 2 * cfg.gdn_heads], w_in[:, :, sa:sa + cfg.ssm_heads]], axis=2)
    small_w = jnp.pad(small_w, ((0, 0), (0, 0), (0, LANE - small_w.shape[2])))
    w_proj = jnp.concatenate(
        [w.astype(BF16) for w in (w_in[:, :, :ga], w_in[:, :, gdn_cols:gdn_cols + 2 * sw + 2 * cfg.ssm_bc],
                                  w_in[:, :, gdn_cols + ssm_cols:], small_w)], axis=2)
    assert w_proj.shape[2] == cfg.proj_cols

    def small_row(l, gdn_first, gdn_second, ssm_part):
        return _pad_lanes(jnp.concatenate([gdn_first[l], gdn_second, ssm_part[l]]))

    zeros_g = jnp.zeros((cfg.gdn_heads,), F32)
    head_of = jnp.arange(gw) // hd
    gdn_seg = (head_of[:, None] == head_of[None, :]).astype(BF16)
    wo_b, wg_b, wu_b, wd_b = (w.astype(BF16) for w in (w_out, ffn_w_gate, ffn_w_up, ffn_w_down))
    sample_states = (state_gdn, state_gdn_conv, state_ssm, state_ssm_conv, state_rwkv, state_rwkv_shift)

    p_states = [[] for _ in sample_states]
    s_states = [[] for _ in sample_states]
    for l in range(depth):
        proj = _proj_call(x_parts, norm1_w[l], w_proj[l])
        prm = (
            gdn_conv_w[l], small_row(l, gdn_A_log, zeros_g, ssm_A_log), small_row(l, gdn_dt_bias, zeros_g, ssm_dt_bias),
            gdn_norm_w[l].reshape(1, -1),
            ssm_conv_w[l], ssm_conv_b[l].reshape(1, -1), ssm_D[l].reshape(1, -1), ssm_norm_w[l].reshape(1, -1),
            rwkv_mu[l].reshape(1, -1), rwkv_w0[l].reshape(1, -1), rwkv_w_up[l], rwkv_a0[l].reshape(1, -1),
            rwkv_a_up[l], rwkv_g_up[l], rwkv_k_k[l].reshape(1, -1), rwkv_k_a[l].reshape(1, -1),
            rwkv_r_k[l].reshape(1, -1), rwkv_ln_w[l].reshape(1, -1), rwkv_ln_b[l].reshape(1, -1),
            gdn_seg,
        )
        p_init = tuple(jnp.zeros((nbp,) + st.shape[2:], F32) for st in sample_states)
        s_init = tuple(st[l].astype(F32) for st in sample_states)
        mix_p, *p_new = _mixer_call(cfg, proj, 0, p_init, tp // CHUNK, prm)
        mix_s, *s_new = _mixer_call(cfg, proj, nbp * tp, s_init, ts // CHUNK, prm)
        final = l == depth - 1
        out_rows = [nbp * tp, nbs * ts] if final else [nbp * tp + nbs * ts]
        x_parts = _ffn_call(x_parts, [mix_p, mix_s], wo_b[l], norm2_w[l], wg_b[l], wu_b[l], wd_b[l], final_norm_w,
                            out_rows, final)
        for acc, st in zip(p_states, p_new):
            acc.append(st)
        for acc, st in zip(s_states, s_new):
            acc.append(st)

    y_prompt = x_parts[0].reshape(nbp, tp, d)
    y_sample = x_parts[1].reshape(nbs, ts, d)
    return (y_prompt, y_sample, *(jnp.stack(st) for st in p_states), *(jnp.stack(st) for st in s_states))
```

```python
import functools
import math
from typing import NamedTuple

import jax
import jax.numpy as jnp
from jax import lax
from jax.experimental import pallas as pl
from jax.experimental.pallas import tpu as pltpu

F32 = jnp.float32
BF16 = jnp.bfloat16

CHUNK = 64
SLOTS = 2
PRELUDE_DELAY = 5
SLOT_DELAY = 3
NORM_EPS = 1e-6
RWKV_GN_EPS = 64e-5
MASKED_EXPONENT = -1e30
LANE = 128
HIST = 8
VMEM_LIMIT_BYTES = 56 * 1024 * 1024
PROJ_ROWS = 512
FFN_ROWS = 512
FFN_COLS = 256


class Cfg(NamedTuple):
    d_model: int
    head_dim: int
    conv_w: int
    gdn_heads: int
    ssm_heads: int
    ssm_groups: int
    ssm_state: int
    rwkv_heads: int
    lora_w: int
    lora_a: int
    lora_g: int

    @property
    def gdn_width(self):
        return self.gdn_heads * self.head_dim

    @property
    def ssm_width(self):
        return self.ssm_heads * self.head_dim

    @property
    def ssm_bc(self):
        return self.ssm_groups * self.ssm_state

    @property
    def rwkv_width(self):
        return self.rwkv_heads * self.head_dim

    @property
    def rwkv_cols(self):
        return 3 * self.rwkv_width + self.lora_w + self.lora_a + self.lora_g

    @property
    def o_gdn_qkv(self):
        return 0

    @property
    def o_gdn_z(self):
        return 3 * self.gdn_width

    @property
    def o_ssm_z(self):
        return self.o_gdn_z + self.gdn_width

    @property
    def o_ssm_xbc(self):
        return self.o_ssm_z + self.ssm_width

    @property
    def o_rwkv(self):
        return self.o_ssm_xbc + self.ssm_width + 2 * self.ssm_bc

    @property
    def o_small(self):
        return self.o_rwkv + self.rwkv_cols

    @property
    def proj_cols(self):
        return self.o_small + LANE

    @property
    def mix_width(self):
        return self.gdn_width + self.ssm_width + self.rwkv_width


def _rms(x, w):
    return x * lax.rsqrt(jnp.mean(x * x, axis=-1, keepdims=True) + NORM_EPS) * w


def _softplus(x):
    return jnp.maximum(x, 0.0) + jnp.log1p(jnp.exp(-jnp.abs(x)))


def _silu(x):
    return x * jax.nn.sigmoid(x)


def _mm(a, b):
    return jnp.dot(a.astype(BF16), b.astype(BF16), preferred_element_type=F32)


def _mm_nt(a, b):
    return lax.dot_general(a.astype(BF16), b.astype(BF16), (((1,), (1,)), ((), ())),
                           preferred_element_type=F32)


def _split_mm(x, m):
    hi = x.astype(BF16)
    lo = (x - hi.astype(F32)).astype(BF16)
    return jnp.dot(hi, m, preferred_element_type=F32) + jnp.dot(lo, m, preferred_element_type=F32)


def _cumsum_rows(tri, x):
    hi = x.astype(BF16)
    lo = (x - hi.astype(F32)).astype(BF16)
    return jnp.dot(tri, hi, preferred_element_type=F32) + jnp.dot(tri, lo, preferred_element_type=F32)


def _inv_one_minus_steps(n, eye_f):
    c = n.shape[0]
    t = eye_f + n
    p = _mm(n, n)
    yield
    for _ in range(int(math.log2(c)) - 2):
        step = _mm(t, p)
        p_next = _mm(p, p)
        yield
        t = t + step
        p = p_next
    step = _mm(t, p)
    yield
    return t + step


def _run_interleaved(tasks):
    tasks = list(tasks)
    while tasks:
        alive = []
        for task in tasks:
            try:
                spawned = next(task)
            except StopIteration:
                continue
            alive.append(task)
            if spawned:
                alive.extend(spawned)
        tasks = alive


class _State:
    def __init__(self, ref, index):
        self.ref, self.index, self.version = ref, index, 0

    def read(self, slot):
        assert self.version == slot, "chunk slot reads a state the previous slot has not written yet"
        return self.ref[self.index]

    def write(self, slot, value):
        assert self.version == slot
        self.ref[self.index] = value
        self.version += 1


def _conv_chunk(buf_ref, x, w_ref, conv_w):
    c = x.shape[0]
    buf_ref[HIST:HIST + c, :] = x
    y = x * w_ref[conv_w - 1:conv_w, :]
    for j in range(conv_w - 1):
        lo = HIST - (conv_w - 1) + j
        y = y + buf_ref[lo:lo + c, :] * w_ref[j:j + 1, :]
    tail = buf_ref[HIST + c - (conv_w - 1):HIST + c, :]
    buf_ref[HIST - (conv_w - 1):HIST, :] = tail
    return y


def _per_set(fn, x, nset):
    rows = x.shape[0] // nset
    return jnp.concatenate([fn(s, x[s * rows:(s + 1) * rows]) for s in range(nset)], axis=0)


def _tile_bounds(parts, rows):
    bounds, lo = [], 0
    for p in parts:
        assert p.shape[0] % rows == 0
        bounds.append((lo, lo + p.shape[0] // rows))
        lo = bounds[-1][1]
    return bounds


def _part_specs(parts, rows):
    return [pl.BlockSpec((rows, p.shape[1]), lambda i, lo=lo, hi=hi: (jnp.clip(i - lo, 0, hi - lo - 1), 0))
            for p, (lo, hi) in zip(parts, _tile_bounds(parts, rows))]


def _read_part(refs, bounds):
    i = pl.program_id(0)
    x = refs[-1][...]
    for ref, (_, hi) in reversed(list(zip(refs[:-1], bounds[:-1]))):
        x = jnp.where(i < hi, ref[...], x)
    return x


def _write_part(refs, bounds, value):
    i = pl.program_id(0)
    if len(refs) == 1:
        refs[0][...] = value
        return
    for ref, (lo, hi) in zip(refs, bounds):
        @pl.when(jnp.logical_and(i >= lo, i < hi))
        def _(ref=ref):
            ref[...] = value


def _proj_body(*refs, bounds):
    x_refs, (nw_ref, w_ref, o_ref) = refs[:len(bounds)], refs[len(bounds):]
    h = _rms(_read_part(x_refs, bounds), nw_ref[...])
    o_ref[...] = jnp.dot(h.astype(BF16), w_ref[...], preferred_element_type=F32)


def _proj_call(x_parts, norm_w, w_bf16):
    d = x_parts[0].shape[1]
    n = sum(p.shape[0] for p in x_parts)
    cols = w_bf16.shape[1]
    rows = PROJ_ROWS
    return pl.pallas_call(
        functools.partial(_proj_body, bounds=_tile_bounds(x_parts, rows)),
        grid=(n // rows,),
        in_specs=_part_specs(x_parts, rows) + [
            pl.BlockSpec((1, d), lambda i: (0, 0)),
            pl.BlockSpec((d, cols), lambda i: (0, 0)),
        ],
        out_specs=pl.BlockSpec((rows, cols), lambda i: (i, 0)),
        out_shape=jax.ShapeDtypeStruct((n, cols), F32),
        compiler_params=pltpu.CompilerParams(
            dimension_semantics=("arbitrary",), vmem_limit_bytes=VMEM_LIMIT_BYTES),
        name="norm_proj",
    )(*x_parts, norm_w.reshape(1, d), w_bf16)


def _ffn_body(*refs, in_bounds, mix_bounds, out_bounds, final):
    nx, nm = len(in_bounds), len(mix_bounds)
    x_refs, mix_refs, refs = refs[:nx], refs[nx:nx + nm], refs[nx + nm:]
    (wo_ref, n2_ref, wg_ref, wu_ref, wd_ref, fn_ref), o_refs = refs[:6], refs[6:]
    mix = _read_part(mix_refs, mix_bounds)
    x = _read_part(x_refs, in_bounds) + jnp.dot(mix, wo_ref[...], preferred_element_type=F32)
    h2 = _rms(x, n2_ref[...]).astype(BF16)
    hidden = wg_ref.shape[1]
    acc = x
    for c0 in range(0, hidden, FFN_COLS):
        g = jnp.dot(h2, wg_ref[:, c0:c0 + FFN_COLS], preferred_element_type=F32)
        u = jnp.dot(h2, wu_ref[:, c0:c0 + FFN_COLS], preferred_element_type=F32)
        ff = (_silu(g) * u).astype(BF16)
        acc = acc + jnp.dot(ff, wd_ref[c0:c0 + FFN_COLS, :], preferred_element_type=F32)
    if final:
        acc = _rms(acc, fn_ref[...])
    _write_part(o_refs, out_bounds, acc)


def _ffn_call(x_parts, mix_parts, wo, n2, wg, wu, wd, fn, out_rows, final):
    d = x_parts[0].shape[1]
    n = sum(p.shape[0] for p in x_parts)
    hidden = wg.shape[1]
    rows = FFN_ROWS
    assert n % rows == 0 and hidden % FFN_COLS == 0 and sum(out_rows) == n
    const = lambda i: (0, 0)
    out_shape = [jax.ShapeDtypeStruct((r, d), F32) for r in out_rows]
    return pl.pallas_call(
        functools.partial(_ffn_body, in_bounds=_tile_bounds(x_parts, rows),
                          mix_bounds=_tile_bounds(mix_parts, rows),
                          out_bounds=_tile_bounds(out_shape, rows), final=final),
        grid=(n // rows,),
        in_specs=_part_specs(x_parts, rows) + _part_specs(mix_parts, rows) + [
            pl.BlockSpec(wo.shape, const),
            pl.BlockSpec((1, d), const),
            pl.BlockSpec(wg.shape, const),
            pl.BlockSpec(wu.shape, const),
            pl.BlockSpec(wd.shape, const),
            pl.BlockSpec((1, d), const),
        ],
        out_specs=_part_specs(out_shape, rows),
        out_shape=out_shape,
        compiler_params=pltpu.CompilerParams(
            dimension_semantics=("arbitrary",), vmem_limit_bytes=VMEM_LIMIT_BYTES),
        name="outproj_ffn",
    )(*x_parts, *mix_parts, wo, n2.reshape(1, d), wg, wu, wd, fn.reshape(1, d))


def _gdn_head(cfg, h, slot, delay, rs, pre, state, norm_w, masks, outs):
    eye_f, causal_bias, offdiag_f, _ = masks
    hd, gw = cfg.head_dim, cfg.gdn_width
    c = CHUNK
    g = rs.start // c
    for _ in range(delay):
        yield
    sl = slice(h * hd, (h + 1) * hd)
    q = pre["q"][rs, sl]
    k = pre["k"][rs, sl]
    kt = pre["kt"][g * gw + h * hd:g * gw + (h + 1) * hd, :]
    b = pre["beta"][rs, cfg.gdn_heads + h:cfg.gdn_heads + h + 1]
    kb = k * b
    vb = pre["v"][rs, sl] * b
    kq = _mm(jnp.concatenate([kb, q], axis=0), kt)
    yield
    gcol = pre["cum"][rs, h:h + 1]
    grow = pre["cum_t"][g * LANE + h:g * LANE + h + 1, :]
    dec = jnp.exp(gcol - grow + causal_bias)
    lower = kq[:c] * (dec * offdiag_f)
    attn = kq[c:] * dec
    t = yield from _inv_one_minus_steps(-lower, eye_f)
    eg = pre["eg"][rs, h:h + 1]
    u = _mm(t, vb)
    w = _mm(t, kb * eg)
    yield
    s = state.read(slot)
    wq_s = _mm(jnp.concatenate([w, q * eg], axis=0), s)
    yield
    g_last = gcol[c - 1:c, :]
    v_new = u - wq_s[:c]
    o = wq_s[c:] + _mm(attn, v_new)
    state.write(slot, s * jnp.exp(g_last) + _mm(kt * jnp.exp(g_last - grow), v_new))
    yield
    outs[h] = _rms(o, norm_w) * pre["gzs"][rs, sl]


def _ssd_head(cfg, h, slot, delay, rs, pre, state, d_row, masks, cb_cache, ys):
    _, causal_bias, _, _ = masks
    hd, ns, width = cfg.head_dim, cfg.ssm_state, cfg.ssm_width
    gh = cfg.gdn_heads
    c = CHUNK
    g = rs.start // c
    for _ in range(delay):
        yield
    grp = h // (cfg.ssm_heads // cfg.ssm_groups)
    bm = pre["xbc"][rs, width + grp * ns:width + (grp + 1) * ns]
    cm = pre["xbc"][rs, width + cfg.ssm_bc + grp * ns:width + cfg.ssm_bc + (grp + 1) * ns]
    if grp not in cb_cache:
        cb_cache[grp] = _mm_nt(cm, bm)
    lane = 2 * gh + h
    acol = pre["cum"][rs, lane:lane + 1]
    arow = pre["cum_t"][g * LANE + lane:g * LANE + lane + 1, :]
    a_last = acol[c - 1:c, :]
    x = pre["xbc"][rs, h * hd:(h + 1) * hd]
    xdt = x * pre["sp"][rs, lane:lane + 1]
    s = state.read(slot)
    y_off = _mm_nt(cm, s)
    dt_row = pre["sp_t"][g * LANE + lane:g * LANE + lane + 1, :]
    xt_dec = pre["xt"][g * width + h * hd:g * width + (h + 1) * hd, :] * (dt_row * jnp.exp(a_last - arow))
    state.write(slot, s * jnp.exp(a_last) + _mm(xt_dec, bm))
    yield
    lmat = jnp.exp(acol - arow + causal_bias)
    y_diag = _mm(cb_cache[grp] * lmat, xdt)
    yield
    ys[h] = y_diag + y_off * jnp.exp(acol) + d_row[:, h:h + 1] * x


def _rwkv_head(cfg, h, slot, delay, rs, pre, state, ln_w, ln_b, masks, incl2, outs):
    eye_f, _, _, strict2 = masks
    hd, rw = cfg.head_dim, cfg.rwkv_width
    c = CHUNK
    g = rs.start // c
    for _ in range(delay):
        yield
    sl = slice(h * hd, (h + 1) * hd)
    at = pre["at"][rs, sl]
    bt = pre["bt"][rs, sl]
    kt = pre["kt_r"][rs, sl]
    vh = pre["v_r"][rs, sl]
    pch = pre["pm"][rs.stop - 1:rs.stop, sl]
    ar = jnp.concatenate([at, pre["rt"][rs, sl]], axis=0)
    bk = jnp.concatenate([bt, kt], axis=0)
    cross = _mm_nt(ar, bk)
    vk = _mm(pre["vt_r"][g * rw + h * hd:g * rw + (h + 1) * hd, :], kt * pch)
    yield
    a_abk = jnp.where(strict2, cross[:c], 0.0)
    a_ab = a_abk[:, :c]
    aakv = _mm(a_abk[:, c:], vh)
    t = yield from _inv_one_minus_steps(a_ab, eye_f)
    s = state.read(slot)
    ar_s = _mm_nt(ar, s)
    yield
    u = _mm(t, ar_s[:c] + aakv)
    yield
    uv = jnp.concatenate([u, vh], axis=0)
    y = ar_s[c:] + _mm(jnp.where(incl2, cross[c:], 0.0), uv)
    ut = _mm_nt(eye_f, u)
    yield
    state.write(slot, s * pch + _mm(ut, bt * pch) + vk)
    yield
    mean = jnp.mean(y, axis=-1, keepdims=True)
    yc = y - mean
    var = jnp.mean(yc * yc, axis=-1, keepdims=True)
    yn = yc * lax.rsqrt(var + RWKV_GN_EPS) * ln_w[:, sl] + ln_b[:, sl]
    outs[h] = (yn + pre["bonus"][rs, sl] * vh) * pre["gate"][rs, sl]


def _prelude(cfg, nset, p_ref, bufs, prm, out):
    (gconv_w_ref, alog_ref, dtb_ref, sconv_w_ref, sconv_b_ref, mu_ref, w0_ref, wup_ref, a0_ref, aup_ref,
     gup_ref, kk_ref, ka_ref, rk_ref, gseg_ref, rseg_ref) = prm
    gbuf, sbuf, rbuf = bufs
    c = CHUNK
    rows = p_ref.shape[0]
    nslot = rows // c
    cw = cfg.conv_w
    gw, sw, rw = cfg.gdn_width, cfg.ssm_width, cfg.rwkv_width
    for _ in range(PRELUDE_DELAY):
        yield

    rr = lax.broadcasted_iota(jnp.int32, (rows, rows), 0)
    cc = lax.broadcasted_iota(jnp.int32, (rows, rows), 1)
    same_chunk = functools.reduce(jnp.logical_and, [(rr >= m * c) == (cc >= m * c) for m in range(1, nslot)],
                                  rr >= 0)
    tri = jnp.logical_and(same_chunk, rr >= cc).astype(BF16)

    small = p_ref[:, cfg.o_small:cfg.o_small + LANE]
    sp = _softplus(small + dtb_ref[...])
    cum = _cumsum_rows(tri, sp * (-jnp.exp(alog_ref[...])))

    def shift(s, pr):
        n = pr.shape[0]
        rbuf[s, HIST:HIST + n, :] = pr
        prev = rbuf[s, HIST - 1:HIST - 1 + n, :]
        rbuf[s, HIST - 1:HIST, :] = pr[n - 1:n, :]
        return pr + (prev - pr) * mu_ref[...]

    xm = _per_set(shift, p_ref[:, cfg.o_rwkv:cfg.o_rwkv + cfg.rwkv_cols], nset)
    c0 = 3 * rw
    c1 = c0 + cfg.lora_w
    c2 = c1 + cfg.lora_a
    r = xm[:, :rw]
    k = xm[:, rw:2 * rw]
    v = xm[:, 2 * rw:c0]
    lora_w = _mm(jnp.tanh(xm[:, c0:c1]), wup_ref[...])
    lora_a = _mm(xm[:, c1:c2], aup_ref[...])
    gate = _mm(jax.nn.sigmoid(xm[:, c2:]), gup_ref[...])

    qkv = _per_set(lambda s, x: _conv_chunk(gbuf.at[s], x, gconv_w_ref, cw),
                   p_ref[:, cfg.o_gdn_qkv:cfg.o_gdn_qkv + 3 * gw], nset)
    qkv = _silu(qkv)
    q_raw, k_raw = qkv[:, :gw], qkv[:, gw:2 * gw]
    ssq = _split_mm(jnp.concatenate([q_raw * q_raw, k_raw * k_raw], axis=0), gseg_ref[...])
    yield
    w_log = -_softplus(-(w0_ref[...] + lora_w)) - 0.5
    logw = -jnp.exp(w_log)
    rcum = _cumsum_rows(tri, logw)
    iclr = jax.nn.sigmoid(a0_ref[...] + lora_a)
    k2 = k * (1.0 + (iclr - 1.0) * ka_ref[...])
    kk_raw = k * kk_ref[...]
    kk_ssq = _split_mm(kk_raw * kk_raw, rseg_ref[...])
    bonus = _split_mm(r * k2 * rk_ref[...], rseg_ref[...])
    yield
    xbc = _per_set(lambda s, x: _conv_chunk(sbuf.at[s], x, sconv_w_ref, cw),
                   p_ref[:, cfg.o_ssm_xbc:cfg.o_ssm_xbc + sw + 2 * cfg.ssm_bc], nset)
    xbc = _silu(xbc + sconv_b_ref[...])
    k_all = k_raw * lax.rsqrt(ssq[rows:] + NORM_EPS)
    pm = jnp.exp(rcum)
    pinv = jnp.exp(-rcum)
    kkn = kk_raw * lax.rsqrt(kk_ssq + NORM_EPS)
    out.update(
        q=q_raw * lax.rsqrt(ssq[:rows] + NORM_EPS) * (cfg.head_dim ** -0.5), k=k_all, v=qkv[:, 2 * gw:],
        gzs=_silu(p_ref[:, cfg.o_gdn_z:cfg.o_gdn_z + gw]),
        cum=cum, sp=sp, beta=jax.nn.sigmoid(small), eg=jnp.exp(cum),
        xbc=xbc, szs=_silu(p_ref[:, cfg.o_ssm_z:cfg.o_ssm_z + sw]),
        at=-kkn * jnp.exp(rcum - logw), bt=kkn * iclr * pinv, rt=r * pm, kt_r=k2 * pinv, v_r=v, gate=gate,
        bonus=bonus, pm=pm,
        kt=jnp.concatenate([k_all[g * c:(g + 1) * c].T for g in range(nslot)], axis=0),
        cum_t=jnp.concatenate([cum[g * c:(g + 1) * c].T for g in range(nslot)], axis=0),
        sp_t=jnp.concatenate([sp[g * c:(g + 1) * c].T for g in range(nslot)], axis=0),
        xt=jnp.concatenate([xbc[g * c:(g + 1) * c, :sw].T for g in range(nslot)], axis=0),
        vt_r=jnp.concatenate([v[g * c:(g + 1) * c].T for g in range(nslot)], axis=0),
    )


def _prelude_shapes(cfg, rows):
    nslot = rows // CHUNK
    gw, sw, rw = cfg.gdn_width, cfg.ssm_width, cfg.rwkv_width
    shapes = {name: (rows, gw) for name in ("q", "k", "v", "gzs")}
    shapes.update({name: (rows, LANE) for name in ("cum", "sp", "beta", "eg")})
    shapes.update(xbc=(rows, sw + 2 * cfg.ssm_bc), szs=(rows, sw))
    shapes.update({name: (rows, rw) for name in ("at", "bt", "rt", "kt_r", "v_r", "gate", "bonus", "pm")})
    shapes.update(kt=(nslot * gw, CHUNK), cum_t=(nslot * LANE, CHUNK), sp_t=(nslot * LANE, CHUNK),
                  xt=(nslot * sw, CHUNK), vt_r=(nslot * rw, CHUNK))
    return shapes


def _mixer_body(p_ref, gdn0_ref, gdnc0_ref, ssm0_ref, ssmc0_ref, rwkv0_ref, shift0_ref,
                gconv_w_ref, alog_ref, dtb_ref, gnorm_ref,
                sconv_w_ref, sconv_b_ref, sd_ref, snorm_ref,
                mu_ref, w0_ref, wup_ref, a0_ref, aup_ref, gup_ref, kk_ref, ka_ref, rk_ref, lnw_ref, lnb_ref,
                gseg_ref, rseg_ref,
                mix_ref, gdn_out_ref, gdnc_out_ref, ssm_out_ref, ssmc_out_ref, rwkv_out_ref, shift_out_ref,
                gdn_s, ssm_s, rwkv_s, gbuf, sbuf, rbuf, *pre_refs, cfg, nset, steps, nblocks, names):
    pre = dict(zip(names, pre_refs))
    c = CHUNK
    rows = p_ref.shape[0]
    nslot = rows // c
    per_set = nslot // nset
    slot_plan = [(s, j) for s in range(nset) for j in range(per_set)]
    cw = cfg.conv_w
    gh, sh = cfg.gdn_heads, cfg.ssm_heads
    t = pl.program_id(0)
    p_block = jnp.minimum(t, nblocks - 1)
    c_block = jnp.maximum(t - 1, 0)

    @pl.when(t == 0)
    def _clear():
        for ref in (gdn_s, ssm_s, rwkv_s) + tuple(pre_refs):
            ref[...] = jnp.zeros(ref.shape, ref.dtype)

    @pl.when(p_block % steps == 0)
    def _load_history():
        gbuf[:, HIST - (cw - 1):HIST, :] = gdnc0_ref[...]
        sbuf[:, HIST - (cw - 1):HIST, :] = ssmc0_ref[...]
        rbuf[:, HIST - 1:HIST, :] = shift0_ref[...]

    @pl.when(jnp.logical_and(t >= 1, c_block % steps == 0))
    def _load_state():
        gdn_s[...] = gdn0_ref[...]
        ssm_s[...] = ssm0_ref[...]
        rwkv_s[...] = rwkv0_ref[...]

    ri = lax.broadcasted_iota(jnp.int32, (c, c), 0)
    ci = lax.broadcasted_iota(jnp.int32, (c, c), 1)
    eye_f = (ri == ci).astype(F32)
    offdiag_f = 1.0 - eye_f
    causal_bias = jnp.where(ri >= ci, 0.0, MASKED_EXPONENT)
    ri2 = lax.broadcasted_iota(jnp.int32, (c, 2 * c), 0)
    ci2 = lax.broadcasted_iota(jnp.int32, (c, 2 * c), 1)
    ci2 = jnp.where(ci2 >= c, ci2 - c, ci2)
    strict2 = ri2 > ci2
    incl2 = ri2 >= ci2
    masks = (eye_f, causal_bias, offdiag_f, strict2)

    gnorm, sd, ln_w, ln_b = gnorm_ref[...], sd_ref[...], lnw_ref[...], lnb_ref[...]
    gdn_states = [[_State(gdn_s, (s, h)) for h in range(gh)] for s in range(nset)]
    ssm_states = [[_State(ssm_s, (s, h)) for h in range(sh)] for s in range(nset)]
    rwkv_states = [[_State(rwkv_s, (s, h)) for h in range(cfg.rwkv_heads)] for s in range(nset)]
    gdn_o = [[None] * gh for _ in slot_plan]
    ssd_y = [[None] * sh for _ in slot_plan]
    rwkv_o = [[None] * cfg.rwkv_heads for _ in slot_plan]

    new_pre = {}
    prm = (gconv_w_ref, alog_ref, dtb_ref, sconv_w_ref, sconv_b_ref, mu_ref, w0_ref, wup_ref, a0_ref, aup_ref,
           gup_ref, kk_ref, ka_ref, rk_ref, gseg_ref, rseg_ref)
    tasks = []
    for g, (s, j) in enumerate(slot_plan):
        rs = slice(g * c, (g + 1) * c)
        delay = j * SLOT_DELAY
        tasks += [_gdn_head(cfg, h, j, delay, rs, pre, gdn_states[s][h], gnorm, masks, gdn_o[g]) for h in range(gh)]
        tasks += [_rwkv_head(cfg, h, j, delay, rs, pre, rwkv_states[s][h], ln_w, ln_b, masks, incl2, rwkv_o[g])
                  for h in range(cfg.rwkv_heads)]
        cb_cache = {}
        tasks += [_ssd_head(cfg, h, j, delay + h, rs, pre, ssm_states[s][h], sd, masks, cb_cache, ssd_y[g])
                  for h in range(sh)]
    tasks.append(_prelude(cfg, nset, p_ref, (gbuf, sbuf, rbuf), prm, new_pre))
    _run_interleaved(tasks)

    ng = cfg.ssm_groups
    gw = cfg.ssm_width // ng
    snorm = snorm_ref[...]
    for g in range(nslot):
        rs = slice(g * c, (g + 1) * c)
        ssd_o = []
        for grp in range(ng):
            yg = jnp.concatenate(ssd_y[g][grp * (sh // ng):(grp + 1) * (sh // ng)], axis=-1)
            yg = yg * pre["szs"][rs, grp * gw:(grp + 1) * gw]
            yg = yg * lax.rsqrt(jnp.mean(yg * yg, axis=-1, keepdims=True) + NORM_EPS)
            ssd_o.append(yg * snorm[:, grp * gw:(grp + 1) * gw])
        mix_ref[rs, :] = jnp.concatenate(gdn_o[g] + ssd_o + rwkv_o[g], axis=-1).astype(mix_ref.dtype)

    gdn_out_ref[...] = gdn_s[...]
    ssm_out_ref[...] = ssm_s[...]
    rwkv_out_ref[...] = rwkv_s[...]
    gdnc_out_ref[...] = gbuf[:, HIST - (cw - 1):HIST, :]
    ssmc_out_ref[...] = sbuf[:, HIST - (cw - 1):HIST, :]
    shift_out_ref[...] = rbuf[:, HIST - 1:HIST, :]
    for name in names:
        pre[name][...] = new_pre[name]


def _mixer_call(cfg, proj, row0, init, chunks_per_seq, prm):
    nseq = init[0].shape[0]
    if chunks_per_seq % SLOTS == 0:
        nset, per_set = 1, SLOTS
    elif chunks_per_seq == 1 and nseq % SLOTS == 0:
        nset, per_set = SLOTS, 1
    else:
        nset, per_set = 1, 1
    rows = nset * per_set * CHUNK
    steps = chunks_per_seq // per_set
    assert row0 % rows == 0 and nseq % nset == 0
    blk0 = row0 // rows
    nblocks = (nseq // nset) * steps

    def p_block(t):
        return jnp.minimum(t, nblocks - 1)

    def c_block(t):
        return jnp.maximum(t - 1, 0)

    def set_spec(st, block_of):
        nd = st.ndim
        return pl.BlockSpec((nset,) + st.shape[1:], lambda t: (block_of(t) // steps,) + (0,) * (nd - 1))

    gdn0, gdnc0, ssm0, ssmc0, rwkv0, shift0 = init
    side = (c_block, p_block, c_block, p_block, c_block, p_block)
    in_specs = ([pl.BlockSpec((rows, proj.shape[1]), lambda t: (blk0 + p_block(t), 0))]
                + [set_spec(st, blk) for st, blk in zip(init, side)]
                + [pl.BlockSpec(p.shape, lambda t: (0, 0)) for p in prm])
    out_specs = ([pl.BlockSpec((rows, cfg.mix_width), lambda t: (c_block(t), 0))]
                 + [set_spec(st, blk) for st, blk in zip(init, side)])
    out_shape = [jax.ShapeDtypeStruct((nseq * chunks_per_seq * CHUNK, cfg.mix_width), BF16)] + [
        jax.ShapeDtypeStruct(st.shape, F32) for st in init]
    shapes = _prelude_shapes(cfg, rows)
    names = tuple(shapes)
    scratch = [
        pltpu.VMEM((nset,) + gdn0.shape[1:], F32),
        pltpu.VMEM((nset,) + ssm0.shape[1:], F32),
        pltpu.VMEM((nset,) + rwkv0.shape[1:], F32),
        pltpu.VMEM((nset, HIST + per_set * CHUNK, gdnc0.shape[2]), F32),
        pltpu.VMEM((nset, HIST + per_set * CHUNK, ssmc0.shape[2]), F32),
        pltpu.VMEM((nset, HIST + per_set * CHUNK, shift0.shape[2]), F32),
    ] + [pltpu.VMEM(shapes[name], F32) for name in names]
    return pl.pallas_call(
        functools.partial(_mixer_body, cfg=cfg, nset=nset, steps=steps, nblocks=nblocks, names=names),
        grid=(nblocks + 1,),
        in_specs=in_specs,
        out_specs=out_specs,
        out_shape=out_shape,
        scratch_shapes=scratch,
        compiler_params=pltpu.CompilerParams(
            dimension_semantics=("arbitrary",), vmem_limit_bytes=VMEM_LIMIT_BYTES),
        name="mixers",
    )(proj, *init, *prm)


def _pad_lanes(v, width=LANE):
    v = v.reshape(1, -1)
    return jnp.pad(v, ((0, 0), (0, width - v.shape[1])))


def kernel(x_prompt, x_sample, state_gdn, state_gdn_conv, state_ssm, state_ssm_conv, state_rwkv, state_rwkv_shift, norm1_w, w_in, gdn_conv_w, gdn_A_log, gdn_dt_bias, gdn_norm_w, ssm_conv_w, ssm_conv_b, ssm_A_log, ssm_dt_bias, ssm_D, ssm_norm_w, rwkv_mu, rwkv_w0, rwkv_w_up, rwkv_a0, rwkv_a_up, rwkv_g_up, rwkv_k_k, rwkv_k_a, rwkv_r_k, rwkv_ln_w, rwkv_ln_b, w_out, norm2_w, ffn_w_gate, ffn_w_up, ffn_w_down, final_norm_w):
    depth = w_in.shape[0]
    nbp, tp, d = x_prompt.shape
    nbs, ts, _ = x_sample.shape
    hd = state_gdn.shape[-1]
    ssm_width = state_ssm.shape[2] * hd
    cfg = Cfg(
        d_model=d, head_dim=hd, conv_w=gdn_conv_w.shape[1],
        gdn_heads=state_gdn.shape[2], ssm_heads=state_ssm.shape[2],
        ssm_groups=(ssm_conv_w.shape[2] - ssm_width) // (2 * state_ssm.shape[-1]),
        ssm_state=state_ssm.shape[-1], rwkv_heads=state_rwkv.shape[2],
        lora_w=rwkv_w_up.shape[1], lora_a=rwkv_a_up.shape[1], lora_g=rwkv_g_up.shape[1])
    assert tp % CHUNK == 0 and ts % CHUNK == 0 and tp >= cfg.conv_w and ts >= cfg.conv_w
    assert 2 * cfg.gdn_heads + cfg.ssm_heads <= LANE
    gw, sw = cfg.gdn_width, cfg.ssm_width
    gdn_cols = 4 * gw + 2 * cfg.gdn_heads
    ssm_cols = 2 * sw + 2 * cfg.ssm_bc + cfg.ssm_heads
    assert w_in.shape[2] == gdn_cols + ssm_cols + cfg.rwkv_cols

    x_parts = [x_prompt.reshape(nbp * tp, d), x_sample.reshape(nbs * ts, d)]

    ga = 4 * gw
    sa = gdn_cols + 2 * sw + 2 * cfg.ssm_bc
    small_w = jnp.concatenate([w_in[:, :, ga:ga + 2 * cfg.gdn_heads], w_in[:, :, sa:sa + cfg.ssm_heads]], axis=2)
    small_w = jnp.pad(small_w, ((0, 0), (0, 0), (0, LANE - small_w.shape[2])))
    w_proj = jnp.concatenate(
        [w.astype(BF16) for w in (w_in[:, :, :ga], w_in[:, :, gdn_cols:gdn_cols + 2 * sw + 2 * cfg.ssm_bc],
                                  w_in[:, :, gdn_cols + ssm_cols:], small_w)], axis=2)
    assert w_proj.shape[2] == cfg.proj_cols

    def small_row(l, gdn_first, gdn_second, ssm_part):
        return _pad_lanes(jnp.concatenate([gdn_first[l], gdn_second, ssm_part[l]]))

    zeros_g = jnp.zeros((cfg.gdn_heads,), F32)
    head_of = jnp.arange(gw) // hd
    gdn_seg = (head_of[:, None] == head_of[None, :]).astype(BF16)
    head_of = jnp.arange(cfg.rwkv_width) // hd
    rwkv_seg = (head_of[:, None] == head_of[None, :]).astype(BF16)
    wo_b, wg_b, wu_b, wd_b = (w.astype(BF16) for w in (w_out, ffn_w_gate, ffn_w_up, ffn_w_down))
    sample_states = (state_gdn, state_gdn_conv, state_ssm, state_ssm_conv, state_rwkv, state_rwkv_shift)

    p_states = [[] for _ in sample_states]
    s_states = [[] for _ in sample_states]
    for l in range(depth):
        proj = _proj_call(x_parts, norm1_w[l], w_proj[l])
        prm = (
            gdn_conv_w[l], small_row(l, gdn_A_log, zeros_g, ssm_A_log), small_row(l, gdn_dt_bias, zeros_g, ssm_dt_bias),
            gdn_norm_w[l].reshape(1, -1),
            ssm_conv_w[l], ssm_conv_b[l].reshape(1, -1), ssm_D[l].reshape(1, -1), ssm_norm_w[l].reshape(1, -1),
            rwkv_mu[l].reshape(1, -1), rwkv_w0[l].reshape(1, -1), rwkv_w_up[l], rwkv_a0[l].reshape(1, -1),
            rwkv_a_up[l], rwkv_g_up[l], rwkv_k_k[l].reshape(1, -1), rwkv_k_a[l].reshape(1, -1),
            rwkv_r_k[l].reshape(1, -1), rwkv_ln_w[l].reshape(1, -1), rwkv_ln_b[l].reshape(1, -1),
            gdn_seg, rwkv_seg,
        )
        p_init = tuple(jnp.zeros((nbp,) + st.shape[2:], F32) for st in sample_states)
        s_init = tuple(st[l].astype(F32) for st in sample_states)
        mix_p, *p_new = _mixer_call(cfg, proj, 0, p_init, tp // CHUNK, prm)
        mix_s, *s_new = _mixer_call(cfg, proj, nbp * tp, s_init, ts // CHUNK, prm)
        final = l == depth - 1
        out_rows = [nbp * tp, nbs * ts] if final else [nbp * tp + nbs * ts]
        x_parts = _ffn_call(x_parts, [mix_p, mix_s], wo_b[l], norm2_w[l], wg_b[l], wu_b[l], wd_b[l], final_norm_w,
                            out_rows, final)
        for acc, st in zip(p_states, p_new):
            acc.append(st)
        for acc, st in zip(s_states, s_new):
            acc.append(st)

    y_prompt = x_parts[0].reshape(nbp, tp, d)
    y_sample = x_parts[1].reshape(nbs, ts, d)
    return (y_prompt, y_sample, *(jnp.stack(st) for st in p_states), *(jnp.stack(st) for st in s_states))
```

```python
import functools
import math
from typing import NamedTuple

import jax
import jax.numpy as jnp
from jax import lax
from jax.experimental import pallas as pl
from jax.experimental.pallas import tpu as pltpu

F32 = jnp.float32
BF16 = jnp.bfloat16

CHUNK = 64
SLOTS = 2
PRELUDE_DELAY = 5
SLOT_DELAY = 3
NORM_EPS = 1e-6
RWKV_GN_EPS = 64e-5
MASKED_EXPONENT = -1e30
LANE = 128
HIST = 8
VMEM_LIMIT_BYTES = 56 * 1024 * 1024
PROJ_ROWS = 512
FFN_ROWS = 512
FFN_COLS = 256


class Cfg(NamedTuple):
    d_model: int
    head_dim: int
    conv_w: int
    gdn_heads: int
    ssm_heads: int
    ssm_groups: int
    ssm_state: int
    rwkv_heads: int
    lora_w: int
    lora_a: int
    lora_g: int

    @property
    def gdn_width(self):
        return self.gdn_heads * self.head_dim

    @property
    def ssm_width(self):
        return self.ssm_heads * self.head_dim

    @property
    def ssm_bc(self):
        return self.ssm_groups * self.ssm_state

    @property
    def rwkv_width(self):
        return self.rwkv_heads * self.head_dim

    @property
    def rwkv_cols(self):
        return 3 * self.rwkv_width + self.lora_w + self.lora_a + self.lora_g

    @property
    def o_gdn_qkv(self):
        return 0

    @property
    def o_gdn_z(self):
        return 3 * self.gdn_width

    @property
    def o_ssm_z(self):
        return self.o_gdn_z + self.gdn_width

    @property
    def o_ssm_xbc(self):
        return self.o_ssm_z + self.ssm_width

    @property
    def o_rwkv(self):
        return self.o_ssm_xbc + self.ssm_width + 2 * self.ssm_bc

    @property
    def o_small(self):
        return self.o_rwkv + self.rwkv_cols

    @property
    def proj_cols(self):
        return self.o_small + LANE

    @property
    def mix_width(self):
        return self.gdn_width + self.ssm_width + self.rwkv_width


def _rms(x, w):
    return x * lax.rsqrt(jnp.mean(x * x, axis=-1, keepdims=True) + NORM_EPS) * w


def _softplus(x):
    return jnp.maximum(x, 0.0) + jnp.log1p(jnp.exp(-jnp.abs(x)))


def _silu(x):
    return x * jax.nn.sigmoid(x)


def _mm(a, b):
    return jnp.dot(a.astype(BF16), b.astype(BF16), preferred_element_type=F32)


def _mm_nt(a, b):
    return lax.dot_general(a.astype(BF16), b.astype(BF16), (((1,), (1,)), ((), ())),
                           preferred_element_type=F32)


def _split_mm(x, m):
    hi = x.astype(BF16)
    lo = (x - hi.astype(F32)).astype(BF16)
    return jnp.dot(hi, m, preferred_element_type=F32) + jnp.dot(lo, m, preferred_element_type=F32)


def _cumsum_rows(tri, x):
    hi = x.astype(BF16)
    lo = (x - hi.astype(F32)).astype(BF16)
    return jnp.dot(tri, hi, preferred_element_type=F32) + jnp.dot(tri, lo, preferred_element_type=F32)


def _inv_one_minus_steps(n, eye_f):
    c = n.shape[0]
    t = eye_f + n
    p = _mm(n, n)
    yield
    for _ in range(int(math.log2(c)) - 2):
        step = _mm(t, p)
        p_next = _mm(p, p)
        yield
        t = t + step
        p = p_next
    step = _mm(t, p)
    yield
    return t + step


def _run_interleaved(tasks):
    tasks = list(tasks)
    while tasks:
        alive = []
        for task in tasks:
            try:
                spawned = next(task)
            except StopIteration:
                continue
            alive.append(task)
            if spawned:
                alive.extend(spawned)
        tasks = alive


class _State:
    def __init__(self, ref, index):
        self.ref, self.index, self.version = ref, index, 0

    def read(self, slot):
        assert self.version == slot, "chunk slot reads a state the previous slot has not written yet"
        return self.ref[self.index]

    def write(self, slot, value):
        assert self.version == slot
        self.ref[self.index] = value
        self.version += 1


def _conv_chunk(buf_ref, x, w_ref, conv_w):
    c = x.shape[0]
    buf_ref[HIST:HIST + c, :] = x
    y = x * w_ref[conv_w - 1:conv_w, :]
    for j in range(conv_w - 1):
        lo = HIST - (conv_w - 1) + j
        y = y + buf_ref[lo:lo + c, :] * w_ref[j:j + 1, :]
    tail = buf_ref[HIST + c - (conv_w - 1):HIST + c, :]
    buf_ref[HIST - (conv_w - 1):HIST, :] = tail
    return y


def _per_set(fn, x, nset):
    rows = x.shape[0] // nset
    return jnp.concatenate([fn(s, x[s * rows:(s + 1) * rows]) for s in range(nset)], axis=0)


def _tile_bounds(parts, rows):
    bounds, lo = [], 0
    for p in parts:
        assert p.shape[0] % rows == 0
        bounds.append((lo, lo + p.shape[0] // rows))
        lo = bounds[-1][1]
    return bounds


def _part_specs(parts, rows):
    return [pl.BlockSpec((rows, p.shape[1]), lambda i, lo=lo, hi=hi: (jnp.clip(i - lo, 0, hi - lo - 1), 0))
            for p, (lo, hi) in zip(parts, _tile_bounds(parts, rows))]


def _read_part(refs, bounds):
    i = pl.program_id(0)
    x = refs[-1][...]
    for ref, (_, hi) in reversed(list(zip(refs[:-1], bounds[:-1]))):
        x = jnp.where(i < hi, ref[...], x)
    return x


def _write_part(refs, bounds, value):
    i = pl.program_id(0)
    if len(refs) == 1:
        refs[0][...] = value
        return
    for ref, (lo, hi) in zip(refs, bounds):
        @pl.when(jnp.logical_and(i >= lo, i < hi))
        def _(ref=ref):
            ref[...] = value


def _proj_body(*refs, bounds, segments, small_cols):
    x_refs, (nw_ref, w_ref, o_ref, w_scr) = refs[:len(bounds)], refs[len(bounds):]

    @pl.when(pl.program_id(0) == 0)
    def _regroup():
        dst = 0
        for src, width in segments:
            w_scr[:, dst:dst + width] = w_ref[:, src:src + width].astype(BF16)
            dst += width
        small = jnp.concatenate([w_ref[:, src:src + width] for src, width in small_cols], axis=1)
        small = jnp.concatenate([small, jnp.zeros((small.shape[0], LANE - small.shape[1]), F32)], axis=1)
        w_scr[:, dst:dst + LANE] = small.astype(BF16)

    h = _rms(_read_part(x_refs, bounds), nw_ref[...])
    o_ref[...] = jnp.dot(h.astype(BF16), w_scr[...], preferred_element_type=F32)


def _proj_call(cfg, x_parts, norm_w, w_in):
    d = x_parts[0].shape[1]
    n = sum(p.shape[0] for p in x_parts)
    rows = PROJ_ROWS
    gw, sw = cfg.gdn_width, cfg.ssm_width
    gdn_cols = 4 * gw + 2 * cfg.gdn_heads
    ssm_cols = 2 * sw + 2 * cfg.ssm_bc + cfg.ssm_heads
    assert w_in.shape[1] == gdn_cols + ssm_cols + cfg.rwkv_cols
    segments = ((0, 4 * gw), (gdn_cols, 2 * sw + 2 * cfg.ssm_bc), (gdn_cols + ssm_cols, cfg.rwkv_cols))
    small_cols = ((4 * gw, 2 * cfg.gdn_heads), (gdn_cols + 2 * sw + 2 * cfg.ssm_bc, cfg.ssm_heads))
    cols = cfg.proj_cols
    return pl.pallas_call(
        functools.partial(_proj_body, bounds=_tile_bounds(x_parts, rows), segments=segments, small_cols=small_cols),
        grid=(n // rows,),
        in_specs=_part_specs(x_parts, rows) + [
            pl.BlockSpec((1, d), lambda i: (0, 0)),
            pl.BlockSpec(w_in.shape, lambda i: (0, 0), pipeline_mode=pl.Buffered(1)),
        ],
        out_specs=pl.BlockSpec((rows, cols), lambda i: (i, 0)),
        out_shape=jax.ShapeDtypeStruct((n, cols), F32),
        scratch_shapes=[pltpu.VMEM((d, cols), BF16)],
        compiler_params=pltpu.CompilerParams(
            dimension_semantics=("arbitrary",), vmem_limit_bytes=VMEM_LIMIT_BYTES),
        name="norm_proj",
    )(*x_parts, norm_w.reshape(1, d), w_in)


def _ffn_body(*refs, in_bounds, mix_bounds, out_bounds, final):
    nx, nm = len(in_bounds), len(mix_bounds)
    x_refs, mix_refs, refs = refs[:nx], refs[nx:nx + nm], refs[nx + nm:]
    (wo_ref, n2_ref, wg_ref, wu_ref, wd_ref, fn_ref), o_refs = refs[:6], refs[6:]
    mix = _read_part(mix_refs, mix_bounds)
    x = _read_part(x_refs, in_bounds) + jnp.dot(mix, wo_ref[...], preferred_element_type=F32)
    h2 = _rms(x, n2_ref[...]).astype(BF16)
    hidden = wg_ref.shape[1]
    acc = x
    for c0 in range(0, hidden, FFN_COLS):
        g = jnp.dot(h2, wg_ref[:, c0:c0 + FFN_COLS], preferred_element_type=F32)
        u = jnp.dot(h2, wu_ref[:, c0:c0 + FFN_COLS], preferred_element_type=F32)
        ff = (_silu(g) * u).astype(BF16)
        acc = acc + jnp.dot(ff, wd_ref[c0:c0 + FFN_COLS, :], preferred_element_type=F32)
    if final:
        acc = _rms(acc, fn_ref[...])
    _write_part(o_refs, out_bounds, acc)


def _ffn_call(x_parts, mix_parts, wo, n2, wg, wu, wd, fn, out_rows, final):
    d = x_parts[0].shape[1]
    n = sum(p.shape[0] for p in x_parts)
    hidden = wg.shape[1]
    rows = FFN_ROWS
    assert n % rows == 0 and hidden % FFN_COLS == 0 and sum(out_rows) == n
    const = lambda i: (0, 0)
    out_shape = [jax.ShapeDtypeStruct((r, d), F32) for r in out_rows]
    return pl.pallas_call(
        functools.partial(_ffn_body, in_bounds=_tile_bounds(x_parts, rows),
                          mix_bounds=_tile_bounds(mix_parts, rows),
                          out_bounds=_tile_bounds(out_shape, rows), final=final),
        grid=(n // rows,),
        in_specs=_part_specs(x_parts, rows) + _part_specs(mix_parts, rows) + [
            pl.BlockSpec(wo.shape, const),
            pl.BlockSpec((1, d), const),
            pl.BlockSpec(wg.shape, const),
            pl.BlockSpec(wu.shape, const),
            pl.BlockSpec(wd.shape, const),
            pl.BlockSpec((1, d), const),
        ],
        out_specs=_part_specs(out_shape, rows),
        out_shape=out_shape,
        compiler_params=pltpu.CompilerParams(
            dimension_semantics=("arbitrary",), vmem_limit_bytes=VMEM_LIMIT_BYTES),
        name="outproj_ffn",
    )(*x_parts, *mix_parts, wo, n2.reshape(1, d), wg, wu, wd, fn.reshape(1, d))


def _gdn_head(cfg, h, slot, delay, rs, pre, state, norm_w, masks, outs):
    eye_f, causal_bias, offdiag_f, _ = masks
    hd, gw = cfg.head_dim, cfg.gdn_width
    c = CHUNK
    g = rs.start // c
    for _ in range(delay):
        yield
    sl = slice(h * hd, (h + 1) * hd)
    q = pre["q"][rs, sl]
    k = pre["k"][rs, sl]
    kt = pre["kt"][g * gw + h * hd:g * gw + (h + 1) * hd, :]
    b = pre["beta"][rs, cfg.gdn_heads + h:cfg.gdn_heads + h + 1]
    kb = k * b
    vb = pre["v"][rs, sl] * b
    kq = _mm(jnp.concatenate([kb, q], axis=0), kt)
    yield
    gcol = pre["cum"][rs, h:h + 1]
    grow = pre["cum_t"][g * LANE + h:g * LANE + h + 1, :]
    dec = jnp.exp(gcol - grow + causal_bias)
    lower = kq[:c] * (dec * offdiag_f)
    attn = kq[c:] * dec
    t = yield from _inv_one_minus_steps(-lower, eye_f)
    eg = pre["eg"][rs, h:h + 1]
    u = _mm(t, vb)
    w = _mm(t, kb * eg)
    yield
    s = state.read(slot)
    wq_s = _mm(jnp.concatenate([w, q * eg], axis=0), s)
    yield
    g_last = gcol[c - 1:c, :]
    v_new = u - wq_s[:c]
    o = wq_s[c:] + _mm(attn, v_new)
    state.write(slot, s * jnp.exp(g_last) + _mm(kt * jnp.exp(g_last - grow), v_new))
    yield
    outs[h] = _rms(o, norm_w) * pre["gzs"][rs, sl]


def _ssd_head(cfg, h, slot, delay, rs, pre, state, d_row, masks, cb_cache, ys):
    _, causal_bias, _, _ = masks
    hd, ns, width = cfg.head_dim, cfg.ssm_state, cfg.ssm_width
    gh = cfg.gdn_heads
    c = CHUNK
    g = rs.start // c
    for _ in range(delay):
        yield
    grp = h // (cfg.ssm_heads // cfg.ssm_groups)
    bm = pre["xbc"][rs, width + grp * ns:width + (grp + 1) * ns]
    cm = pre["xbc"][rs, width + cfg.ssm_bc + grp * ns:width + cfg.ssm_bc + (grp + 1) * ns]
    if grp not in cb_cache:
        cb_cache[grp] = _mm_nt(cm, bm)
    lane = 2 * gh + h
    acol = pre["cum"][rs, lane:lane + 1]
    arow = pre["cum_t"][g * LANE + lane:g * LANE + lane + 1, :]
    a_last = acol[c - 1:c, :]
    x = pre["xbc"][rs, h * hd:(h + 1) * hd]
    xdt = x * pre["sp"][rs, lane:lane + 1]
    s = state.read(slot)
    y_off = _mm_nt(cm, s)
    dt_row = pre["sp_t"][g * LANE + lane:g * LANE + lane + 1, :]
    xt_dec = pre["xt"][g * width + h * hd:g * width + (h + 1) * hd, :] * (dt_row * jnp.exp(a_last - arow))
    state.write(slot, s * jnp.exp(a_last) + _mm(xt_dec, bm))
    yield
    lmat = jnp.exp(acol - arow + causal_bias)
    y_diag = _mm(cb_cache[grp] * lmat, xdt)
    yield
    ys[h] = y_diag + y_off * jnp.exp(acol) + d_row[:, h:h + 1] * x


def _rwkv_head(cfg, h, slot, delay, rs, pre, state, ln_w, ln_b, masks, incl2, outs):
    eye_f, _, _, strict2 = masks
    hd, rw = cfg.head_dim, cfg.rwkv_width
    c = CHUNK
    g = rs.start // c
    for _ in range(delay):
        yield
    sl = slice(h * hd, (h + 1) * hd)
    at = pre["at"][rs, sl]
    bt = pre["bt"][rs, sl]
    kt = pre["kt_r"][rs, sl]
    vh = pre["v_r"][rs, sl]
    pch = pre["pm"][rs.stop - 1:rs.stop, sl]
    ar = jnp.concatenate([at, pre["rt"][rs, sl]], axis=0)
    bk = jnp.concatenate([bt, kt], axis=0)
    cross = _mm_nt(ar, bk)
    vk = _mm(pre["vt_r"][g * rw + h * hd:g * rw + (h + 1) * hd, :], kt * pch)
    yield
    a_abk = jnp.where(strict2, cross[:c], 0.0)
    a_ab = a_abk[:, :c]
    aakv = _mm(a_abk[:, c:], vh)
    t = yield from _inv_one_minus_steps(a_ab, eye_f)
    s = state.read(slot)
    ar_s = _mm_nt(ar, s)
    yield
    u = _mm(t, ar_s[:c] + aakv)
    yield
    uv = jnp.concatenate([u, vh], axis=0)
    y = ar_s[c:] + _mm(jnp.where(incl2, cross[c:], 0.0), uv)
    ut = _mm_nt(eye_f, u)
    yield
    state.write(slot, s * pch + _mm(ut, bt * pch) + vk)
    yield
    mean = jnp.mean(y, axis=-1, keepdims=True)
    yc = y - mean
    var = jnp.mean(yc * yc, axis=-1, keepdims=True)
    yn = yc * lax.rsqrt(var + RWKV_GN_EPS) * ln_w[:, sl] + ln_b[:, sl]
    outs[h] = (yn + pre["bonus"][rs, sl] * vh) * pre["gate"][rs, sl]


def _prelude(cfg, nset, p_ref, bufs, prm, out):
    (gconv_w_ref, alog_ref, dtb_ref, sconv_w_ref, sconv_b_ref, mu_ref, w0_ref, wup_ref, a0_ref, aup_ref,
     gup_ref, kk_ref, ka_ref, rk_ref, gseg_ref, rseg_ref) = prm
    gbuf, sbuf, rbuf = bufs
    c = CHUNK
    rows = p_ref.shape[0]
    nslot = rows // c
    cw = cfg.conv_w
    gw, sw, rw = cfg.gdn_width, cfg.ssm_width, cfg.rwkv_width
    for _ in range(PRELUDE_DELAY):
        yield

    rr = lax.broadcasted_iota(jnp.int32, (rows, rows), 0)
    cc = lax.broadcasted_iota(jnp.int32, (rows, rows), 1)
    same_chunk = functools.reduce(jnp.logical_and, [(rr >= m * c) == (cc >= m * c) for m in range(1, nslot)],
                                  rr >= 0)
    tri = jnp.logical_and(same_chunk, rr >= cc).astype(BF16)

    small = p_ref[:, cfg.o_small:cfg.o_small + LANE]
    sp = _softplus(small + dtb_ref[...])
    cum = _cumsum_rows(tri, sp * (-jnp.exp(alog_ref[...])))

    def shift(s, pr):
        n = pr.shape[0]
        rbuf[s, HIST:HIST + n, :] = pr
        prev = rbuf[s, HIST - 1:HIST - 1 + n, :]
        rbuf[s, HIST - 1:HIST, :] = pr[n - 1:n, :]
        return pr + (prev - pr) * mu_ref[...]

    xm = _per_set(shift, p_ref[:, cfg.o_rwkv:cfg.o_rwkv + cfg.rwkv_cols], nset)
    c0 = 3 * rw
    c1 = c0 + cfg.lora_w
    c2 = c1 + cfg.lora_a
    r = xm[:, :rw]
    k = xm[:, rw:2 * rw]
    v = xm[:, 2 * rw:c0]
    lora_w = _mm(jnp.tanh(xm[:, c0:c1]), wup_ref[...])
    lora_a = _mm(xm[:, c1:c2], aup_ref[...])
    gate = _mm(jax.nn.sigmoid(xm[:, c2:]), gup_ref[...])

    qkv = _per_set(lambda s, x: _conv_chunk(gbuf.at[s], x, gconv_w_ref, cw),
                   p_ref[:, cfg.o_gdn_qkv:cfg.o_gdn_qkv + 3 * gw], nset)
    qkv = _silu(qkv)
    q_raw, k_raw = qkv[:, :gw], qkv[:, gw:2 * gw]
    ssq = _split_mm(jnp.concatenate([q_raw * q_raw, k_raw * k_raw], axis=0), gseg_ref[...])
    yield
    w_log = -_softplus(-(w0_ref[...] + lora_w)) - 0.5
    logw = -jnp.exp(w_log)
    rcum = _cumsum_rows(tri, logw)
    iclr = jax.nn.sigmoid(a0_ref[...] + lora_a)
    k2 = k * (1.0 + (iclr - 1.0) * ka_ref[...])
    kk_raw = k * kk_ref[...]
    kk_ssq = _split_mm(kk_raw * kk_raw, rseg_ref[...])
    bonus = _split_mm(r * k2 * rk_ref[...], rseg_ref[...])
    yield
    xbc = _per_set(lambda s, x: _conv_chunk(sbuf.at[s], x, sconv_w_ref, cw),
                   p_ref[:, cfg.o_ssm_xbc:cfg.o_ssm_xbc + sw + 2 * cfg.ssm_bc], nset)
    xbc = _silu(xbc + sconv_b_ref[...])
    k_all = k_raw * lax.rsqrt(ssq[rows:] + NORM_EPS)
    pm = jnp.exp(rcum)
    pinv = jnp.exp(-rcum)
    kkn = kk_raw * lax.rsqrt(kk_ssq + NORM_EPS)
    out.update(
        q=q_raw * lax.rsqrt(ssq[:rows] + NORM_EPS) * (cfg.head_dim ** -0.5), k=k_all, v=qkv[:, 2 * gw:],
        gzs=_silu(p_ref[:, cfg.o_gdn_z:cfg.o_gdn_z + gw]),
        cum=cum, sp=sp, beta=jax.nn.sigmoid(small), eg=jnp.exp(cum),
        xbc=xbc, szs=_silu(p_ref[:, cfg.o_ssm_z:cfg.o_ssm_z + sw]),
        at=-kkn * jnp.exp(rcum - logw), bt=kkn * iclr * pinv, rt=r * pm, kt_r=k2 * pinv, v_r=v, gate=gate,
        bonus=bonus, pm=pm,
        kt=jnp.concatenate([k_all[g * c:(g + 1) * c].T for g in range(nslot)], axis=0),
        cum_t=jnp.concatenate([cum[g * c:(g + 1) * c].T for g in range(nslot)], axis=0),
        sp_t=jnp.concatenate([sp[g * c:(g + 1) * c].T for g in range(nslot)], axis=0),
        xt=jnp.concatenate([xbc[g * c:(g + 1) * c, :sw].T for g in range(nslot)], axis=0),
        vt_r=jnp.concatenate([v[g * c:(g + 1) * c].T for g in range(nslot)], axis=0),
    )


def _prelude_shapes(cfg, rows):
    nslot = rows // CHUNK
    gw, sw, rw = cfg.gdn_width, cfg.ssm_width, cfg.rwkv_width
    shapes = {name: (rows, gw) for name in ("q", "k", "v", "gzs")}
    shapes.update({name: (rows, LANE) for name in ("cum", "sp", "beta", "eg")})
    shapes.update(xbc=(rows, sw + 2 * cfg.ssm_bc), szs=(rows, sw))
    shapes.update({name: (rows, rw) for name in ("at", "bt", "rt", "kt_r", "v_r", "gate", "bonus", "pm")})
    shapes.update(kt=(nslot * gw, CHUNK), cum_t=(nslot * LANE, CHUNK), sp_t=(nslot * LANE, CHUNK),
                  xt=(nslot * sw, CHUNK), vt_r=(nslot * rw, CHUNK))
    return shapes


def _mixer_body(p_ref, gdn0_ref, gdnc0_ref, ssm0_ref, ssmc0_ref, rwkv0_ref, shift0_ref,
                gconv_w_ref, alog_ref, dtb_ref, gnorm_ref,
                sconv_w_ref, sconv_b_ref, sd_ref, snorm_ref,
                mu_ref, w0_ref, wup_ref, a0_ref, aup_ref, gup_ref, kk_ref, ka_ref, rk_ref, lnw_ref, lnb_ref,
                gseg_ref, rseg_ref,
                mix_ref, gdn_out_ref, gdnc_out_ref, ssm_out_ref, ssmc_out_ref, rwkv_out_ref, shift_out_ref,
                gdn_s, ssm_s, rwkv_s, gbuf, sbuf, rbuf, *pre_refs, cfg, nset, steps, nblocks, names):
    pre = dict(zip(names, pre_refs))
    c = CHUNK
    rows = p_ref.shape[0]
    nslot = rows // c
    per_set = nslot // nset
    slot_plan = [(s, j) for s in range(nset) for j in range(per_set)]
    cw = cfg.conv_w
    gh, sh = cfg.gdn_heads, cfg.ssm_heads
    t = pl.program_id(0)
    p_block = jnp.minimum(t, nblocks - 1)
    c_block = jnp.maximum(t - 1, 0)

    @pl.when(t == 0)
    def _clear():
        for ref in (gdn_s, ssm_s, rwkv_s) + tuple(pre_refs):
            ref[...] = jnp.zeros(ref.shape, ref.dtype)

    @pl.when(p_block % steps == 0)
    def _load_history():
        gbuf[:, HIST - (cw - 1):HIST, :] = gdnc0_ref[...]
        sbuf[:, HIST - (cw - 1):HIST, :] = ssmc0_ref[...]
        rbuf[:, HIST - 1:HIST, :] = shift0_ref[...]

    @pl.when(jnp.logical_and(t >= 1, c_block % steps == 0))
    def _load_state():
        gdn_s[...] = gdn0_ref[...]
        ssm_s[...] = ssm0_ref[...]
        rwkv_s[...] = rwkv0_ref[...]

    ri = lax.broadcasted_iota(jnp.int32, (c, c), 0)
    ci = lax.broadcasted_iota(jnp.int32, (c, c), 1)
    eye_f = (ri == ci).astype(F32)
    offdiag_f = 1.0 - eye_f
    causal_bias = jnp.where(ri >= ci, 0.0, MASKED_EXPONENT)
    ri2 = lax.broadcasted_iota(jnp.int32, (c, 2 * c), 0)
    ci2 = lax.broadcasted_iota(jnp.int32, (c, 2 * c), 1)
    ci2 = jnp.where(ci2 >= c, ci2 - c, ci2)
    strict2 = ri2 > ci2
    incl2 = ri2 >= ci2
    masks = (eye_f, causal_bias, offdiag_f, strict2)

    gnorm, sd, ln_w, ln_b = gnorm_ref[...], sd_ref[...], lnw_ref[...], lnb_ref[...]
    gdn_states = [[_State(gdn_s, (s, h)) for h in range(gh)] for s in range(nset)]
    ssm_states = [[_State(ssm_s, (s, h)) for h in range(sh)] for s in range(nset)]
    rwkv_states = [[_State(rwkv_s, (s, h)) for h in range(cfg.rwkv_heads)] for s in range(nset)]
    gdn_o = [[None] * gh for _ in slot_plan]
    ssd_y = [[None] * sh for _ in slot_plan]
    rwkv_o = [[None] * cfg.rwkv_heads for _ in slot_plan]

    new_pre = {}
    prm = (gconv_w_ref, alog_ref, dtb_ref, sconv_w_ref, sconv_b_ref, mu_ref, w0_ref, wup_ref, a0_ref, aup_ref,
           gup_ref, kk_ref, ka_ref, rk_ref, gseg_ref, rseg_ref)
    tasks = []
    for g, (s, j) in enumerate(slot_plan):
        rs = slice(g * c, (g + 1) * c)
        delay = j * SLOT_DELAY
        tasks += [_gdn_head(cfg, h, j, delay, rs, pre, gdn_states[s][h], gnorm, masks, gdn_o[g]) for h in range(gh)]
        tasks += [_rwkv_head(cfg, h, j, delay, rs, pre, rwkv_states[s][h], ln_w, ln_b, masks, incl2, rwkv_o[g])
                  for h in range(cfg.rwkv_heads)]
        cb_cache = {}
        tasks += [_ssd_head(cfg, h, j, delay + h, rs, pre, ssm_states[s][h], sd, masks, cb_cache, ssd_y[g])
                  for h in range(sh)]
    tasks.append(_prelude(cfg, nset, p_ref, (gbuf, sbuf, rbuf), prm, new_pre))
    _run_interleaved(tasks)

    ng = cfg.ssm_groups
    gw = cfg.ssm_width // ng
    snorm = snorm_ref[...]
    for g in range(nslot):
        rs = slice(g * c, (g + 1) * c)
        ssd_o = []
        for grp in range(ng):
            yg = jnp.concatenate(ssd_y[g][grp * (sh // ng):(grp + 1) * (sh // ng)], axis=-1)
            yg = yg * pre["szs"][rs, grp * gw:(grp + 1) * gw]
            yg = yg * lax.rsqrt(jnp.mean(yg * yg, axis=-1, keepdims=True) + NORM_EPS)
            ssd_o.append(yg * snorm[:, grp * gw:(grp + 1) * gw])
        mix_ref[rs, :] = jnp.concatenate(gdn_o[g] + ssd_o + rwkv_o[g], axis=-1).astype(mix_ref.dtype)

    gdn_out_ref[...] = gdn_s[...]
    ssm_out_ref[...] = ssm_s[...]
    rwkv_out_ref[...] = rwkv_s[...]
    gdnc_out_ref[...] = gbuf[:, HIST - (cw - 1):HIST, :]
    ssmc_out_ref[...] = sbuf[:, HIST - (cw - 1):HIST, :]
    shift_out_ref[...] = rbuf[:, HIST - 1:HIST, :]
    for name in names:
        pre[name][...] = new_pre[name]


def _mixer_call(cfg, proj, row0, init, chunks_per_seq, prm):
    nseq = init[0].shape[0]
    if chunks_per_seq % SLOTS == 0:
        nset, per_set = 1, SLOTS
    elif chunks_per_seq == 1 and nseq % SLOTS == 0:
        nset, per_set = SLOTS, 1
    else:
        nset, per_set = 1, 1
    rows = nset * per_set * CHUNK
    steps = chunks_per_seq // per_set
    assert row0 % rows == 0 and nseq % nset == 0
    blk0 = row0 // rows
    nblocks = (nseq // nset) * steps

    def p_block(t):
        return jnp.minimum(t, nblocks - 1)

    def c_block(t):
        return jnp.maximum(t - 1, 0)

    def set_spec(st, block_of):
        nd = st.ndim
        return pl.BlockSpec((nset,) + st.shape[1:], lambda t: (block_of(t) // steps,) + (0,) * (nd - 1))

    gdn0, gdnc0, ssm0, ssmc0, rwkv0, shift0 = init
    side = (c_block, p_block, c_block, p_block, c_block, p_block)
    in_specs = ([pl.BlockSpec((rows, proj.shape[1]), lambda t: (blk0 + p_block(t), 0))]
                + [set_spec(st, blk) for st, blk in zip(init, side)]
                + [pl.BlockSpec(p.shape, lambda t: (0, 0)) for p in prm])
    out_specs = ([pl.BlockSpec((rows, cfg.mix_width), lambda t: (c_block(t), 0))]
                 + [set_spec(st, blk) for st, blk in zip(init, side)])
    out_shape = [jax.ShapeDtypeStruct((nseq * chunks_per_seq * CHUNK, cfg.mix_width), BF16)] + [
        jax.ShapeDtypeStruct(st.shape, F32) for st in init]
    shapes = _prelude_shapes(cfg, rows)
    names = tuple(shapes)
    scratch = [
        pltpu.VMEM((nset,) + gdn0.shape[1:], F32),
        pltpu.VMEM((nset,) + ssm0.shape[1:], F32),
        pltpu.VMEM((nset,) + rwkv0.shape[1:], F32),
        pltpu.VMEM((nset, HIST + per_set * CHUNK, gdnc0.shape[2]), F32),
        pltpu.VMEM((nset, HIST + per_set * CHUNK, ssmc0.shape[2]), F32),
        pltpu.VMEM((nset, HIST + per_set * CHUNK, shift0.shape[2]), F32),
    ] + [pltpu.VMEM(shapes[name], F32) for name in names]
    return pl.pallas_call(
        functools.partial(_mixer_body, cfg=cfg, nset=nset, steps=steps, nblocks=nblocks, names=names),
        grid=(nblocks + 1,),
        in_specs=in_specs,
        out_specs=out_specs,
        out_shape=out_shape,
        scratch_shapes=scratch,
        compiler_params=pltpu.CompilerParams(
            dimension_semantics=("arbitrary",), vmem_limit_bytes=VMEM_LIMIT_BYTES),
        name="mixers",
    )(proj, *init, *prm)


def _pad_lanes(v, width=LANE):
    v = v.reshape(1, -1)
    return jnp.pad(v, ((0, 0), (0, width - v.shape[1])))


def kernel(x_prompt, x_sample, state_gdn, state_gdn_conv, state_ssm, state_ssm_conv, state_rwkv, state_rwkv_shift, norm1_w, w_in, gdn_conv_w, gdn_A_log, gdn_dt_bias, gdn_norm_w, ssm_conv_w, ssm_conv_b, ssm_A_log, ssm_dt_bias, ssm_D, ssm_norm_w, rwkv_mu, rwkv_w0, rwkv_w_up, rwkv_a0, rwkv_a_up, rwkv_g_up, rwkv_k_k, rwkv_k_a, rwkv_r_k, rwkv_ln_w, rwkv_ln_b, w_out, norm2_w, ffn_w_gate, ffn_w_up, ffn_w_down, final_norm_w):
    depth = w_in.shape[0]
    nbp, tp, d = x_prompt.shape
    nbs, ts, _ = x_sample.shape
    hd = state_gdn.shape[-1]
    ssm_width = state_ssm.shape[2] * hd
    cfg = Cfg(
        d_model=d, head_dim=hd, conv_w=gdn_conv_w.shape[1],
        gdn_heads=state_gdn.shape[2], ssm_heads=state_ssm.shape[2],
        ssm_groups=(ssm_conv_w.shape[2] - ssm_width) // (2 * state_ssm.shape[-1]),
        ssm_state=state_ssm.shape[-1], rwkv_heads=state_rwkv.shape[2],
        lora_w=rwkv_w_up.shape[1], lora_a=rwkv_a_up.shape[1], lora_g=rwkv_g_up.shape[1])
    assert tp % CHUNK == 0 and ts % CHUNK == 0 and tp >= cfg.conv_w and ts >= cfg.conv_w
    assert 2 * cfg.gdn_heads + cfg.ssm_heads <= LANE
    gw, sw = cfg.gdn_width, cfg.ssm_width
    gdn_cols = 4 * gw + 2 * cfg.gdn_heads
    ssm_cols = 2 * sw + 2 * cfg.ssm_bc + cfg.ssm_heads
    assert w_in.shape[2] == gdn_cols + ssm_cols + cfg.rwkv_cols

    x_parts = [x_prompt.reshape(nbp * tp, d), x_sample.reshape(nbs * ts, d)]

    def small_row(l, gdn_first, gdn_second, ssm_part):
        return _pad_lanes(jnp.concatenate([gdn_first[l], gdn_second, ssm_part[l]]))

    zeros_g = jnp.zeros((cfg.gdn_heads,), F32)
    head_of = jnp.arange(gw) // hd
    gdn_seg = (head_of[:, None] == head_of[None, :]).astype(BF16)
    head_of = jnp.arange(cfg.rwkv_width) // hd
    rwkv_seg = (head_of[:, None] == head_of[None, :]).astype(BF16)
    wo_b, wg_b, wu_b, wd_b = (w.astype(BF16) for w in (w_out, ffn_w_gate, ffn_w_up, ffn_w_down))
    sample_states = (state_gdn, state_gdn_conv, state_ssm, state_ssm_conv, state_rwkv, state_rwkv_shift)

    p_states = [[] for _ in sample_states]
    s_states = [[] for _ in sample_states]
    for l in range(depth):
        proj = _proj_call(cfg, x_parts, norm1_w[l], w_in[l])
        prm = (
            gdn_conv_w[l], small_row(l, gdn_A_log, zeros_g, ssm_A_log), small_row(l, gdn_dt_bias, zeros_g, ssm_dt_bias),
            gdn_norm_w[l].reshape(1, -1),
            ssm_conv_w[l], ssm_conv_b[l].reshape(1, -1), ssm_D[l].reshape(1, -1), ssm_norm_w[l].reshape(1, -1),
            rwkv_mu[l].reshape(1, -1), rwkv_w0[l].reshape(1, -1), rwkv_w_up[l], rwkv_a0[l].reshape(1, -1),
            rwkv_a_up[l], rwkv_g_up[l], rwkv_k_k[l].reshape(1, -1), rwkv_k_a[l].reshape(1, -1),
            rwkv_r_k[l].reshape(1, -1), rwkv_ln_w[l].reshape(1, -1), rwkv_ln_b[l].reshape(1, -1),
            gdn_seg, rwkv_seg,
        )
        p_init = tuple(jnp.zeros((nbp,) + st.shape[2:], F32) for st in sample_states)
        s_init = tuple(st[l].astype(F32) for st in sample_states)
        mix_p, *p_new = _mixer_call(cfg, proj, 0, p_init, tp // CHUNK, prm)
        mix_s, *s_new = _mixer_call(cfg, proj, nbp * tp, s_init, ts // CHUNK, prm)
        final = l == depth - 1
        out_rows = [nbp * tp, nbs * ts] if final else [nbp * tp + nbs * ts]
        x_parts = _ffn_call(x_parts, [mix_p, mix_s], wo_b[l], norm2_w[l], wg_b[l], wu_b[l], wd_b[l], final_norm_w,
                            out_rows, final)
        for acc, st in zip(p_states, p_new):
            acc.append(st)
        for acc, st in zip(s_states, s_new):
            acc.append(st)

    y_prompt = x_parts[0].reshape(nbp, tp, d)
    y_sample = x_parts[1].reshape(nbs, ts, d)
    return (y_prompt, y_sample, *(jnp.stack(st) for st in p_states), *(jnp.stack(st) for st in s_states))
```

```python
import functools
import math
from typing import NamedTuple

import jax
import jax.numpy as jnp
from jax import lax
from jax.experimental import pallas as pl
from jax.experimental.pallas import tpu as pltpu

F32 = jnp.float32
BF16 = jnp.bfloat16

CHUNK = 64
SLOTS = 2
PRELUDE_DELAY = 5
SLOT_DELAY = 3
NORM_EPS = 1e-6
RWKV_GN_EPS = 64e-5
MASKED_EXPONENT = -1e30
LANE = 128
HIST = 8
VMEM_LIMIT_BYTES = 56 * 1024 * 1024
PROJ_ROWS = 512
FFN_ROWS = 512
FFN_COLS = 256


class Cfg(NamedTuple):
    d_model: int
    head_dim: int
    conv_w: int
    gdn_heads: int
    ssm_heads: int
    ssm_groups: int
    ssm_state: int
    rwkv_heads: int
    lora_w: int
    lora_a: int
    lora_g: int

    @property
    def gdn_width(self):
        return self.gdn_heads * self.head_dim

    @property
    def ssm_width(self):
        return self.ssm_heads * self.head_dim

    @property
    def ssm_bc(self):
        return self.ssm_groups * self.ssm_state

    @property
    def rwkv_width(self):
        return self.rwkv_heads * self.head_dim

    @property
    def rwkv_cols(self):
        return 3 * self.rwkv_width + self.lora_w + self.lora_a + self.lora_g

    @property
    def o_gdn_qkv(self):
        return 0

    @property
    def o_gdn_z(self):
        return 3 * self.gdn_width

    @property
    def o_ssm_z(self):
        return self.o_gdn_z + self.gdn_width

    @property
    def o_ssm_xbc(self):
        return self.o_ssm_z + self.ssm_width

    @property
    def o_rwkv(self):
        return self.o_ssm_xbc + self.ssm_width + 2 * self.ssm_bc

    @property
    def o_small(self):
        return self.o_rwkv + self.rwkv_cols

    @property
    def proj_cols(self):
        return self.o_small + LANE

    @property
    def mix_width(self):
        return self.gdn_width + self.ssm_width + self.rwkv_width


def _rms(x, w):
    return x * lax.rsqrt(jnp.mean(x * x, axis=-1, keepdims=True) + NORM_EPS) * w


def _softplus(x):
    return jnp.maximum(x, 0.0) + jnp.log1p(jnp.exp(-jnp.abs(x)))


def _silu(x):
    return x * jax.nn.sigmoid(x)


def _mm(a, b):
    return jnp.dot(a.astype(BF16), b.astype(BF16), preferred_element_type=F32)


def _mm_nt(a, b):
    return lax.dot_general(a.astype(BF16), b.astype(BF16), (((1,), (1,)), ((), ())),
                           preferred_element_type=F32)


def _split_mm(x, m):
    hi = x.astype(BF16)
    lo = (x - hi.astype(F32)).astype(BF16)
    return jnp.dot(hi, m, preferred_element_type=F32) + jnp.dot(lo, m, preferred_element_type=F32)


def _cumsum_rows(tri, x):
    hi = x.astype(BF16)
    lo = (x - hi.astype(F32)).astype(BF16)
    return jnp.dot(tri, hi, preferred_element_type=F32) + jnp.dot(tri, lo, preferred_element_type=F32)


def _inv_one_minus_steps(n, eye_f):
    c = n.shape[0]
    t = eye_f + n
    p = _mm(n, n)
    yield
    for _ in range(int(math.log2(c)) - 2):
        step = _mm(t, p)
        p_next = _mm(p, p)
        yield
        t = t + step
        p = p_next
    step = _mm(t, p)
    yield
    return t + step


def _run_interleaved(tasks):
    tasks = list(tasks)
    while tasks:
        alive = []
        for task in tasks:
            try:
                spawned = next(task)
            except StopIteration:
                continue
            alive.append(task)
            if spawned:
                alive.extend(spawned)
        tasks = alive


class _State:
    def __init__(self, ref, index):
        self.ref, self.index, self.version = ref, index, 0

    def read(self, slot):
        assert self.version == slot, "chunk slot reads a state the previous slot has not written yet"
        return self.ref[self.index]

    def write(self, slot, value):
        assert self.version == slot
        self.ref[self.index] = value
        self.version += 1


def _conv_chunk(buf_ref, x, w_ref, conv_w):
    c = x.shape[0]
    buf_ref[HIST:HIST + c, :] = x
    y = x * w_ref[conv_w - 1:conv_w, :]
    for j in range(conv_w - 1):
        lo = HIST - (conv_w - 1) + j
        y = y + buf_ref[lo:lo + c, :] * w_ref[j:j + 1, :]
    tail = buf_ref[HIST + c - (conv_w - 1):HIST + c, :]
    buf_ref[HIST - (conv_w - 1):HIST, :] = tail
    return y


def _per_set(fn, x, nset):
    rows = x.shape[0] // nset
    return jnp.concatenate([fn(s, x[s * rows:(s + 1) * rows]) for s in range(nset)], axis=0)


def _tile_bounds(parts, rows):
    bounds, lo = [], 0
    for p in parts:
        assert p.shape[0] % rows == 0
        bounds.append((lo, lo + p.shape[0] // rows))
        lo = bounds[-1][1]
    return bounds


def _part_specs(parts, rows):
    return [pl.BlockSpec((rows, p.shape[1]), lambda i, lo=lo, hi=hi: (jnp.clip(i - lo, 0, hi - lo - 1), 0))
            for p, (lo, hi) in zip(parts, _tile_bounds(parts, rows))]


def _read_part(refs, bounds):
    i = pl.program_id(0)
    x = refs[-1][...]
    for ref, (_, hi) in reversed(list(zip(refs[:-1], bounds[:-1]))):
        x = jnp.where(i < hi, ref[...], x)
    return x


def _write_part(refs, bounds, value):
    i = pl.program_id(0)
    if len(refs) == 1:
        refs[0][...] = value
        return
    for ref, (lo, hi) in zip(refs, bounds):
        @pl.when(jnp.logical_and(i >= lo, i < hi))
        def _(ref=ref):
            ref[...] = value


def _proj_body(*refs, bounds, segments, small_cols):
    x_refs, (nw_ref, w_ref, o_ref, w_scr) = refs[:len(bounds)], refs[len(bounds):]

    @pl.when(pl.program_id(0) == 0)
    def _regroup():
        dst = 0
        for src, width in segments:
            w_scr[:, dst:dst + width] = w_ref[:, src:src + width].astype(BF16)
            dst += width
        small = jnp.concatenate([w_ref[:, src:src + width] for src, width in small_cols], axis=1)
        small = jnp.concatenate([small, jnp.zeros((small.shape[0], LANE - small.shape[1]), F32)], axis=1)
        w_scr[:, dst:dst + LANE] = small.astype(BF16)

    h = _rms(_read_part(x_refs, bounds), nw_ref[...])
    o_ref[...] = jnp.dot(h.astype(BF16), w_scr[...], preferred_element_type=F32)


def _proj_call(cfg, x_parts, norm_w, w_in, layer):
    d = x_parts[0].shape[1]
    n = sum(p.shape[0] for p in x_parts)
    rows = PROJ_ROWS
    gw, sw = cfg.gdn_width, cfg.ssm_width
    gdn_cols = 4 * gw + 2 * cfg.gdn_heads
    ssm_cols = 2 * sw + 2 * cfg.ssm_bc + cfg.ssm_heads
    assert w_in.shape[2] == gdn_cols + ssm_cols + cfg.rwkv_cols
    segments = ((0, 4 * gw), (gdn_cols, 2 * sw + 2 * cfg.ssm_bc), (gdn_cols + ssm_cols, cfg.rwkv_cols))
    small_cols = ((4 * gw, 2 * cfg.gdn_heads), (gdn_cols + 2 * sw + 2 * cfg.ssm_bc, cfg.ssm_heads))
    cols = cfg.proj_cols
    return pl.pallas_call(
        functools.partial(_proj_body, bounds=_tile_bounds(x_parts, rows), segments=segments, small_cols=small_cols),
        grid=(n // rows,),
        in_specs=_part_specs(x_parts, rows) + [
            pl.BlockSpec((1, d), lambda i: (0, 0)),
            pl.BlockSpec((None,) + w_in.shape[1:], lambda i: (layer, 0, 0), pipeline_mode=pl.Buffered(1)),
        ],
        out_specs=pl.BlockSpec((rows, cols), lambda i: (i, 0)),
        out_shape=jax.ShapeDtypeStruct((n, cols), F32),
        scratch_shapes=[pltpu.VMEM((d, cols), BF16)],
        compiler_params=pltpu.CompilerParams(
            dimension_semantics=("arbitrary",), vmem_limit_bytes=VMEM_LIMIT_BYTES),
        name="norm_proj",
    )(*x_parts, norm_w.reshape(1, d), w_in)


def _ffn_body(*refs, in_bounds, mix_bounds, out_bounds, final):
    nx, nm = len(in_bounds), len(mix_bounds)
    x_refs, mix_refs, refs = refs[:nx], refs[nx:nx + nm], refs[nx + nm:]
    (wo_ref, n2_ref, wg_ref, wu_ref, wd_ref, fn_ref), o_refs = refs[:6], refs[6:]
    mix = _read_part(mix_refs, mix_bounds)
    x = _read_part(x_refs, in_bounds) + jnp.dot(mix, wo_ref[...], preferred_element_type=F32)
    h2 = _rms(x, n2_ref[...]).astype(BF16)
    hidden = wg_ref.shape[1]
    acc = x
    for c0 in range(0, hidden, FFN_COLS):
        g = jnp.dot(h2, wg_ref[:, c0:c0 + FFN_COLS], preferred_element_type=F32)
        u = jnp.dot(h2, wu_ref[:, c0:c0 + FFN_COLS], preferred_element_type=F32)
        ff = (_silu(g) * u).astype(BF16)
        acc = acc + jnp.dot(ff, wd_ref[c0:c0 + FFN_COLS, :], preferred_element_type=F32)
    if final:
        acc = _rms(acc, fn_ref[...])
    _write_part(o_refs, out_bounds, acc)


def _ffn_call(x_parts, mix_parts, wo, n2, wg, wu, wd, fn, layer, out_rows, final):
    d = x_parts[0].shape[1]
    n = sum(p.shape[0] for p in x_parts)
    hidden = wg.shape[2]
    rows = FFN_ROWS
    assert n % rows == 0 and hidden % FFN_COLS == 0 and sum(out_rows) == n
    const = lambda i: (0, 0)

    def layer_spec(w):
        return pl.BlockSpec((None,) + w.shape[1:], lambda i: (layer, 0, 0))

    out_shape = [jax.ShapeDtypeStruct((r, d), F32) for r in out_rows]
    return pl.pallas_call(
        functools.partial(_ffn_body, in_bounds=_tile_bounds(x_parts, rows),
                          mix_bounds=_tile_bounds(mix_parts, rows),
                          out_bounds=_tile_bounds(out_shape, rows), final=final),
        grid=(n // rows,),
        in_specs=_part_specs(x_parts, rows) + _part_specs(mix_parts, rows) + [
            layer_spec(wo),
            pl.BlockSpec((1, d), const),
            layer_spec(wg),
            layer_spec(wu),
            layer_spec(wd),
            pl.BlockSpec((1, d), const),
        ],
        out_specs=_part_specs(out_shape, rows),
        out_shape=out_shape,
        compiler_params=pltpu.CompilerParams(
            dimension_semantics=("arbitrary",), vmem_limit_bytes=VMEM_LIMIT_BYTES),
        name="outproj_ffn",
    )(*x_parts, *mix_parts, wo, n2.reshape(1, d), wg, wu, wd, fn.reshape(1, d))


def _gdn_head(cfg, h, slot, delay, rs, pre, state, norm_w, masks, outs):
    eye_f, causal_bias, offdiag_f, _ = masks
    hd, gw = cfg.head_dim, cfg.gdn_width
    c = CHUNK
    g = rs.start // c
    for _ in range(delay):
        yield
    sl = slice(h * hd, (h + 1) * hd)
    q = pre["q"][rs, sl]
    k = pre["k"][rs, sl]
    kt = pre["kt"][g * gw + h * hd:g * gw + (h + 1) * hd, :]
    b = pre["beta"][rs, cfg.gdn_heads + h:cfg.gdn_heads + h + 1]
    kb = k * b
    vb = pre["v"][rs, sl] * b
    kq = _mm(jnp.concatenate([kb, q], axis=0), kt)
    yield
    gcol = pre["cum"][rs, h:h + 1]
    grow = pre["cum_t"][g * LANE + h:g * LANE + h + 1, :]
    dec = jnp.exp(gcol - grow + causal_bias)
    lower = kq[:c] * (dec * offdiag_f)
    attn = kq[c:] * dec
    t = yield from _inv_one_minus_steps(-lower, eye_f)
    eg = pre["eg"][rs, h:h + 1]
    u = _mm(t, vb)
    w = _mm(t, kb * eg)
    yield
    s = state.read(slot)
    wq_s = _mm(jnp.concatenate([w, q * eg], axis=0), s)
    yield
    g_last = gcol[c - 1:c, :]
    v_new = u - wq_s[:c]
    o = wq_s[c:] + _mm(attn, v_new)
    state.write(slot, s * jnp.exp(g_last) + _mm(kt * jnp.exp(g_last - grow), v_new))
    yield
    outs[h] = _rms(o, norm_w) * pre["gzs"][rs, sl]


def _ssd_head(cfg, h, slot, delay, rs, pre, state, d_row, masks, cb_cache, ys):
    _, causal_bias, _, _ = masks
    hd, ns, width = cfg.head_dim, cfg.ssm_state, cfg.ssm_width
    gh = cfg.gdn_heads
    c = CHUNK
    g = rs.start // c
    for _ in range(delay):
        yield
    grp = h // (cfg.ssm_heads // cfg.ssm_groups)
    bm = pre["xbc"][rs, width + grp * ns:width + (grp + 1) * ns]
    cm = pre["xbc"][rs, width + cfg.ssm_bc + grp * ns:width + cfg.ssm_bc + (grp + 1) * ns]
    if grp not in cb_cache:
        cb_cache[grp] = _mm_nt(cm, bm)
    lane = 2 * gh + h
    acol = pre["cum"][rs, lane:lane + 1]
    arow = pre["cum_t"][g * LANE + lane:g * LANE + lane + 1, :]
    a_last = acol[c - 1:c, :]
    x = pre["xbc"][rs, h * hd:(h + 1) * hd]
    xdt = x * pre["sp"][rs, lane:lane + 1]
    s = state.read(slot)
    y_off = _mm_nt(cm, s)
    dt_row = pre["sp_t"][g * LANE + lane:g * LANE + lane + 1, :]
    xt_dec = pre["xt"][g * width + h * hd:g * width + (h + 1) * hd, :] * (dt_row * jnp.exp(a_last - arow))
    state.write(slot, s * jnp.exp(a_last) + _mm(xt_dec, bm))
    yield
    lmat = jnp.exp(acol - arow + causal_bias)
    y_diag = _mm(cb_cache[grp] * lmat, xdt)
    yield
    ys[h] = y_diag + y_off * jnp.exp(acol) + d_row[:, h:h + 1] * x


def _rwkv_head(cfg, h, slot, delay, rs, pre, state, ln_w, ln_b, masks, incl2, outs):
    eye_f, _, _, strict2 = masks
    hd, rw = cfg.head_dim, cfg.rwkv_width
    c = CHUNK
    g = rs.start // c
    for _ in range(delay):
        yield
    sl = slice(h * hd, (h + 1) * hd)
    at = pre["at"][rs, sl]
    bt = pre["bt"][rs, sl]
    kt = pre["kt_r"][rs, sl]
    vh = pre["v_r"][rs, sl]
    pch = pre["pm"][rs.stop - 1:rs.stop, sl]
    ar = jnp.concatenate([at, pre["rt"][rs, sl]], axis=0)
    bk = jnp.concatenate([bt, kt], axis=0)
    cross = _mm_nt(ar, bk)
    vk = _mm(pre["vt_r"][g * rw + h * hd:g * rw + (h + 1) * hd, :], kt * pch)
    yield
    a_abk = jnp.where(strict2, cross[:c], 0.0)
    a_ab = a_abk[:, :c]
    aakv = _mm(a_abk[:, c:], vh)
    t = yield from _inv_one_minus_steps(a_ab, eye_f)
    s = state.read(slot)
    ar_s = _mm_nt(ar, s)
    yield
    u = _mm(t, ar_s[:c] + aakv)
    yield
    uv = jnp.concatenate([u, vh], axis=0)
    y = ar_s[c:] + _mm(jnp.where(incl2, cross[c:], 0.0), uv)
    ut = _mm_nt(eye_f, u)
    yield
    state.write(slot, s * pch + _mm(ut, bt * pch) + vk)
    yield
    mean = jnp.mean(y, axis=-1, keepdims=True)
    yc = y - mean
    var = jnp.mean(yc * yc, axis=-1, keepdims=True)
    yn = yc * lax.rsqrt(var + RWKV_GN_EPS) * ln_w[:, sl] + ln_b[:, sl]
    outs[h] = (yn + pre["bonus"][rs, sl] * vh) * pre["gate"][rs, sl]


def _prelude(cfg, nset, p_ref, bufs, prm, out):
    (gconv_w_ref, alog_ref, dtb_ref, sconv_w_ref, sconv_b_ref, mu_ref, w0_ref, wup_ref, a0_ref, aup_ref,
     gup_ref, kk_ref, ka_ref, rk_ref, gseg_ref, rseg_ref) = prm
    gbuf, sbuf, rbuf = bufs
    c = CHUNK
    rows = p_ref.shape[0]
    nslot = rows // c
    cw = cfg.conv_w
    gw, sw, rw = cfg.gdn_width, cfg.ssm_width, cfg.rwkv_width
    for _ in range(PRELUDE_DELAY):
        yield

    rr = lax.broadcasted_iota(jnp.int32, (rows, rows), 0)
    cc = lax.broadcasted_iota(jnp.int32, (rows, rows), 1)
    same_chunk = functools.reduce(jnp.logical_and, [(rr >= m * c) == (cc >= m * c) for m in range(1, nslot)],
                                  rr >= 0)
    tri = jnp.logical_and(same_chunk, rr >= cc).astype(BF16)

    small = p_ref[:, cfg.o_small:cfg.o_small + LANE]
    sp = _softplus(small + dtb_ref[...])
    cum = _cumsum_rows(tri, sp * (-jnp.exp(alog_ref[...])))

    def shift(s, pr):
        n = pr.shape[0]
        rbuf[s, HIST:HIST + n, :] = pr
        prev = rbuf[s, HIST - 1:HIST - 1 + n, :]
        rbuf[s, HIST - 1:HIST, :] = pr[n - 1:n, :]
        return pr + (prev - pr) * mu_ref[...]

    xm = _per_set(shift, p_ref[:, cfg.o_rwkv:cfg.o_rwkv + cfg.rwkv_cols], nset)
    c0 = 3 * rw
    c1 = c0 + cfg.lora_w
    c2 = c1 + cfg.lora_a
    r = xm[:, :rw]
    k = xm[:, rw:2 * rw]
    v = xm[:, 2 * rw:c0]
    lora_w = _mm(jnp.tanh(xm[:, c0:c1]), wup_ref[...])
    lora_a = _mm(xm[:, c1:c2], aup_ref[...])
    gate = _mm(jax.nn.sigmoid(xm[:, c2:]), gup_ref[...])

    qkv = _per_set(lambda s, x: _conv_chunk(gbuf.at[s], x, gconv_w_ref, cw),
                   p_ref[:, cfg.o_gdn_qkv:cfg.o_gdn_qkv + 3 * gw], nset)
    qkv = _silu(qkv)
    q_raw, k_raw = qkv[:, :gw], qkv[:, gw:2 * gw]
    ssq = _split_mm(jnp.concatenate([q_raw * q_raw, k_raw * k_raw], axis=0), gseg_ref[...])
    yield
    w_log = -_softplus(-(w0_ref[...] + lora_w)) - 0.5
    logw = -jnp.exp(w_log)
    rcum = _cumsum_rows(tri, logw)
    iclr = jax.nn.sigmoid(a0_ref[...] + lora_a)
    k2 = k * (1.0 + (iclr - 1.0) * ka_ref[...])
    kk_raw = k * kk_ref[...]
    kk_ssq = _split_mm(kk_raw * kk_raw, rseg_ref[...])
    bonus = _split_mm(r * k2 * rk_ref[...], rseg_ref[...])
    yield
    xbc = _per_set(lambda s, x: _conv_chunk(sbuf.at[s], x, sconv_w_ref, cw),
                   p_ref[:, cfg.o_ssm_xbc:cfg.o_ssm_xbc + sw + 2 * cfg.ssm_bc], nset)
    xbc = _silu(xbc + sconv_b_ref[...])
    k_all = k_raw * lax.rsqrt(ssq[rows:] + NORM_EPS)
    pm = jnp.exp(rcum)
    pinv = jnp.exp(-rcum)
    kkn = kk_raw * lax.rsqrt(kk_ssq + NORM_EPS)
    out.update(
        q=q_raw * lax.rsqrt(ssq[:rows] + NORM_EPS) * (cfg.head_dim ** -0.5), k=k_all, v=qkv[:, 2 * gw:],
        gzs=_silu(p_ref[:, cfg.o_gdn_z:cfg.o_gdn_z + gw]),
        cum=cum, sp=sp, beta=jax.nn.sigmoid(small), eg=jnp.exp(cum),
        xbc=xbc, szs=_silu(p_ref[:, cfg.o_ssm_z:cfg.o_ssm_z + sw]),
        at=-kkn * jnp.exp(rcum - logw), bt=kkn * iclr * pinv, rt=r * pm, kt_r=k2 * pinv, v_r=v, gate=gate,
        bonus=bonus, pm=pm,
        kt=jnp.concatenate([k_all[g * c:(g + 1) * c].T for g in range(nslot)], axis=0),
        cum_t=jnp.concatenate([cum[g * c:(g + 1) * c].T for g in range(nslot)], axis=0),
        sp_t=jnp.concatenate([sp[g * c:(g + 1) * c].T for g in range(nslot)], axis=0),
        xt=jnp.concatenate([xbc[g * c:(g + 1) * c, :sw].T for g in range(nslot)], axis=0),
        vt_r=jnp.concatenate([v[g * c:(g + 1) * c].T for g in range(nslot)], axis=0),
    )


def _prelude_shapes(cfg, rows):
    nslot = rows // CHUNK
    gw, sw, rw = cfg.gdn_width, cfg.ssm_width, cfg.rwkv_width
    shapes = {name: (rows, gw) for name in ("q", "k", "v", "gzs")}
    shapes.update({name: (rows, LANE) for name in ("cum", "sp", "beta", "eg")})
    shapes.update(xbc=(rows, sw + 2 * cfg.ssm_bc), szs=(rows, sw))
    shapes.update({name: (rows, rw) for name in ("at", "bt", "rt", "kt_r", "v_r", "gate", "bonus", "pm")})
    shapes.update(kt=(nslot * gw, CHUNK), cum_t=(nslot * LANE, CHUNK), sp_t=(nslot * LANE, CHUNK),
                  xt=(nslot * sw, CHUNK), vt_r=(nslot * rw, CHUNK))
    return shapes


def _mixer_body(p_ref, gdn0_ref, gdnc0_ref, ssm0_ref, ssmc0_ref, rwkv0_ref, shift0_ref,
                gconv_w_ref, alog_ref, dtb_ref, gnorm_ref,
                sconv_w_ref, sconv_b_ref, sd_ref, snorm_ref,
                mu_ref, w0_ref, wup_ref, a0_ref, aup_ref, gup_ref, kk_ref, ka_ref, rk_ref, lnw_ref, lnb_ref,
                gseg_ref, rseg_ref,
                mix_ref, gdn_out_ref, gdnc_out_ref, ssm_out_ref, ssmc_out_ref, rwkv_out_ref, shift_out_ref,
                gdn_s, ssm_s, rwkv_s, gbuf, sbuf, rbuf, *pre_refs, cfg, nset, steps, nblocks, names):
    pre = dict(zip(names, pre_refs))
    c = CHUNK
    rows = p_ref.shape[0]
    nslot = rows // c
    per_set = nslot // nset
    slot_plan = [(s, j) for s in range(nset) for j in range(per_set)]
    cw = cfg.conv_w
    gh, sh = cfg.gdn_heads, cfg.ssm_heads
    t = pl.program_id(0)
    p_block = jnp.minimum(t, nblocks - 1)
    c_block = jnp.maximum(t - 1, 0)

    @pl.when(t == 0)
    def _clear():
        for ref in (gdn_s, ssm_s, rwkv_s) + tuple(pre_refs):
            ref[...] = jnp.zeros(ref.shape, ref.dtype)

    @pl.when(p_block % steps == 0)
    def _load_history():
        gbuf[:, HIST - (cw - 1):HIST, :] = gdnc0_ref[...]
        sbuf[:, HIST - (cw - 1):HIST, :] = ssmc0_ref[...]
        rbuf[:, HIST - 1:HIST, :] = shift0_ref[...]

    @pl.when(jnp.logical_and(t >= 1, c_block % steps == 0))
    def _load_state():
        gdn_s[...] = gdn0_ref[...]
        ssm_s[...] = ssm0_ref[...]
        rwkv_s[...] = rwkv0_ref[...]

    ri = lax.broadcasted_iota(jnp.int32, (c, c), 0)
    ci = lax.broadcasted_iota(jnp.int32, (c, c), 1)
    eye_f = (ri == ci).astype(F32)
    offdiag_f = 1.0 - eye_f
    causal_bias = jnp.where(ri >= ci, 0.0, MASKED_EXPONENT)
    ri2 = lax.broadcasted_iota(jnp.int32, (c, 2 * c), 0)
    ci2 = lax.broadcasted_iota(jnp.int32, (c, 2 * c), 1)
    ci2 = jnp.where(ci2 >= c, ci2 - c, ci2)
    strict2 = ri2 > ci2
    incl2 = ri2 >= ci2
    masks = (eye_f, causal_bias, offdiag_f, strict2)

    gnorm, sd, ln_w, ln_b = gnorm_ref[...], sd_ref[...], lnw_ref[...], lnb_ref[...]
    gdn_states = [[_State(gdn_s, (s, h)) for h in range(gh)] for s in range(nset)]
    ssm_states = [[_State(ssm_s, (s, h)) for h in range(sh)] for s in range(nset)]
    rwkv_states = [[_State(rwkv_s, (s, h)) for h in range(cfg.rwkv_heads)] for s in range(nset)]
    gdn_o = [[None] * gh for _ in slot_plan]
    ssd_y = [[None] * sh for _ in slot_plan]
    rwkv_o = [[None] * cfg.rwkv_heads for _ in slot_plan]

    new_pre = {}
    prm = (gconv_w_ref, alog_ref, dtb_ref, sconv_w_ref, sconv_b_ref, mu_ref, w0_ref, wup_ref, a0_ref, aup_ref,
           gup_ref, kk_ref, ka_ref, rk_ref, gseg_ref, rseg_ref)
    tasks = []
    for g, (s, j) in enumerate(slot_plan):
        rs = slice(g * c, (g + 1) * c)
        delay = j * SLOT_DELAY
        tasks += [_gdn_head(cfg, h, j, delay, rs, pre, gdn_states[s][h], gnorm, masks, gdn_o[g]) for h in range(gh)]
        tasks += [_rwkv_head(cfg, h, j, delay, rs, pre, rwkv_states[s][h], ln_w, ln_b, masks, incl2, rwkv_o[g])
                  for h in range(cfg.rwkv_heads)]
        cb_cache = {}
        tasks += [_ssd_head(cfg, h, j, delay + h, rs, pre, ssm_states[s][h], sd, masks, cb_cache, ssd_y[g])
                  for h in range(sh)]
    tasks.append(_prelude(cfg, nset, p_ref, (gbuf, sbuf, rbuf), prm, new_pre))
    _run_interleaved(tasks)

    ng = cfg.ssm_groups
    gw = cfg.ssm_width // ng
    snorm = snorm_ref[...]
    for g in range(nslot):
        rs = slice(g * c, (g + 1) * c)
        ssd_o = []
        for grp in range(ng):
            yg = jnp.concatenate(ssd_y[g][grp * (sh // ng):(grp + 1) * (sh // ng)], axis=-1)
            yg = yg * pre["szs"][rs, grp * gw:(grp + 1) * gw]
            yg = yg * lax.rsqrt(jnp.mean(yg * yg, axis=-1, keepdims=True) + NORM_EPS)
            ssd_o.append(yg * snorm[:, grp * gw:(grp + 1) * gw])
        mix_ref[rs, :] = jnp.concatenate(gdn_o[g] + ssd_o + rwkv_o[g], axis=-1).astype(mix_ref.dtype)

    gdn_out_ref[...] = gdn_s[...]
    ssm_out_ref[...] = ssm_s[...]
    rwkv_out_ref[...] = rwkv_s[...]
    gdnc_out_ref[...] = gbuf[:, HIST - (cw - 1):HIST, :]
    ssmc_out_ref[...] = sbuf[:, HIST - (cw - 1):HIST, :]
    shift_out_ref[...] = rbuf[:, HIST - 1:HIST, :]
    for name in names:
        pre[name][...] = new_pre[name]


def _mixer_call(cfg, proj, row0, init, layer, chunks_per_seq, prm):
    nseq = init[0].shape[1]
    if chunks_per_seq % SLOTS == 0:
        nset, per_set = 1, SLOTS
    elif chunks_per_seq == 1 and nseq % SLOTS == 0:
        nset, per_set = SLOTS, 1
    else:
        nset, per_set = 1, 1
    rows = nset * per_set * CHUNK
    steps = chunks_per_seq // per_set
    assert row0 % rows == 0 and nseq % nset == 0
    blk0 = row0 // rows
    nblocks = (nseq // nset) * steps

    def p_block(t):
        return jnp.minimum(t, nblocks - 1)

    def c_block(t):
        return jnp.maximum(t - 1, 0)

    def in_spec(st, block_of):
        nd = st.ndim
        return pl.BlockSpec((None, nset) + st.shape[2:], lambda t: (layer, block_of(t) // steps) + (0,) * (nd - 2))

    def out_spec(st, block_of):
        nd = st.ndim - 1
        return pl.BlockSpec((nset,) + st.shape[2:], lambda t: (block_of(t) // steps,) + (0,) * (nd - 1))

    gdn0, gdnc0, ssm0, ssmc0, rwkv0, shift0 = init
    side = (c_block, p_block, c_block, p_block, c_block, p_block)
    in_specs = ([pl.BlockSpec((rows, proj.shape[1]), lambda t: (blk0 + p_block(t), 0))]
                + [in_spec(st, blk) for st, blk in zip(init, side)]
                + [pl.BlockSpec(p.shape, lambda t: (0, 0)) for p in prm])
    out_specs = ([pl.BlockSpec((rows, cfg.mix_width), lambda t: (c_block(t), 0))]
                 + [out_spec(st, blk) for st, blk in zip(init, side)])
    out_shape = [jax.ShapeDtypeStruct((nseq * chunks_per_seq * CHUNK, cfg.mix_width), BF16)] + [
        jax.ShapeDtypeStruct(st.shape[1:], F32) for st in init]
    shapes = _prelude_shapes(cfg, rows)
    names = tuple(shapes)
    scratch = [
        pltpu.VMEM((nset,) + gdn0.shape[2:], F32),
        pltpu.VMEM((nset,) + ssm0.shape[2:], F32),
        pltpu.VMEM((nset,) + rwkv0.shape[2:], F32),
        pltpu.VMEM((nset, HIST + per_set * CHUNK, gdnc0.shape[3]), F32),
        pltpu.VMEM((nset, HIST + per_set * CHUNK, ssmc0.shape[3]), F32),
        pltpu.VMEM((nset, HIST + per_set * CHUNK, shift0.shape[3]), F32),
    ] + [pltpu.VMEM(shapes[name], F32) for name in names]
    return pl.pallas_call(
        functools.partial(_mixer_body, cfg=cfg, nset=nset, steps=steps, nblocks=nblocks, names=names),
        grid=(nblocks + 1,),
        in_specs=in_specs,
        out_specs=out_specs,
        out_shape=out_shape,
        scratch_shapes=scratch,
        compiler_params=pltpu.CompilerParams(
            dimension_semantics=("arbitrary",), vmem_limit_bytes=VMEM_LIMIT_BYTES),
        name="mixers",
    )(proj, *init, *prm)


def _pad_lanes(v, width=LANE):
    v = v.reshape(1, -1)
    return jnp.pad(v, ((0, 0), (0, width - v.shape[1])))


def kernel(x_prompt, x_sample, state_gdn, state_gdn_conv, state_ssm, state_ssm_conv, state_rwkv, state_rwkv_shift, norm1_w, w_in, gdn_conv_w, gdn_A_log, gdn_dt_bias, gdn_norm_w, ssm_conv_w, ssm_conv_b, ssm_A_log, ssm_dt_bias, ssm_D, ssm_norm_w, rwkv_mu, rwkv_w0, rwkv_w_up, rwkv_a0, rwkv_a_up, rwkv_g_up, rwkv_k_k, rwkv_k_a, rwkv_r_k, rwkv_ln_w, rwkv_ln_b, w_out, norm2_w, ffn_w_gate, ffn_w_up, ffn_w_down, final_norm_w):
    depth = w_in.shape[0]
    nbp, tp, d = x_prompt.shape
    nbs, ts, _ = x_sample.shape
    hd = state_gdn.shape[-1]
    ssm_width = state_ssm.shape[2] * hd
    cfg = Cfg(
        d_model=d, head_dim=hd, conv_w=gdn_conv_w.shape[1],
        gdn_heads=state_gdn.shape[2], ssm_heads=state_ssm.shape[2],
        ssm_groups=(ssm_conv_w.shape[2] - ssm_width) // (2 * state_ssm.shape[-1]),
        ssm_state=state_ssm.shape[-1], rwkv_heads=state_rwkv.shape[2],
        lora_w=rwkv_w_up.shape[1], lora_a=rwkv_a_up.shape[1], lora_g=rwkv_g_up.shape[1])
    assert tp % CHUNK == 0 and ts % CHUNK == 0 and tp >= cfg.conv_w and ts >= cfg.conv_w
    assert 2 * cfg.gdn_heads + cfg.ssm_heads <= LANE
    gw, sw = cfg.gdn_width, cfg.ssm_width
    gdn_cols = 4 * gw + 2 * cfg.gdn_heads
    ssm_cols = 2 * sw + 2 * cfg.ssm_bc + cfg.ssm_heads
    assert w_in.shape[2] == gdn_cols + ssm_cols + cfg.rwkv_cols

    x_parts = [x_prompt.reshape(nbp * tp, d), x_sample.reshape(nbs * ts, d)]

    def small_row(l, gdn_first, gdn_second, ssm_part):
        return _pad_lanes(jnp.concatenate([gdn_first[l], gdn_second, ssm_part[l]]))

    zeros_g = jnp.zeros((cfg.gdn_heads,), F32)
    head_of = jnp.arange(gw) // hd
    gdn_seg = (head_of[:, None] == head_of[None, :]).astype(BF16)
    head_of = jnp.arange(cfg.rwkv_width) // hd
    rwkv_seg = (head_of[:, None] == head_of[None, :]).astype(BF16)
    wo_b, wg_b, wu_b, wd_b = (w.astype(BF16) for w in (w_out, ffn_w_gate, ffn_w_up, ffn_w_down))
    sample_states = (state_gdn, state_gdn_conv, state_ssm, state_ssm_conv, state_rwkv, state_rwkv_shift)

    p_init = tuple(jnp.zeros((1, nbp) + st.shape[2:], F32) for st in sample_states)
    s_init = tuple(st.astype(F32) for st in sample_states)
    p_states = [[] for _ in sample_states]
    s_states = [[] for _ in sample_states]
    for l in range(depth):
        proj = _proj_call(cfg, x_parts, norm1_w[l], w_in, l)
        prm = (
            gdn_conv_w[l], small_row(l, gdn_A_log, zeros_g, ssm_A_log), small_row(l, gdn_dt_bias, zeros_g, ssm_dt_bias),
            gdn_norm_w[l].reshape(1, -1),
            ssm_conv_w[l], ssm_conv_b[l].reshape(1, -1), ssm_D[l].reshape(1, -1), ssm_norm_w[l].reshape(1, -1),
            rwkv_mu[l].reshape(1, -1), rwkv_w0[l].reshape(1, -1), rwkv_w_up[l], rwkv_a0[l].reshape(1, -1),
            rwkv_a_up[l], rwkv_g_up[l], rwkv_k_k[l].reshape(1, -1), rwkv_k_a[l].reshape(1, -1),
            rwkv_r_k[l].reshape(1, -1), rwkv_ln_w[l].reshape(1, -1), rwkv_ln_b[l].reshape(1, -1),
            gdn_seg, rwkv_seg,
        )
        mix_p, *p_new = _mixer_call(cfg, proj, 0, p_init, 0, tp // CHUNK, prm)
        mix_s, *s_new = _mixer_call(cfg, proj, nbp * tp, s_init, l, ts // CHUNK, prm)
        final = l == depth - 1
        out_rows = [nbp * tp, nbs * ts] if final else [nbp * tp + nbs * ts]
        x_parts = _ffn_call(x_parts, [mix_p, mix_s], wo_b, norm2_w[l], wg_b, wu_b, wd_b, final_norm_w, l,
                            out_rows, final)
        for acc, st in zip(p_states, p_new):
            acc.append(st)
        for acc, st in zip(s_states, s_new):
            acc.append(st)

    y_prompt = x_parts[0].reshape(nbp, tp, d)
    y_sample = x_parts[1].reshape(nbs, ts, d)
    return (y_prompt, y_sample, *(jnp.stack(st) for st in p_states), *(jnp.stack(st) for st in s_states))
```

```python
import functools
import math
from typing import NamedTuple

import jax
import jax.numpy as jnp
from jax import lax
from jax.experimental import pallas as pl
from jax.experimental.pallas import tpu as pltpu

F32 = jnp.float32
BF16 = jnp.bfloat16

CHUNK = 64
SLOTS = 2
PRELUDE_DELAY = 5
SLOT_DELAY = 3
NORM_EPS = 1e-6
RWKV_GN_EPS = 64e-5
MASKED_EXPONENT = -1e30
LANE = 128
HIST = 8
VMEM_LIMIT_BYTES = 56 * 1024 * 1024
PROJ_ROWS = 512
FFN_ROWS = 512
FFN_COLS = 256


class Cfg(NamedTuple):
    d_model: int
    head_dim: int
    conv_w: int
    gdn_heads: int
    ssm_heads: int
    ssm_groups: int
    ssm_state: int
    rwkv_heads: int
    lora_w: int
    lora_a: int
    lora_g: int

    @property
    def gdn_width(self):
        return self.gdn_heads * self.head_dim

    @property
    def ssm_width(self):
        return self.ssm_heads * self.head_dim

    @property
    def ssm_bc(self):
        return self.ssm_groups * self.ssm_state

    @property
    def rwkv_width(self):
        return self.rwkv_heads * self.head_dim

    @property
    def rwkv_cols(self):
        return 3 * self.rwkv_width + self.lora_w + self.lora_a + self.lora_g

    @property
    def o_gdn_qkv(self):
        return 0

    @property
    def o_gdn_z(self):
        return 3 * self.gdn_width

    @property
    def o_ssm_z(self):
        return self.o_gdn_z + self.gdn_width

    @property
    def o_ssm_xbc(self):
        return self.o_ssm_z + self.ssm_width

    @property
    def o_rwkv(self):
        return self.o_ssm_xbc + self.ssm_width + 2 * self.ssm_bc

    @property
    def o_small(self):
        return self.o_rwkv + self.rwkv_cols

    @property
    def proj_cols(self):
        return self.o_small + LANE

    @property
    def mix_width(self):
        return self.gdn_width + self.ssm_width + self.rwkv_width


def _rms(x, w):
    return x * lax.rsqrt(jnp.mean(x * x, axis=-1, keepdims=True) + NORM_EPS) * w


def _softplus(x):
    return jnp.maximum(x, 0.0) + jnp.log1p(jnp.exp(-jnp.abs(x)))


def _silu(x):
    return x * jax.nn.sigmoid(x)


def _mm(a, b):
    return jnp.dot(a.astype(BF16), b.astype(BF16), preferred_element_type=F32)


def _mm_nt(a, b):
    return lax.dot_general(a.astype(BF16), b.astype(BF16), (((1,), (1,)), ((), ())),
                           preferred_element_type=F32)


def _split_mm(x, m):
    hi = x.astype(BF16)
    lo = (x - hi.astype(F32)).astype(BF16)
    return jnp.dot(hi, m, preferred_element_type=F32) + jnp.dot(lo, m, preferred_element_type=F32)


def _cumsum_rows(tri, x):
    hi = x.astype(BF16)
    lo = (x - hi.astype(F32)).astype(BF16)
    return jnp.dot(tri, hi, preferred_element_type=F32) + jnp.dot(tri, lo, preferred_element_type=F32)


def _inv_one_minus_steps(n, eye_f):
    c = n.shape[0]
    t = eye_f + n
    p = _mm(n, n)
    yield
    for _ in range(int(math.log2(c)) - 2):
        step = _mm(t, p)
        p_next = _mm(p, p)
        yield
        t = t + step
        p = p_next
    step = _mm(t, p)
    yield
    return t + step


def _run_interleaved(tasks):
    tasks = list(tasks)
    while tasks:
        alive = []
        for task in tasks:
            try:
                spawned = next(task)
            except StopIteration:
                continue
            alive.append(task)
            if spawned:
                alive.extend(spawned)
        tasks = alive


class _State:
    def __init__(self, ref, index):
        self.ref, self.index, self.version = ref, index, 0

    def read(self, slot):
        assert self.version == slot, "chunk slot reads a state the previous slot has not written yet"
        return self.ref[self.index]

    def write(self, slot, value):
        assert self.version == slot
        self.ref[self.index] = value
        self.version += 1


def _conv_chunk(buf_ref, x, w_ref, conv_w):
    c = x.shape[0]
    buf_ref[HIST:HIST + c, :] = x
    y = x * w_ref[conv_w - 1:conv_w, :]
    for j in range(conv_w - 1):
        lo = HIST - (conv_w - 1) + j
        y = y + buf_ref[lo:lo + c, :] * w_ref[j:j + 1, :]
    tail = buf_ref[HIST + c - (conv_w - 1):HIST + c, :]
    buf_ref[HIST - (conv_w - 1):HIST, :] = tail
    return y


def _per_set(fn, x, nset):
    rows = x.shape[0] // nset
    return jnp.concatenate([fn(s, x[s * rows:(s + 1) * rows]) for s in range(nset)], axis=0)


def _tile_bounds(parts, rows):
    bounds, lo = [], 0
    for p in parts:
        assert p.shape[0] % rows == 0
        bounds.append((lo, lo + p.shape[0] // rows))
        lo = bounds[-1][1]
    return bounds


def _part_specs(parts, rows):
    return [pl.BlockSpec((rows, p.shape[1]), lambda i, lo=lo, hi=hi: (jnp.clip(i - lo, 0, hi - lo - 1), 0))
            for p, (lo, hi) in zip(parts, _tile_bounds(parts, rows))]


def _read_part(refs, bounds):
    i = pl.program_id(0)
    x = refs[-1][...]
    for ref, (_, hi) in reversed(list(zip(refs[:-1], bounds[:-1]))):
        x = jnp.where(i < hi, ref[...], x)
    return x


def _write_part(refs, bounds, value):
    i = pl.program_id(0)
    if len(refs) == 1:
        refs[0][...] = value
        return
    for ref, (lo, hi) in zip(refs, bounds):
        @pl.when(jnp.logical_and(i >= lo, i < hi))
        def _(ref=ref):
            ref[...] = value


def _proj_body(*refs, bounds, segments, small_cols):
    x_refs, (nw_ref, w_ref, o_ref, w_scr) = refs[:len(bounds)], refs[len(bounds):]

    @pl.when(pl.program_id(0) == 0)
    def _regroup():
        dst = 0
        for src, width in segments:
            w_scr[dst:dst + width, :] = w_ref[src:src + width, :].astype(BF16)
            dst += width
        small = jnp.concatenate([w_ref[src:src + width, :] for src, width in small_cols], axis=0)
        small = jnp.concatenate([small, jnp.zeros((LANE - small.shape[0], small.shape[1]), F32)], axis=0)
        w_scr[dst:dst + LANE, :] = small.astype(BF16)

    h = _rms(_read_part(x_refs, bounds), nw_ref[...])
    o_ref[...] = _mm_nt(h, w_scr[...])


def _proj_call(cfg, x_parts, norm_w, w_in, layer):
    d = x_parts[0].shape[1]
    n = sum(p.shape[0] for p in x_parts)
    rows = PROJ_ROWS
    gw, sw = cfg.gdn_width, cfg.ssm_width
    gdn_cols = 4 * gw + 2 * cfg.gdn_heads
    ssm_cols = 2 * sw + 2 * cfg.ssm_bc + cfg.ssm_heads
    assert w_in.shape[1] == gdn_cols + ssm_cols + cfg.rwkv_cols
    segments = ((0, 4 * gw), (gdn_cols, 2 * sw + 2 * cfg.ssm_bc), (gdn_cols + ssm_cols, cfg.rwkv_cols))
    small_cols = ((4 * gw, 2 * cfg.gdn_heads), (gdn_cols + 2 * sw + 2 * cfg.ssm_bc, cfg.ssm_heads))
    cols = cfg.proj_cols
    return pl.pallas_call(
        functools.partial(_proj_body, bounds=_tile_bounds(x_parts, rows), segments=segments, small_cols=small_cols),
        grid=(n // rows,),
        in_specs=_part_specs(x_parts, rows) + [
            pl.BlockSpec((1, d), lambda i: (0, 0)),
            pl.BlockSpec((None,) + w_in.shape[1:], lambda i: (layer, 0, 0), pipeline_mode=pl.Buffered(1)),
        ],
        out_specs=pl.BlockSpec((rows, cols), lambda i: (i, 0)),
        out_shape=jax.ShapeDtypeStruct((n, cols), F32),
        scratch_shapes=[pltpu.VMEM((cols, d), BF16)],
        compiler_params=pltpu.CompilerParams(
            dimension_semantics=("arbitrary",), vmem_limit_bytes=VMEM_LIMIT_BYTES),
        name="norm_proj",
    )(*x_parts, norm_w.reshape(1, d), w_in)


def _ffn_body(*refs, in_bounds, mix_bounds, out_bounds, final):
    nx, nm = len(in_bounds), len(mix_bounds)
    x_refs, mix_refs, refs = refs[:nx], refs[nx:nx + nm], refs[nx + nm:]
    (wo_ref, n2_ref, wg_ref, wu_ref, wd_ref, fn_ref), o_refs = refs[:6], refs[6:]
    mix = _read_part(mix_refs, mix_bounds)
    x = _read_part(x_refs, in_bounds) + jnp.dot(mix, wo_ref[...], preferred_element_type=F32)
    h2 = _rms(x, n2_ref[...]).astype(BF16)
    hidden = wg_ref.shape[1]
    acc = x
    for c0 in range(0, hidden, FFN_COLS):
        g = jnp.dot(h2, wg_ref[:, c0:c0 + FFN_COLS], preferred_element_type=F32)
        u = jnp.dot(h2, wu_ref[:, c0:c0 + FFN_COLS], preferred_element_type=F32)
        ff = (_silu(g) * u).astype(BF16)
        acc = acc + jnp.dot(ff, wd_ref[c0:c0 + FFN_COLS, :], preferred_element_type=F32)
    if final:
        acc = _rms(acc, fn_ref[...])
    _write_part(o_refs, out_bounds, acc)


def _ffn_call(x_parts, mix_parts, wo, n2, wg, wu, wd, fn, layer, out_rows, final):
    d = x_parts[0].shape[1]
    n = sum(p.shape[0] for p in x_parts)
    hidden = wg.shape[2]
    rows = FFN_ROWS
    assert n % rows == 0 and hidden % FFN_COLS == 0 and sum(out_rows) == n
    const = lambda i: (0, 0)

    def layer_spec(w):
        return pl.BlockSpec((None,) + w.shape[1:], lambda i: (layer, 0, 0))

    out_shape = [jax.ShapeDtypeStruct((r, d), F32) for r in out_rows]
    return pl.pallas_call(
        functools.partial(_ffn_body, in_bounds=_tile_bounds(x_parts, rows),
                          mix_bounds=_tile_bounds(mix_parts, rows),
                          out_bounds=_tile_bounds(out_shape, rows), final=final),
        grid=(n // rows,),
        in_specs=_part_specs(x_parts, rows) + _part_specs(mix_parts, rows) + [
            layer_spec(wo),
            pl.BlockSpec((1, d), const),
            layer_spec(wg),
            layer_spec(wu),
            layer_spec(wd),
            pl.BlockSpec((1, d), const),
        ],
        out_specs=_part_specs(out_shape, rows),
        out_shape=out_shape,
        compiler_params=pltpu.CompilerParams(
            dimension_semantics=("arbitrary",), vmem_limit_bytes=VMEM_LIMIT_BYTES),
        name="outproj_ffn",
    )(*x_parts, *mix_parts, wo, n2.reshape(1, d), wg, wu, wd, fn.reshape(1, d))


def _gdn_head(cfg, h, slot, delay, rs, pre, state, norm_w, masks, outs):
    eye_f, causal_bias, offdiag_f, _ = masks
    hd, gw = cfg.head_dim, cfg.gdn_width
    c = CHUNK
    g = rs.start // c
    for _ in range(delay):
        yield
    sl = slice(h * hd, (h + 1) * hd)
    q = pre["q"][rs, sl]
    k = pre["k"][rs, sl]
    kt = pre["kt"][g * gw + h * hd:g * gw + (h + 1) * hd, :]
    b = pre["beta"][rs, cfg.gdn_heads + h:cfg.gdn_heads + h + 1]
    kb = k * b
    vb = pre["v"][rs, sl] * b
    kq = _mm(jnp.concatenate([kb, q], axis=0), kt)
    yield
    gcol = pre["cum"][rs, h:h + 1]
    grow = pre["cum_t"][g * LANE + h:g * LANE + h + 1, :]
    dec = jnp.exp(gcol - grow + causal_bias)
    lower = kq[:c] * (dec * offdiag_f)
    attn = kq[c:] * dec
    t = yield from _inv_one_minus_steps(-lower, eye_f)
    eg = pre["eg"][rs, h:h + 1]
    u = _mm(t, vb)
    w = _mm(t, kb * eg)
    yield
    s = state.read(slot)
    wq_s = _mm(jnp.concatenate([w, q * eg], axis=0), s)
    yield
    g_last = gcol[c - 1:c, :]
    v_new = u - wq_s[:c]
    o = wq_s[c:] + _mm(attn, v_new)
    state.write(slot, s * jnp.exp(g_last) + _mm(kt * jnp.exp(g_last - grow), v_new))
    yield
    outs[h] = _rms(o, norm_w) * pre["gzs"][rs, sl]


def _ssd_head(cfg, h, slot, delay, rs, pre, state, d_row, masks, cb_cache, ys):
    _, causal_bias, _, _ = masks
    hd, ns, width = cfg.head_dim, cfg.ssm_state, cfg.ssm_width
    gh = cfg.gdn_heads
    c = CHUNK
    g = rs.start // c
    for _ in range(delay):
        yield
    grp = h // (cfg.ssm_heads // cfg.ssm_groups)
    bm = pre["xbc"][rs, width + grp * ns:width + (grp + 1) * ns]
    cm = pre["xbc"][rs, width + cfg.ssm_bc + grp * ns:width + cfg.ssm_bc + (grp + 1) * ns]
    if grp not in cb_cache:
        cb_cache[grp] = _mm_nt(cm, bm)
    lane = 2 * gh + h
    acol = pre["cum"][rs, lane:lane + 1]
    arow = pre["cum_t"][g * LANE + lane:g * LANE + lane + 1, :]
    a_last = acol[c - 1:c, :]
    x = pre["xbc"][rs, h * hd:(h + 1) * hd]
    xdt = x * pre["sp"][rs, lane:lane + 1]
    s = state.read(slot)
    y_off = _mm_nt(cm, s)
    dt_row = pre["sp_t"][g * LANE + lane:g * LANE + lane + 1, :]
    xt_dec = pre["xt"][g * width + h * hd:g * width + (h + 1) * hd, :] * (dt_row * jnp.exp(a_last - arow))
    state.write(slot, s * jnp.exp(a_last) + _mm(xt_dec, bm))
    yield
    lmat = jnp.exp(acol - arow + causal_bias)
    y_diag = _mm(cb_cache[grp] * lmat, xdt)
    yield
    ys[h] = y_diag + y_off * jnp.exp(acol) + d_row[:, h:h + 1] * x


def _rwkv_head(cfg, h, slot, delay, rs, pre, state, ln_w, ln_b, masks, incl2, outs):
    eye_f, _, _, strict2 = masks
    hd, rw = cfg.head_dim, cfg.rwkv_width
    c = CHUNK
    g = rs.start // c
    for _ in range(delay):
        yield
    sl = slice(h * hd, (h + 1) * hd)
    at = pre["at"][rs, sl]
    bt = pre["bt"][rs, sl]
    kt = pre["kt_r"][rs, sl]
    vh = pre["v_r"][rs, sl]
    pch = pre["pm"][rs.stop - 1:rs.stop, sl]
    ar = jnp.concatenate([at, pre["rt"][rs, sl]], axis=0)
    bk = jnp.concatenate([bt, kt], axis=0)
    cross = _mm_nt(ar, bk)
    vk = _mm(pre["vt_r"][g * rw + h * hd:g * rw + (h + 1) * hd, :], kt * pch)
    yield
    a_abk = jnp.where(strict2, cross[:c], 0.0)
    a_ab = a_abk[:, :c]
    aakv = _mm(a_abk[:, c:], vh)
    t = yield from _inv_one_minus_steps(a_ab, eye_f)
    s = state.read(slot)
    ar_s = _mm_nt(ar, s)
    yield
    u = _mm(t, ar_s[:c] + aakv)
    yield
    uv = jnp.concatenate([u, vh], axis=0)
    y = ar_s[c:] + _mm(jnp.where(incl2, cross[c:], 0.0), uv)
    ut = _mm_nt(eye_f, u)
    yield
    state.write(slot, s * pch + _mm(ut, bt * pch) + vk)
    yield
    mean = jnp.mean(y, axis=-1, keepdims=True)
    yc = y - mean
    var = jnp.mean(yc * yc, axis=-1, keepdims=True)
    yn = yc * lax.rsqrt(var + RWKV_GN_EPS) * ln_w[:, sl] + ln_b[:, sl]
    outs[h] = (yn + pre["bonus"][rs, sl] * vh) * pre["gate"][rs, sl]


def _prelude(cfg, nset, p_ref, bufs, prm, out):
    (gconv_w_ref, alog_ref, dtb_ref, sconv_w_ref, sconv_b_ref, mu_ref, w0_ref, wup_ref, a0_ref, aup_ref,
     gup_ref, kk_ref, ka_ref, rk_ref, gseg_ref, rseg_ref) = prm
    gbuf, sbuf, rbuf = bufs
    c = CHUNK
    rows = p_ref.shape[0]
    nslot = rows // c
    cw = cfg.conv_w
    gw, sw, rw = cfg.gdn_width, cfg.ssm_width, cfg.rwkv_width
    for _ in range(PRELUDE_DELAY):
        yield

    rr = lax.broadcasted_iota(jnp.int32, (rows, rows), 0)
    cc = lax.broadcasted_iota(jnp.int32, (rows, rows), 1)
    same_chunk = functools.reduce(jnp.logical_and, [(rr >= m * c) == (cc >= m * c) for m in range(1, nslot)],
                                  rr >= 0)
    tri = jnp.logical_and(same_chunk, rr >= cc).astype(BF16)

    small = p_ref[:, cfg.o_small:cfg.o_small + LANE]
    sp = _softplus(small + dtb_ref[...])
    cum = _cumsum_rows(tri, sp * (-jnp.exp(alog_ref[...])))

    def shift(s, pr):
        n = pr.shape[0]
        rbuf[s, HIST:HIST + n, :] = pr
        prev = rbuf[s, HIST - 1:HIST - 1 + n, :]
        rbuf[s, HIST - 1:HIST, :] = pr[n - 1:n, :]
        return pr + (prev - pr) * mu_ref[...]

    xm = _per_set(shift, p_ref[:, cfg.o_rwkv:cfg.o_rwkv + cfg.rwkv_cols], nset)
    c0 = 3 * rw
    c1 = c0 + cfg.lora_w
    c2 = c1 + cfg.lora_a
    r = xm[:, :rw]
    k = xm[:, rw:2 * rw]
    v = xm[:, 2 * rw:c0]
    lora_w = _mm(jnp.tanh(xm[:, c0:c1]), wup_ref[...])
    lora_a = _mm(xm[:, c1:c2], aup_ref[...])
    gate = _mm(jax.nn.sigmoid(xm[:, c2:]), gup_ref[...])

    qkv = _per_set(lambda s, x: _conv_chunk(gbuf.at[s], x, gconv_w_ref, cw),
                   p_ref[:, cfg.o_gdn_qkv:cfg.o_gdn_qkv + 3 * gw], nset)
    qkv = _silu(qkv)
    q_raw, k_raw = qkv[:, :gw], qkv[:, gw:2 * gw]
    ssq = _split_mm(jnp.concatenate([q_raw * q_raw, k_raw * k_raw], axis=0), gseg_ref[...])
    yield
    w_log = -_softplus(-(w0_ref[...] + lora_w)) - 0.5
    logw = -jnp.exp(w_log)
    rcum = _cumsum_rows(tri, logw)
    iclr = jax.nn.sigmoid(a0_ref[...] + lora_a)
    k2 = k * (1.0 + (iclr - 1.0) * ka_ref[...])
    kk_raw = k * kk_ref[...]
    kk_ssq = _split_mm(kk_raw * kk_raw, rseg_ref[...])
    bonus = _split_mm(r * k2 * rk_ref[...], rseg_ref[...])
    yield
    xbc = _per_set(lambda s, x: _conv_chunk(sbuf.at[s], x, sconv_w_ref, cw),
                   p_ref[:, cfg.o_ssm_xbc:cfg.o_ssm_xbc + sw + 2 * cfg.ssm_bc], nset)
    xbc = _silu(xbc + sconv_b_ref[...])
    k_all = k_raw * lax.rsqrt(ssq[rows:] + NORM_EPS)
    pm = jnp.exp(rcum)
    pinv = jnp.exp(-rcum)
    kkn = kk_raw * lax.rsqrt(kk_ssq + NORM_EPS)
    out.update(
        q=q_raw * lax.rsqrt(ssq[:rows] + NORM_EPS) * (cfg.head_dim ** -0.5), k=k_all, v=qkv[:, 2 * gw:],
        gzs=_silu(p_ref[:, cfg.o_gdn_z:cfg.o_gdn_z + gw]),
        cum=cum, sp=sp, beta=jax.nn.sigmoid(small), eg=jnp.exp(cum),
        xbc=xbc, szs=_silu(p_ref[:, cfg.o_ssm_z:cfg.o_ssm_z + sw]),
        at=-kkn * jnp.exp(rcum - logw), bt=kkn * iclr * pinv, rt=r * pm, kt_r=k2 * pinv, v_r=v, gate=gate,
        bonus=bonus, pm=pm,
        kt=jnp.concatenate([k_all[g * c:(g + 1) * c].T for g in range(nslot)], axis=0),
        cum_t=jnp.concatenate([cum[g * c:(g + 1) * c].T for g in range(nslot)], axis=0),
        sp_t=jnp.concatenate([sp[g * c:(g + 1) * c].T for g in range(nslot)], axis=0),
        xt=jnp.concatenate([xbc[g * c:(g + 1) * c, :sw].T for g in range(nslot)], axis=0),
        vt_r=jnp.concatenate([v[g * c:(g + 1) * c].T for g in range(nslot)], axis=0),
    )


def _prelude_shapes(cfg, rows):
    nslot = rows // CHUNK
    gw, sw, rw = cfg.gdn_width, cfg.ssm_width, cfg.rwkv_width
    shapes = {name: (rows, gw) for name in ("q", "k", "v", "gzs")}
    shapes.update({name: (rows, LANE) for name in ("cum", "sp", "beta", "eg")})
    shapes.update(xbc=(rows, sw + 2 * cfg.ssm_bc), szs=(rows, sw))
    shapes.update({name: (rows, rw) for name in ("at", "bt", "rt", "kt_r", "v_r", "gate", "bonus", "pm")})
    shapes.update(kt=(nslot * gw, CHUNK), cum_t=(nslot * LANE, CHUNK), sp_t=(nslot * LANE, CHUNK),
                  xt=(nslot * sw, CHUNK), vt_r=(nslot * rw, CHUNK))
    return shapes


def _mixer_body(p_ref, gdn0_ref, gdnc0_ref, ssm0_ref, ssmc0_ref, rwkv0_ref, shift0_ref,
                gconv_w_ref, alog_ref, dtb_ref, gnorm_ref,
                sconv_w_ref, sconv_b_ref, sd_ref, snorm_ref,
                mu_ref, w0_ref, wup_ref, a0_ref, aup_ref, gup_ref, kk_ref, ka_ref, rk_ref, lnw_ref, lnb_ref,
                gseg_ref, rseg_ref,
                mix_ref, gdn_out_ref, gdnc_out_ref, ssm_out_ref, ssmc_out_ref, rwkv_out_ref, shift_out_ref,
                gdn_s, ssm_s, rwkv_s, gbuf, sbuf, rbuf, *pre_refs, cfg, nset, steps, nblocks, names):
    pre = dict(zip(names, pre_refs))
    c = CHUNK
    rows = p_ref.shape[0]
    nslot = rows // c
    per_set = nslot // nset
    slot_plan = [(s, j) for s in range(nset) for j in range(per_set)]
    cw = cfg.conv_w
    gh, sh = cfg.gdn_heads, cfg.ssm_heads
    t = pl.program_id(0)
    p_block = jnp.minimum(t, nblocks - 1)
    c_block = jnp.maximum(t - 1, 0)

    @pl.when(t == 0)
    def _clear():
        for ref in (gdn_s, ssm_s, rwkv_s) + tuple(pre_refs):
            ref[...] = jnp.zeros(ref.shape, ref.dtype)

    @pl.when(p_block % steps == 0)
    def _load_history():
        gbuf[:, HIST - (cw - 1):HIST, :] = gdnc0_ref[...]
        sbuf[:, HIST - (cw - 1):HIST, :] = ssmc0_ref[...]
        rbuf[:, HIST - 1:HIST, :] = shift0_ref[...]

    @pl.when(jnp.logical_and(t >= 1, c_block % steps == 0))
    def _load_state():
        gdn_s[...] = gdn0_ref[...]
        ssm_s[...] = ssm0_ref[...]
        rwkv_s[...] = rwkv0_ref[...]

    ri = lax.broadcasted_iota(jnp.int32, (c, c), 0)
    ci = lax.broadcasted_iota(jnp.int32, (c, c), 1)
    eye_f = (ri == ci).astype(F32)
    offdiag_f = 1.0 - eye_f
    causal_bias = jnp.where(ri >= ci, 0.0, MASKED_EXPONENT)
    ri2 = lax.broadcasted_iota(jnp.int32, (c, 2 * c), 0)
    ci2 = lax.broadcasted_iota(jnp.int32, (c, 2 * c), 1)
    ci2 = jnp.where(ci2 >= c, ci2 - c, ci2)
    strict2 = ri2 > ci2
    incl2 = ri2 >= ci2
    masks = (eye_f, causal_bias, offdiag_f, strict2)

    gnorm, sd, ln_w, ln_b = gnorm_ref[...], sd_ref[...], lnw_ref[...], lnb_ref[...]
    gdn_states = [[_State(gdn_s, (s, h)) for h in range(gh)] for s in range(nset)]
    ssm_states = [[_State(ssm_s, (s, h)) for h in range(sh)] for s in range(nset)]
    rwkv_states = [[_State(rwkv_s, (s, h)) for h in range(cfg.rwkv_heads)] for s in range(nset)]
    gdn_o = [[None] * gh for _ in slot_plan]
    ssd_y = [[None] * sh for _ in slot_plan]
    rwkv_o = [[None] * cfg.rwkv_heads for _ in slot_plan]

    new_pre = {}
    prm = (gconv_w_ref, alog_ref, dtb_ref, sconv_w_ref, sconv_b_ref, mu_ref, w0_ref, wup_ref, a0_ref, aup_ref,
           gup_ref, kk_ref, ka_ref, rk_ref, gseg_ref, rseg_ref)
    tasks = []
    for g, (s, j) in enumerate(slot_plan):
        rs = slice(g * c, (g + 1) * c)
        delay = j * SLOT_DELAY
        tasks += [_gdn_head(cfg, h, j, delay, rs, pre, gdn_states[s][h], gnorm, masks, gdn_o[g]) for h in range(gh)]
        tasks += [_rwkv_head(cfg, h, j, delay, rs, pre, rwkv_states[s][h], ln_w, ln_b, masks, incl2, rwkv_o[g])
                  for h in range(cfg.rwkv_heads)]
        cb_cache = {}
        tasks += [_ssd_head(cfg, h, j, delay + h, rs, pre, ssm_states[s][h], sd, masks, cb_cache, ssd_y[g])
                  for h in range(sh)]
    tasks.append(_prelude(cfg, nset, p_ref, (gbuf, sbuf, rbuf), prm, new_pre))
    _run_interleaved(tasks)

    ng = cfg.ssm_groups
    gw = cfg.ssm_width // ng
    snorm = snorm_ref[...]
    for g in range(nslot):
        rs = slice(g * c, (g + 1) * c)
        ssd_o = []
        for grp in range(ng):
            yg = jnp.concatenate(ssd_y[g][grp * (sh // ng):(grp + 1) * (sh // ng)], axis=-1)
            yg = yg * pre["szs"][rs, grp * gw:(grp + 1) * gw]
            yg = yg * lax.rsqrt(jnp.mean(yg * yg, axis=-1, keepdims=True) + NORM_EPS)
            ssd_o.append(yg * snorm[:, grp * gw:(grp + 1) * gw])
        mix_ref[rs, :] = jnp.concatenate(gdn_o[g] + ssd_o + rwkv_o[g], axis=-1).astype(mix_ref.dtype)

    gdn_out_ref[...] = gdn_s[...]
    ssm_out_ref[...] = ssm_s[...]
    rwkv_out_ref[...] = rwkv_s[...]
    gdnc_out_ref[...] = gbuf[:, HIST - (cw - 1):HIST, :]
    ssmc_out_ref[...] = sbuf[:, HIST - (cw - 1):HIST, :]
    shift_out_ref[...] = rbuf[:, HIST - 1:HIST, :]
    for name in names:
        pre[name][...] = new_pre[name]


def _mixer_call(cfg, proj, row0, init, layer, chunks_per_seq, prm):
    nseq = init[0].shape[1]
    if chunks_per_seq % SLOTS == 0:
        nset, per_set = 1, SLOTS
    elif chunks_per_seq == 1 and nseq % SLOTS == 0:
        nset, per_set = SLOTS, 1
    else:
        nset, per_set = 1, 1
    rows = nset * per_set * CHUNK
    steps = chunks_per_seq // per_set
    assert row0 % rows == 0 and nseq % nset == 0
    blk0 = row0 // rows
    nblocks = (nseq // nset) * steps

    def p_block(t):
        return jnp.minimum(t, nblocks - 1)

    def c_block(t):
        return jnp.maximum(t - 1, 0)

    def in_spec(st, block_of):
        nd = st.ndim
        return pl.BlockSpec((None, nset) + st.shape[2:], lambda t: (layer, block_of(t) // steps) + (0,) * (nd - 2))

    def out_spec(st, block_of):
        nd = st.ndim - 1
        return pl.BlockSpec((nset,) + st.shape[2:], lambda t: (block_of(t) // steps,) + (0,) * (nd - 1))

    gdn0, gdnc0, ssm0, ssmc0, rwkv0, shift0 = init
    side = (c_block, p_block, c_block, p_block, c_block, p_block)
    in_specs = ([pl.BlockSpec((rows, proj.shape[1]), lambda t: (blk0 + p_block(t), 0))]
                + [in_spec(st, blk) for st, blk in zip(init, side)]
                + [pl.BlockSpec(p.shape, lambda t: (0, 0)) for p in prm])
    out_specs = ([pl.BlockSpec((rows, cfg.mix_width), lambda t: (c_block(t), 0))]
                 + [out_spec(st, blk) for st, blk in zip(init, side)])
    out_shape = [jax.ShapeDtypeStruct((nseq * chunks_per_seq * CHUNK, cfg.mix_width), BF16)] + [
        jax.ShapeDtypeStruct(st.shape[1:], F32) for st in init]
    shapes = _prelude_shapes(cfg, rows)
    names = tuple(shapes)
    scratch = [
        pltpu.VMEM((nset,) + gdn0.shape[2:], F32),
        pltpu.VMEM((nset,) + ssm0.shape[2:], F32),
        pltpu.VMEM((nset,) + rwkv0.shape[2:], F32),
        pltpu.VMEM((nset, HIST + per_set * CHUNK, gdnc0.shape[3]), F32),
        pltpu.VMEM((nset, HIST + per_set * CHUNK, ssmc0.shape[3]), F32),
        pltpu.VMEM((nset, HIST + per_set * CHUNK, shift0.shape[3]), F32),
    ] + [pltpu.VMEM(shapes[name], F32) for name in names]
    return pl.pallas_call(
        functools.partial(_mixer_body, cfg=cfg, nset=nset, steps=steps, nblocks=nblocks, names=names),
        grid=(nblocks + 1,),
        in_specs=in_specs,
        out_specs=out_specs,
        out_shape=out_shape,
        scratch_shapes=scratch,
        compiler_params=pltpu.CompilerParams(
            dimension_semantics=("arbitrary",), vmem_limit_bytes=VMEM_LIMIT_BYTES),
        name="mixers",
    )(proj, *init, *prm)


def _pad_lanes(v, width=LANE):
    v = v.reshape(1, -1)
    return jnp.pad(v, ((0, 0), (0, width - v.shape[1])))


def kernel(x_prompt, x_sample, state_gdn, state_gdn_conv, state_ssm, state_ssm_conv, state_rwkv, state_rwkv_shift, norm1_w, w_in, gdn_conv_w, gdn_A_log, gdn_dt_bias, gdn_norm_w, ssm_conv_w, ssm_conv_b, ssm_A_log, ssm_dt_bias, ssm_D, ssm_norm_w, rwkv_mu, rwkv_w0, rwkv_w_up, rwkv_a0, rwkv_a_up, rwkv_g_up, rwkv_k_k, rwkv_k_a, rwkv_r_k, rwkv_ln_w, rwkv_ln_b, w_out, norm2_w, ffn_w_gate, ffn_w_up, ffn_w_down, final_norm_w):
    depth = w_in.shape[0]
    nbp, tp, d = x_prompt.shape
    nbs, ts, _ = x_sample.shape
    hd = state_gdn.shape[-1]
    ssm_width = state_ssm.shape[2] * hd
    cfg = Cfg(
        d_model=d, head_dim=hd, conv_w=gdn_conv_w.shape[1],
        gdn_heads=state_gdn.shape[2], ssm_heads=state_ssm.shape[2],
        ssm_groups=(ssm_conv_w.shape[2] - ssm_width) // (2 * state_ssm.shape[-1]),
        ssm_state=state_ssm.shape[-1], rwkv_heads=state_rwkv.shape[2],
        lora_w=rwkv_w_up.shape[1], lora_a=rwkv_a_up.shape[1], lora_g=rwkv_g_up.shape[1])
    assert tp % CHUNK == 0 and ts % CHUNK == 0 and tp >= cfg.conv_w and ts >= cfg.conv_w
    assert 2 * cfg.gdn_heads + cfg.ssm_heads <= LANE
    gw, sw = cfg.gdn_width, cfg.ssm_width
    gdn_cols = 4 * gw + 2 * cfg.gdn_heads
    ssm_cols = 2 * sw + 2 * cfg.ssm_bc + cfg.ssm_heads
    assert w_in.shape[2] == gdn_cols + ssm_cols + cfg.rwkv_cols

    x_parts = [x_prompt.reshape(nbp * tp, d), x_sample.reshape(nbs * ts, d)]

    w_in_t = jnp.swapaxes(w_in, 1, 2)

    def small_row(l, gdn_first, gdn_second, ssm_part):
        return _pad_lanes(jnp.concatenate([gdn_first[l], gdn_second, ssm_part[l]]))

    zeros_g = jnp.zeros((cfg.gdn_heads,), F32)
    head_of = jnp.arange(gw) // hd
    gdn_seg = (head_of[:, None] == head_of[None, :]).astype(BF16)
    head_of = jnp.arange(cfg.rwkv_width) // hd
    rwkv_seg = (head_of[:, None] == head_of[None, :]).astype(BF16)
    wo_b, wg_b, wu_b, wd_b = (w.astype(BF16) for w in (w_out, ffn_w_gate, ffn_w_up, ffn_w_down))
    sample_states = (state_gdn, state_gdn_conv, state_ssm, state_ssm_conv, state_rwkv, state_rwkv_shift)

    p_init = tuple(jnp.zeros((1, nbp) + st.shape[2:], F32) for st in sample_states)
    s_init = tuple(st.astype(F32) for st in sample_states)
    p_states = [[] for _ in sample_states]
    s_states = [[] for _ in sample_states]
    for l in range(depth):
        proj = _proj_call(cfg, x_parts, norm1_w[l], w_in_t, l)
        prm = (
            gdn_conv_w[l], small_row(l, gdn_A_log, zeros_g, ssm_A_log), small_row(l, gdn_dt_bias, zeros_g, ssm_dt_bias),
            gdn_norm_w[l].reshape(1, -1),
            ssm_conv_w[l], ssm_conv_b[l].reshape(1, -1), ssm_D[l].reshape(1, -1), ssm_norm_w[l].reshape(1, -1),
            rwkv_mu[l].reshape(1, -1), rwkv_w0[l].reshape(1, -1), rwkv_w_up[l], rwkv_a0[l].reshape(1, -1),
            rwkv_a_up[l], rwkv_g_up[l], rwkv_k_k[l].reshape(1, -1), rwkv_k_a[l].reshape(1, -1),
            rwkv_r_k[l].reshape(1, -1), rwkv_ln_w[l].reshape(1, -1), rwkv_ln_b[l].reshape(1, -1),
            gdn_seg, rwkv_seg,
        )
        mix_p, *p_new = _mixer_call(cfg, proj, 0, p_init, 0, tp // CHUNK, prm)
        mix_s, *s_new = _mixer_call(cfg, proj, nbp * tp, s_init, l, ts // CHUNK, prm)
        final = l == depth - 1
        out_rows = [nbp * tp, nbs * ts] if final else [nbp * tp + nbs * ts]
        x_parts = _ffn_call(x_parts, [mix_p, mix_s], wo_b, norm2_w[l], wg_b, wu_b, wd_b, final_norm_w, l,
                            out_rows, final)
        for acc, st in zip(p_states, p_new):
            acc.append(st)
        for acc, st in zip(s_states, s_new):
            acc.append(st)

    y_prompt = x_parts[0].reshape(nbp, tp, d)
    y_sample = x_parts[1].reshape(nbs, ts, d)
    return (y_prompt, y_sample, *(jnp.stack(st) for st in p_states), *(jnp.stack(st) for st in s_states))
```

```python
import functools
import math
from typing import NamedTuple

import jax
import jax.numpy as jnp
from jax import lax
from jax.experimental import pallas as pl
from jax.experimental.pallas import tpu as pltpu

F32 = jnp.float32
BF16 = jnp.bfloat16

CHUNK = 64
SLOTS = 2
PRELUDE_DELAY = 5
SLOT_DELAY = 4
NORM_EPS = 1e-6
RWKV_GN_EPS = 64e-5
MASKED_EXPONENT = -1e30
LANE = 128
HIST = 8
VMEM_LIMIT_BYTES = 56 * 1024 * 1024
PROJ_ROWS = 512
FFN_ROWS = 512
FFN_COLS = 256


class Cfg(NamedTuple):
    d_model: int
    head_dim: int
    conv_w: int
    gdn_heads: int
    ssm_heads: int
    ssm_groups: int
    ssm_state: int
    rwkv_heads: int
    lora_w: int
    lora_a: int
    lora_g: int

    @property
    def gdn_width(self):
        return self.gdn_heads * self.head_dim

    @property
    def ssm_width(self):
        return self.ssm_heads * self.head_dim

    @property
    def ssm_bc(self):
        return self.ssm_groups * self.ssm_state

    @property
    def rwkv_width(self):
        return self.rwkv_heads * self.head_dim

    @property
    def rwkv_cols(self):
        return 3 * self.rwkv_width + self.lora_w + self.lora_a + self.lora_g

    @property
    def o_gdn_qkv(self):
        return 0

    @property
    def o_gdn_z(self):
        return 3 * self.gdn_width

    @property
    def o_ssm_z(self):
        return self.o_gdn_z + self.gdn_width

    @property
    def o_ssm_xbc(self):
        return self.o_ssm_z + self.ssm_width

    @property
    def o_rwkv(self):
        return self.o_ssm_xbc + self.ssm_width + 2 * self.ssm_bc

    @property
    def o_small(self):
        return self.o_rwkv + self.rwkv_cols

    @property
    def proj_cols(self):
        return self.o_small + LANE

    @property
    def mix_width(self):
        return self.gdn_width + self.ssm_width + self.rwkv_width


def _rms(x, w):
    return x * lax.rsqrt(jnp.mean(x * x, axis=-1, keepdims=True) + NORM_EPS) * w


def _softplus(x):
    return jnp.maximum(x, 0.0) + jnp.log1p(jnp.exp(-jnp.abs(x)))


def _silu(x):
    return x * jax.nn.sigmoid(x)


def _mm(a, b):
    return jnp.dot(a.astype(BF16), b.astype(BF16), preferred_element_type=F32)


def _mm_nt(a, b):
    return lax.dot_general(a.astype(BF16), b.astype(BF16), (((1,), (1,)), ((), ())),
                           preferred_element_type=F32)


def _split_mm(x, m):
    hi = x.astype(BF16)
    lo = (x - hi.astype(F32)).astype(BF16)
    m = m.astype(BF16)
    return jnp.dot(hi, m, preferred_element_type=F32) + jnp.dot(lo, m, preferred_element_type=F32)


def _cumsum_rows(tri, x):
    hi = x.astype(BF16)
    lo = (x - hi.astype(F32)).astype(BF16)
    return jnp.dot(tri, hi, preferred_element_type=F32) + jnp.dot(tri, lo, preferred_element_type=F32)


def _inv_one_minus_steps(n, eye_f):
    c = n.shape[0]
    t = eye_f + n
    p = _mm(n, n)
    yield
    for _ in range(int(math.log2(c)) - 2):
        step = _mm(t, p)
        p_next = _mm(p, p)
        yield
        t = t + step
        p = p_next
    step = _mm(t, p)
    yield
    return t + step


def _run_interleaved(tasks):
    tasks = list(tasks)
    while tasks:
        alive = []
        for task in tasks:
            try:
                spawned = next(task)
            except StopIteration:
                continue
            alive.append(task)
            if spawned:
                alive.extend(spawned)
        tasks = alive


class _State:
    def __init__(self, ref, index):
        self.ref, self.index, self.version = ref, index, 0

    def read(self, slot):
        assert self.version == slot, "chunk slot reads a state the previous slot has not written yet"
        return self.ref[self.index]

    def write(self, slot, value):
        assert self.version == slot
        self.ref[self.index] = value
        self.version += 1


def _conv_chunk(buf_ref, x, w_ref, conv_w):
    c = x.shape[0]
    buf_ref[HIST:HIST + c, :] = x
    y = x * w_ref[conv_w - 1:conv_w, :]
    for j in range(conv_w - 1):
        lo = HIST - (conv_w - 1) + j
        y = y + buf_ref[lo:lo + c, :] * w_ref[j:j + 1, :]
    tail = buf_ref[HIST + c - (conv_w - 1):HIST + c, :]
    buf_ref[HIST - (conv_w - 1):HIST, :] = tail
    return y


def _per_set(fn, x, nset):
    rows = x.shape[0] // nset
    return jnp.concatenate([fn(s, x[s * rows:(s + 1) * rows]) for s in range(nset)], axis=0)


def _tile_bounds(parts, rows):
    bounds, lo = [], 0
    for p in parts:
        assert p.shape[0] % rows == 0
        bounds.append((lo, lo + p.shape[0] // rows))
        lo = bounds[-1][1]
    return bounds


def _part_specs(parts, rows):
    return [pl.BlockSpec((rows, p.shape[1]), lambda i, lo=lo, hi=hi: (jnp.clip(i - lo, 0, hi - lo - 1), 0))
            for p, (lo, hi) in zip(parts, _tile_bounds(parts, rows))]


def _read_part(refs, bounds):
    i = pl.program_id(0)
    x = refs[-1][...]
    for ref, (_, hi) in reversed(list(zip(refs[:-1], bounds[:-1]))):
        x = jnp.where(i < hi, ref[...], x)
    return x


def _write_part(refs, bounds, value):
    i = pl.program_id(0)
    if len(refs) == 1:
        refs[0][...] = value
        return
    for ref, (lo, hi) in zip(refs, bounds):
        @pl.when(jnp.logical_and(i >= lo, i < hi))
        def _(ref=ref):
            ref[...] = value


def _proj_body(*refs, bounds, segments, small_cols):
    x_refs, (nw_ref, w_ref, o_ref, w_scr) = refs[:len(bounds)], refs[len(bounds):]

    @pl.when(pl.program_id(0) == 0)
    def _regroup():
        dst = 0
        for src, width in segments:
            w_scr[dst:dst + width, :] = w_ref[src:src + width, :].astype(BF16)
            dst += width
        small = jnp.concatenate([w_ref[src:src + width, :] for src, width in small_cols], axis=0)
        small = jnp.concatenate([small, jnp.zeros((LANE - small.shape[0], small.shape[1]), F32)], axis=0)
        w_scr[dst:dst + LANE, :] = small.astype(BF16)

    h = _rms(_read_part(x_refs, bounds), nw_ref[...])
    o_ref[...] = _mm_nt(h, w_scr[...])


def _proj_call(cfg, x_parts, norm_w, w_in, layer):
    d = x_parts[0].shape[1]
    n = sum(p.shape[0] for p in x_parts)
    rows = PROJ_ROWS
    gw, sw = cfg.gdn_width, cfg.ssm_width
    gdn_cols = 4 * gw + 2 * cfg.gdn_heads
    ssm_cols = 2 * sw + 2 * cfg.ssm_bc + cfg.ssm_heads
    assert w_in.shape[1] == gdn_cols + ssm_cols + cfg.rwkv_cols
    segments = ((0, 4 * gw), (gdn_cols, 2 * sw + 2 * cfg.ssm_bc), (gdn_cols + ssm_cols, cfg.rwkv_cols))
    small_cols = ((4 * gw, 2 * cfg.gdn_heads), (gdn_cols + 2 * sw + 2 * cfg.ssm_bc, cfg.ssm_heads))
    cols = cfg.proj_cols
    return pl.pallas_call(
        functools.partial(_proj_body, bounds=_tile_bounds(x_parts, rows), segments=segments, small_cols=small_cols),
        grid=(n // rows,),
        in_specs=_part_specs(x_parts, rows) + [
            pl.BlockSpec((1, d), lambda i: (0, 0)),
            pl.BlockSpec((None,) + w_in.shape[1:], lambda i: (layer, 0, 0), pipeline_mode=pl.Buffered(1)),
        ],
        out_specs=pl.BlockSpec((rows, cols), lambda i: (i, 0)),
        out_shape=jax.ShapeDtypeStruct((n, cols), F32),
        scratch_shapes=[pltpu.VMEM((cols, d), BF16)],
        compiler_params=pltpu.CompilerParams(
            dimension_semantics=("arbitrary",), vmem_limit_bytes=VMEM_LIMIT_BYTES),
        name="norm_proj",
    )(*x_parts, norm_w.reshape(1, d), w_in)


def _ffn_body(*refs, in_bounds, mix_bounds, out_bounds, final):
    nx, nm = len(in_bounds), len(mix_bounds)
    x_refs, mix_refs, refs = refs[:nx], refs[nx:nx + nm], refs[nx + nm:]
    (wo_ref, n2_ref, wg_ref, wu_ref, wd_ref, fn_ref), o_refs = refs[:6], refs[6:]
    mix = _read_part(mix_refs, mix_bounds)
    x = _read_part(x_refs, in_bounds) + jnp.dot(mix, wo_ref[...], preferred_element_type=F32)
    h2 = _rms(x, n2_ref[...]).astype(BF16)
    hidden = wg_ref.shape[1]
    acc = x
    for c0 in range(0, hidden, FFN_COLS):
        g = jnp.dot(h2, wg_ref[:, c0:c0 + FFN_COLS], preferred_element_type=F32)
        u = jnp.dot(h2, wu_ref[:, c0:c0 + FFN_COLS], preferred_element_type=F32)
        ff = (_silu(g) * u).astype(BF16)
        acc = acc + jnp.dot(ff, wd_ref[c0:c0 + FFN_COLS, :], preferred_element_type=F32)
    if final:
        acc = _rms(acc, fn_ref[...])
    _write_part(o_refs, out_bounds, acc)


def _ffn_call(x_parts, mix_parts, wo, n2, wg, wu, wd, fn, layer, out_rows, final):
    d = x_parts[0].shape[1]
    n = sum(p.shape[0] for p in x_parts)
    hidden = wg.shape[2]
    rows = FFN_ROWS
    assert n % rows == 0 and hidden % FFN_COLS == 0 and sum(out_rows) == n
    const = lambda i: (0, 0)

    def layer_spec(w):
        return pl.BlockSpec((None,) + w.shape[1:], lambda i: (layer, 0, 0))

    out_shape = [jax.ShapeDtypeStruct((r, d), F32) for r in out_rows]
    return pl.pallas_call(
        functools.partial(_ffn_body, in_bounds=_tile_bounds(x_parts, rows),
                          mix_bounds=_tile_bounds(mix_parts, rows),
                          out_bounds=_tile_bounds(out_shape, rows), final=final),
        grid=(n // rows,),
        in_specs=_part_specs(x_parts, rows) + _part_specs(mix_parts, rows) + [
            layer_spec(wo),
            pl.BlockSpec((1, d), const),
            layer_spec(wg),
            layer_spec(wu),
            layer_spec(wd),
            pl.BlockSpec((1, d), const),
        ],
        out_specs=_part_specs(out_shape, rows),
        out_shape=out_shape,
        compiler_params=pltpu.CompilerParams(
            dimension_semantics=("arbitrary",), vmem_limit_bytes=VMEM_LIMIT_BYTES),
        name="outproj_ffn",
    )(*x_parts, *mix_parts, wo, n2.reshape(1, d), wg, wu, wd, fn.reshape(1, d))


def _gdn_head(cfg, h, slot, delay, rs, pre, state, norm_w, masks, outs):
    eye_f, causal_bias, offdiag_f, _ = masks
    hd, gw = cfg.head_dim, cfg.gdn_width
    c = CHUNK
    g = rs.start // c
    for _ in range(delay):
        yield
    sl = slice(h * hd, (h + 1) * hd)
    q = pre["q"][rs, sl]
    k = pre["k"][rs, sl]
    kt = pre["kt"][g * gw + h * hd:g * gw + (h + 1) * hd, :]
    b = pre["beta"][rs, cfg.gdn_heads + h:cfg.gdn_heads + h + 1]
    kb = k * b
    vb = pre["v"][rs, sl] * b
    kq = _mm(jnp.concatenate([kb, q], axis=0), kt)
    yield
    gcol = pre["cum"][rs, h:h + 1]
    grow = pre["cum_t"][g * LANE + h:g * LANE + h + 1, :]
    dec = jnp.exp(gcol - grow + causal_bias)
    lower = kq[:c] * (dec * offdiag_f)
    attn = kq[c:] * dec
    t = yield from _inv_one_minus_steps(-lower, eye_f)
    eg = pre["eg"][rs, h:h + 1]
    u = _mm(t, vb)
    w = _mm(t, kb * eg)
    yield
    s = state.read(slot)
    wq_s = _mm(jnp.concatenate([w, q * eg], axis=0), s)
    yield
    g_last = gcol[c - 1:c, :]
    v_new = u - wq_s[:c]
    o = wq_s[c:] + _mm(attn, v_new)
    state.write(slot, s * jnp.exp(g_last) + _mm(kt * jnp.exp(g_last - grow), v_new))
    yield
    outs[h] = _rms(o, norm_w) * pre["gzs"][rs, sl]


def _ssd_head(cfg, h, slot, delay, rs, pre, state, d_row, masks, cb_cache, ys):
    _, causal_bias, _, _ = masks
    hd, ns, width = cfg.head_dim, cfg.ssm_state, cfg.ssm_width
    gh = cfg.gdn_heads
    c = CHUNK
    g = rs.start // c
    for _ in range(delay):
        yield
    grp = h // (cfg.ssm_heads // cfg.ssm_groups)
    bm = pre["xbc"][rs, width + grp * ns:width + (grp + 1) * ns]
    cm = pre["xbc"][rs, width + cfg.ssm_bc + grp * ns:width + cfg.ssm_bc + (grp + 1) * ns]
    if grp not in cb_cache:
        cb_cache[grp] = _mm_nt(cm, bm)
    lane = 2 * gh + h
    acol = pre["cum"][rs, lane:lane + 1]
    arow = pre["cum_t"][g * LANE + lane:g * LANE + lane + 1, :]
    a_last = acol[c - 1:c, :]
    x = pre["xbc"][rs, h * hd:(h + 1) * hd]
    xdt = x * pre["sp"][rs, lane:lane + 1]
    s = state.read(slot)
    y_off = _mm_nt(cm, s)
    dt_row = pre["sp_t"][g * LANE + lane:g * LANE + lane + 1, :]
    xt_dec = pre["xt"][g * width + h * hd:g * width + (h + 1) * hd, :] * (dt_row * jnp.exp(a_last - arow))
    state.write(slot, s * jnp.exp(a_last) + _mm(xt_dec, bm))
    yield
    lmat = jnp.exp(acol - arow + causal_bias)
    y_diag = _mm(cb_cache[grp] * lmat, xdt)
    yield
    ys[h] = y_diag + y_off * jnp.exp(acol) + d_row[:, h:h + 1] * x


def _rwkv_head(cfg, h, slot, delay, rs, pre, state, ln_w, ln_b, masks, incl2, outs):
    eye_f, _, _, strict2 = masks
    hd, rw = cfg.head_dim, cfg.rwkv_width
    c = CHUNK
    g = rs.start // c
    for _ in range(delay):
        yield
    sl = slice(h * hd, (h + 1) * hd)
    at = pre["at"][rs, sl]
    bt = pre["bt"][rs, sl]
    kt = pre["kt_r"][rs, sl]
    vh = pre["v_r"][rs, sl]
    pch = pre["pm"][rs.stop - 1:rs.stop, sl]
    ar = jnp.concatenate([at, pre["rt"][rs, sl]], axis=0)
    bk = jnp.concatenate([bt, kt], axis=0)
    cross = _mm_nt(ar, bk)
    vk = _mm(pre["vt_r"][g * rw + h * hd:g * rw + (h + 1) * hd, :], kt * pch)
    yield
    a_abk = jnp.where(strict2, cross[:c], 0.0)
    a_ab = a_abk[:, :c]
    aakv = _mm(a_abk[:, c:], vh)
    t = yield from _inv_one_minus_steps(a_ab, eye_f)
    s = state.read(slot)
    ar_s = _mm_nt(ar, s)
    yield
    u = _mm(t, ar_s[:c] + aakv)
    yield
    uv = jnp.concatenate([u, vh], axis=0)
    y = ar_s[c:] + _mm(jnp.where(incl2, cross[c:], 0.0), uv)
    ut = _mm_nt(eye_f, u)
    yield
    state.write(slot, s * pch + _mm(ut, bt * pch) + vk)
    yield
    mean = jnp.mean(y, axis=-1, keepdims=True)
    yc = y - mean
    var = jnp.mean(yc * yc, axis=-1, keepdims=True)
    yn = yc * lax.rsqrt(var + RWKV_GN_EPS) * ln_w[:, sl] + ln_b[:, sl]
    outs[h] = (yn + pre["bonus"][rs, sl] * vh) * pre["gate"][rs, sl]


def _prelude(cfg, nset, p_ref, bufs, prm, out):
    (gconv_w_ref, alog_ref, dtb_ref, sconv_w_ref, sconv_b_ref, mu_ref, w0_ref, wup_ref, a0_ref, aup_ref,
     gup_ref, kk_ref, ka_ref, rk_ref, gseg_ref, rseg_ref) = prm
    gbuf, sbuf, rbuf = bufs
    c = CHUNK
    rows = p_ref.shape[0]
    nslot = rows // c
    cw = cfg.conv_w
    gw, sw, rw = cfg.gdn_width, cfg.ssm_width, cfg.rwkv_width
    for _ in range(PRELUDE_DELAY):
        yield

    rr = lax.broadcasted_iota(jnp.int32, (rows, rows), 0)
    cc = lax.broadcasted_iota(jnp.int32, (rows, rows), 1)
    same_chunk = functools.reduce(jnp.logical_and, [(rr >= m * c) == (cc >= m * c) for m in range(1, nslot)],
                                  rr >= 0)
    tri = jnp.logical_and(same_chunk, rr >= cc).astype(BF16)

    small = p_ref[:, cfg.o_small:cfg.o_small + LANE]
    sp = _softplus(small + dtb_ref[...])
    cum = _cumsum_rows(tri, sp * (-jnp.exp(alog_ref[...])))

    def shift(s, pr):
        n = pr.shape[0]
        rbuf[s, HIST:HIST + n, :] = pr
        prev = rbuf[s, HIST - 1:HIST - 1 + n, :]
        rbuf[s, HIST - 1:HIST, :] = pr[n - 1:n, :]
        return pr + (prev - pr) * mu_ref[...]

    xm = _per_set(shift, p_ref[:, cfg.o_rwkv:cfg.o_rwkv + cfg.rwkv_cols], nset)
    c0 = 3 * rw
    c1 = c0 + cfg.lora_w
    c2 = c1 + cfg.lora_a
    r = xm[:, :rw]
    k = xm[:, rw:2 * rw]
    v = xm[:, 2 * rw:c0]
    lora_w = _mm(jnp.tanh(xm[:, c0:c1]), wup_ref[...])
    lora_a = _mm(xm[:, c1:c2], aup_ref[...])
    gate = _mm(jax.nn.sigmoid(xm[:, c2:]), gup_ref[...])

    qkv = _per_set(lambda s, x: _conv_chunk(gbuf.at[s], x, gconv_w_ref, cw),
                   p_ref[:, cfg.o_gdn_qkv:cfg.o_gdn_qkv + 3 * gw], nset)
    qkv = _silu(qkv)
    q_raw, k_raw = qkv[:, :gw], qkv[:, gw:2 * gw]
    ssq = _split_mm(jnp.concatenate([q_raw * q_raw, k_raw * k_raw], axis=0), gseg_ref[...])
    yield
    w_log = -_softplus(-(w0_ref[...] + lora_w)) - 0.5
    logw = -jnp.exp(w_log)
    rcum = _cumsum_rows(tri, logw)
    iclr = jax.nn.sigmoid(a0_ref[...] + lora_a)
    k2 = k * (1.0 + (iclr - 1.0) * ka_ref[...])
    kk_raw = k * kk_ref[...]
    kk_ssq = _split_mm(kk_raw * kk_raw, rseg_ref[...])
    bonus = _split_mm(r * k2 * rk_ref[...], rseg_ref[...])
    yield
    xbc = _per_set(lambda s, x: _conv_chunk(sbuf.at[s], x, sconv_w_ref, cw),
                   p_ref[:, cfg.o_ssm_xbc:cfg.o_ssm_xbc + sw + 2 * cfg.ssm_bc], nset)
    xbc = _silu(xbc + sconv_b_ref[...])
    k_all = k_raw * lax.rsqrt(ssq[rows:] + NORM_EPS)
    pm = jnp.exp(rcum)
    pinv = jnp.exp(-rcum)
    kkn = kk_raw * lax.rsqrt(kk_ssq + NORM_EPS)
    out.update(
        q=q_raw * lax.rsqrt(ssq[:rows] + NORM_EPS) * (cfg.head_dim ** -0.5), k=k_all, v=qkv[:, 2 * gw:],
        gzs=_silu(p_ref[:, cfg.o_gdn_z:cfg.o_gdn_z + gw]),
        cum=cum, sp=sp, beta=jax.nn.sigmoid(small), eg=jnp.exp(cum),
        xbc=xbc, szs=_silu(p_ref[:, cfg.o_ssm_z:cfg.o_ssm_z + sw]),
        at=-kkn * jnp.exp(rcum - logw), bt=kkn * iclr * pinv, rt=r * pm, kt_r=k2 * pinv, v_r=v, gate=gate,
        bonus=bonus, pm=pm,
        kt=jnp.concatenate([k_all[g * c:(g + 1) * c].T for g in range(nslot)], axis=0),
        cum_t=jnp.concatenate([cum[g * c:(g + 1) * c].T for g in range(nslot)], axis=0),
        sp_t=jnp.concatenate([sp[g * c:(g + 1) * c].T for g in range(nslot)], axis=0),
        xt=jnp.concatenate([xbc[g * c:(g + 1) * c, :sw].T for g in range(nslot)], axis=0),
        vt_r=jnp.concatenate([v[g * c:(g + 1) * c].T for g in range(nslot)], axis=0),
    )


def _prelude_shapes(cfg, rows):
    nslot = rows // CHUNK
    gw, sw, rw = cfg.gdn_width, cfg.ssm_width, cfg.rwkv_width
    shapes = {name: (rows, gw) for name in ("q", "k", "v", "gzs")}
    shapes.update({name: (rows, LANE) for name in ("cum", "sp", "beta", "eg")})
    shapes.update(xbc=(rows, sw + 2 * cfg.ssm_bc), szs=(rows, sw))
    shapes.update({name: (rows, rw) for name in ("at", "bt", "rt", "kt_r", "v_r", "gate", "bonus", "pm")})
    shapes.update(kt=(nslot * gw, CHUNK), cum_t=(nslot * LANE, CHUNK), sp_t=(nslot * LANE, CHUNK),
                  xt=(nslot * sw, CHUNK), vt_r=(nslot * rw, CHUNK))
    return shapes


def _param_layout(cfg):
    gw, sw, rw, hd = cfg.gdn_width, cfg.ssm_width, cfg.rwkv_width, cfg.head_dim
    vec_items = [("gconv_w", cfg.conv_w, 3 * gw), ("alog", 1, LANE), ("dtb", 1, LANE), ("gnorm", 1, hd),
                 ("sconv_w", cfg.conv_w, sw + 2 * cfg.ssm_bc), ("sconv_b", 1, sw + 2 * cfg.ssm_bc),
                 ("sd", 1, cfg.ssm_heads), ("snorm", 1, sw), ("mu", 1, cfg.rwkv_cols)]
    vec_items += [(name, 1, rw) for name in ("w0", "a0", "kk", "ka", "rk", "lnw", "lnb")]
    mat_items = [("wup", cfg.lora_w, rw), ("aup", cfg.lora_a, rw), ("gup", cfg.lora_g, rw),
                 ("gseg", gw, gw), ("rseg", rw, rw)]
    layouts = []
    for items in (vec_items, mat_items):
        layout, row = {}, 0
        for name, nrows, width in items:
            layout[name] = (row, nrows, width)
            row += -(-nrows // HIST) * HIST
        layouts.append(layout)
    return layouts


def _pack(layout, arrays):
    width = max(w for _, _, w in layout.values())
    parts = []
    for name, (_, nrows, w) in layout.items():
        a = arrays[name].astype(F32).reshape(nrows, w)
        parts.append(jnp.pad(a, ((0, -nrows % HIST), (0, width - w))))
    return jnp.concatenate(parts, axis=0)


def _mixer_body(p_ref, gdn0_ref, gdnc0_ref, ssm0_ref, ssmc0_ref, rwkv0_ref, shift0_ref, vec_ref, mat_ref,
                mix_ref, gdn_out_ref, gdnc_out_ref, ssm_out_ref, ssmc_out_ref, rwkv_out_ref, shift_out_ref,
                gdn_s, ssm_s, rwkv_s, gbuf, sbuf, rbuf, *pre_refs, cfg, nset, steps, nblocks, names):
    vec_layout, mat_layout = _param_layout(cfg)

    def view(ref, layout, name):
        row, nrows, width = layout[name]
        return ref.at[row:row + nrows, 0:width]

    (gconv_w_ref, alog_ref, dtb_ref, gnorm_ref, sconv_w_ref, sconv_b_ref, sd_ref, snorm_ref, mu_ref,
     w0_ref, a0_ref, kk_ref, ka_ref, rk_ref, lnw_ref, lnb_ref) = (view(vec_ref, vec_layout, n) for n in vec_layout)
    wup_ref, aup_ref, gup_ref, gseg_ref, rseg_ref = (view(mat_ref, mat_layout, n) for n in mat_layout)
    pre = dict(zip(names, pre_refs))
    c = CHUNK
    rows = p_ref.shape[0]
    nslot = rows // c
    per_set = nslot // nset
    slot_plan = [(s, j) for s in range(nset) for j in range(per_set)]
    cw = cfg.conv_w
    gh, sh = cfg.gdn_heads, cfg.ssm_heads
    t = pl.program_id(0)
    p_block = jnp.minimum(t, nblocks - 1)
    c_block = jnp.maximum(t - 1, 0)

    @pl.when(t == 0)
    def _clear():
        for ref in (gdn_s, ssm_s, rwkv_s) + tuple(pre_refs):
            ref[...] = jnp.zeros(ref.shape, ref.dtype)

    @pl.when(p_block % steps == 0)
    def _load_history():
        gbuf[:, HIST - (cw - 1):HIST, :] = gdnc0_ref[...]
        sbuf[:, HIST - (cw - 1):HIST, :] = ssmc0_ref[...]
        rbuf[:, HIST - 1:HIST, :] = shift0_ref[...]

    @pl.when(jnp.logical_and(t >= 1, c_block % steps == 0))
    def _load_state():
        gdn_s[...] = gdn0_ref[...]
        ssm_s[...] = ssm0_ref[...]
        rwkv_s[...] = rwkv0_ref[...]

    ri = lax.broadcasted_iota(jnp.int32, (c, c), 0)
    ci = lax.broadcasted_iota(jnp.int32, (c, c), 1)
    eye_f = (ri == ci).astype(F32)
    offdiag_f = 1.0 - eye_f
    causal_bias = jnp.where(ri >= ci, 0.0, MASKED_EXPONENT)
    ri2 = lax.broadcasted_iota(jnp.int32, (c, 2 * c), 0)
    ci2 = lax.broadcasted_iota(jnp.int32, (c, 2 * c), 1)
    ci2 = jnp.where(ci2 >= c, ci2 - c, ci2)
    strict2 = ri2 > ci2
    incl2 = ri2 >= ci2
    masks = (eye_f, causal_bias, offdiag_f, strict2)

    gnorm, sd, ln_w, ln_b = gnorm_ref[...], sd_ref[...], lnw_ref[...], lnb_ref[...]
    gdn_states = [[_State(gdn_s, (s, h)) for h in range(gh)] for s in range(nset)]
    ssm_states = [[_State(ssm_s, (s, h)) for h in range(sh)] for s in range(nset)]
    rwkv_states = [[_State(rwkv_s, (s, h)) for h in range(cfg.rwkv_heads)] for s in range(nset)]
    gdn_o = [[None] * gh for _ in slot_plan]
    ssd_y = [[None] * sh for _ in slot_plan]
    rwkv_o = [[None] * cfg.rwkv_heads for _ in slot_plan]

    new_pre = {}
    prm = (gconv_w_ref, alog_ref, dtb_ref, sconv_w_ref, sconv_b_ref, mu_ref, w0_ref, wup_ref, a0_ref, aup_ref,
           gup_ref, kk_ref, ka_ref, rk_ref, gseg_ref, rseg_ref)
    tasks = []
    for g, (s, j) in enumerate(slot_plan):
        rs = slice(g * c, (g + 1) * c)
        delay = j * SLOT_DELAY
        tasks += [_gdn_head(cfg, h, j, delay, rs, pre, gdn_states[s][h], gnorm, masks, gdn_o[g]) for h in range(gh)]
        tasks += [_rwkv_head(cfg, h, j, delay, rs, pre, rwkv_states[s][h], ln_w, ln_b, masks, incl2, rwkv_o[g])
                  for h in range(cfg.rwkv_heads)]
        cb_cache = {}
        tasks += [_ssd_head(cfg, h, j, delay + h, rs, pre, ssm_states[s][h], sd, masks, cb_cache, ssd_y[g])
                  for h in range(sh)]
    tasks.append(_prelude(cfg, nset, p_ref, (gbuf, sbuf, rbuf), prm, new_pre))
    _run_interleaved(tasks)

    ng = cfg.ssm_groups
    gw = cfg.ssm_width // ng
    snorm = snorm_ref[...]
    for g in range(nslot):
        rs = slice(g * c, (g + 1) * c)
        ssd_o = []
        for grp in range(ng):
            yg = jnp.concatenate(ssd_y[g][grp * (sh // ng):(grp + 1) * (sh // ng)], axis=-1)
            yg = yg * pre["szs"][rs, grp * gw:(grp + 1) * gw]
            yg = yg * lax.rsqrt(jnp.mean(yg * yg, axis=-1, keepdims=True) + NORM_EPS)
            ssd_o.append(yg * snorm[:, grp * gw:(grp + 1) * gw])
        mix_ref[rs, :] = jnp.concatenate(gdn_o[g] + ssd_o + rwkv_o[g], axis=-1).astype(mix_ref.dtype)

    gdn_out_ref[...] = gdn_s[...]
    ssm_out_ref[...] = ssm_s[...]
    rwkv_out_ref[...] = rwkv_s[...]
    gdnc_out_ref[...] = gbuf[:, HIST - (cw - 1):HIST, :]
    ssmc_out_ref[...] = sbuf[:, HIST - (cw - 1):HIST, :]
    shift_out_ref[...] = rbuf[:, HIST - 1:HIST, :]
    for name in names:
        pre[name][...] = new_pre[name]


def _mixer_call(cfg, proj, row0, init, layer, chunks_per_seq, vecs, mats):
    nseq = init[0].shape[1]
    if chunks_per_seq % SLOTS == 0:
        nset, per_set = 1, SLOTS
    elif chunks_per_seq == 1 and nseq % SLOTS == 0:
        nset, per_set = SLOTS, 1
    else:
        nset, per_set = 1, 1
    rows = nset * per_set * CHUNK
    steps = chunks_per_seq // per_set
    assert row0 % rows == 0 and nseq % nset == 0
    blk0 = row0 // rows
    nblocks = (nseq // nset) * steps

    def p_block(t):
        return jnp.minimum(t, nblocks - 1)

    def c_block(t):
        return jnp.maximum(t - 1, 0)

    def in_spec(st, block_of):
        nd = st.ndim
        return pl.BlockSpec((None, nset) + st.shape[2:], lambda t: (layer, block_of(t) // steps) + (0,) * (nd - 2))

    def out_spec(st, block_of):
        nd = st.ndim - 1
        return pl.BlockSpec((nset,) + st.shape[2:], lambda t: (block_of(t) // steps,) + (0,) * (nd - 1))

    gdn0, gdnc0, ssm0, ssmc0, rwkv0, shift0 = init
    side = (c_block, p_block, c_block, p_block, c_block, p_block)
    in_specs = ([pl.BlockSpec((rows, proj.shape[1]), lambda t: (blk0 + p_block(t), 0))]
                + [in_spec(st, blk) for st, blk in zip(init, side)]
                + [pl.BlockSpec(p.shape, lambda t: (0, 0)) for p in (vecs, mats)])
    out_specs = ([pl.BlockSpec((rows, cfg.mix_width), lambda t: (c_block(t), 0))]
                 + [out_spec(st, blk) for st, blk in zip(init, side)])
    out_shape = [jax.ShapeDtypeStruct((nseq * chunks_per_seq * CHUNK, cfg.mix_width), BF16)] + [
        jax.ShapeDtypeStruct(st.shape[1:], F32) for st in init]
    shapes = _prelude_shapes(cfg, rows)
    names = tuple(shapes)
    scratch = [
        pltpu.VMEM((nset,) + gdn0.shape[2:], F32),
        pltpu.VMEM((nset,) + ssm0.shape[2:], F32),
        pltpu.VMEM((nset,) + rwkv0.shape[2:], F32),
        pltpu.VMEM((nset, HIST + per_set * CHUNK, gdnc0.shape[3]), F32),
        pltpu.VMEM((nset, HIST + per_set * CHUNK, ssmc0.shape[3]), F32),
        pltpu.VMEM((nset, HIST + per_set * CHUNK, shift0.shape[3]), F32),
    ] + [pltpu.VMEM(shapes[name], F32) for name in names]
    return pl.pallas_call(
        functools.partial(_mixer_body, cfg=cfg, nset=nset, steps=steps, nblocks=nblocks, names=names),
        grid=(nblocks + 1,),
        in_specs=in_specs,
        out_specs=out_specs,
        out_shape=out_shape,
        scratch_shapes=scratch,
        compiler_params=pltpu.CompilerParams(
            dimension_semantics=("arbitrary",), vmem_limit_bytes=VMEM_LIMIT_BYTES),
        name="mixers",
    )(proj, *init, vecs, mats)


def _pad_lanes(v, width=LANE):
    v = v.reshape(1, -1)
    return jnp.pad(v, ((0, 0), (0, width - v.shape[1])))


def kernel(x_prompt, x_sample, state_gdn, state_gdn_conv, state_ssm, state_ssm_conv, state_rwkv, state_rwkv_shift, norm1_w, w_in, gdn_conv_w, gdn_A_log, gdn_dt_bias, gdn_norm_w, ssm_conv_w, ssm_conv_b, ssm_A_log, ssm_dt_bias, ssm_D, ssm_norm_w, rwkv_mu, rwkv_w0, rwkv_w_up, rwkv_a0, rwkv_a_up, rwkv_g_up, rwkv_k_k, rwkv_k_a, rwkv_r_k, rwkv_ln_w, rwkv_ln_b, w_out, norm2_w, ffn_w_gate, ffn_w_up, ffn_w_down, final_norm_w):
    depth = w_in.shape[0]
    nbp, tp, d = x_prompt.shape
    nbs, ts, _ = x_sample.shape
    hd = state_gdn.shape[-1]
    ssm_width = state_ssm.shape[2] * hd
    cfg = Cfg(
        d_model=d, head_dim=hd, conv_w=gdn_conv_w.shape[1],
        gdn_heads=state_gdn.shape[2], ssm_heads=state_ssm.shape[2],
        ssm_groups=(ssm_conv_w.shape[2] - ssm_width) // (2 * state_ssm.shape[-1]),
        ssm_state=state_ssm.shape[-1], rwkv_heads=state_rwkv.shape[2],
        lora_w=rwkv_w_up.shape[1], lora_a=rwkv_a_up.shape[1], lora_g=rwkv_g_up.shape[1])
    assert tp % CHUNK == 0 and ts % CHUNK == 0 and tp >= cfg.conv_w and ts >= cfg.conv_w
    assert 2 * cfg.gdn_heads + cfg.ssm_heads <= LANE
    gw, sw = cfg.gdn_width, cfg.ssm_width
    gdn_cols = 4 * gw + 2 * cfg.gdn_heads
    ssm_cols = 2 * sw + 2 * cfg.ssm_bc + cfg.ssm_heads
    assert w_in.shape[2] == gdn_cols + ssm_cols + cfg.rwkv_cols

    x_parts = [x_prompt.reshape(nbp * tp, d), x_sample.reshape(nbs * ts, d)]

    w_in_t = jnp.swapaxes(w_in, 1, 2)

    def small_row(l, gdn_first, gdn_second, ssm_part):
        return _pad_lanes(jnp.concatenate([gdn_first[l], gdn_second, ssm_part[l]]))

    zeros_g = jnp.zeros((cfg.gdn_heads,), F32)
    head_of = jnp.arange(gw) // hd
    gdn_seg = (head_of[:, None] == head_of[None, :]).astype(BF16)
    head_of = jnp.arange(cfg.rwkv_width) // hd
    rwkv_seg = (head_of[:, None] == head_of[None, :]).astype(BF16)
    wo_b, wg_b, wu_b, wd_b = (w.astype(BF16) for w in (w_out, ffn_w_gate, ffn_w_up, ffn_w_down))
    sample_states = (state_gdn, state_gdn_conv, state_ssm, state_ssm_conv, state_rwkv, state_rwkv_shift)

    p_init = tuple(jnp.zeros((1, nbp) + st.shape[2:], F32) for st in sample_states)
    s_init = tuple(st.astype(F32) for st in sample_states)
    p_states = [[] for _ in sample_states]
    s_states = [[] for _ in sample_states]
    for l in range(depth):
        proj = _proj_call(cfg, x_parts, norm1_w[l], w_in_t, l)
        params = dict(
            gconv_w=gdn_conv_w[l], alog=small_row(l, gdn_A_log, zeros_g, ssm_A_log),
            dtb=small_row(l, gdn_dt_bias, zeros_g, ssm_dt_bias), gnorm=gdn_norm_w[l],
            sconv_w=ssm_conv_w[l], sconv_b=ssm_conv_b[l], sd=ssm_D[l], snorm=ssm_norm_w[l],
            mu=rwkv_mu[l], w0=rwkv_w0[l], a0=rwkv_a0[l], kk=rwkv_k_k[l], ka=rwkv_k_a[l], rk=rwkv_r_k[l],
            lnw=rwkv_ln_w[l], lnb=rwkv_ln_b[l],
            wup=rwkv_w_up[l], aup=rwkv_a_up[l], gup=rwkv_g_up[l], gseg=gdn_seg, rseg=rwkv_seg)
        vecs, mats = (_pack(layout, params) for layout in _param_layout(cfg))
        mix_p, *p_new = _mixer_call(cfg, proj, 0, p_init, 0, tp // CHUNK, vecs, mats)
        mix_s, *s_new = _mixer_call(cfg, proj, nbp * tp, s_init, l, ts // CHUNK, vecs, mats)
        final = l == depth - 1
        out_rows = [nbp * tp, nbs * ts] if final else [nbp * tp + nbs * ts]
        x_parts = _ffn_call(x_parts, [mix_p, mix_s], wo_b, norm2_w[l], wg_b, wu_b, wd_b, final_norm_w, l,
                            out_rows, final)
        for acc, st in zip(p_states, p_new):
            acc.append(st)
        for acc, st in zip(s_states, s_new):
            acc.append(st)

    y_prompt = x_parts[0].reshape(nbp, tp, d)
    y_sample = x_parts[1].reshape(nbs, ts, d)
    return (y_prompt, y_sample, *(jnp.stack(st) for st in p_states), *(jnp.stack(st) for st in s_states))
```

```python
import functools
import math
from typing import NamedTuple

import jax
import jax.numpy as jnp
from jax import lax
from jax.experimental import pallas as pl
from jax.experimental.pallas import tpu as pltpu

F32 = jnp.float32
BF16 = jnp.bfloat16

CHUNK = 64
SLOTS = 2
PRELUDE_DELAY = 5
SLOT_DELAY = 4
NORM_EPS = 1e-6
RWKV_GN_EPS = 64e-5
MASKED_EXPONENT = -1e30
LANE = 128
HIST = 8
VMEM_LIMIT_BYTES = 56 * 1024 * 1024
PROJ_ROWS = 512
FFN_ROWS = 512
FFN_COLS = 256


class Cfg(NamedTuple):
    d_model: int
    head_dim: int
    conv_w: int
    gdn_heads: int
    ssm_heads: int
    ssm_groups: int
    ssm_state: int
    rwkv_heads: int
    lora_w: int
    lora_a: int
    lora_g: int

    @property
    def gdn_width(self):
        return self.gdn_heads * self.head_dim

    @property
    def ssm_width(self):
        return self.ssm_heads * self.head_dim

    @property
    def ssm_bc(self):
        return self.ssm_groups * self.ssm_state

    @property
    def rwkv_width(self):
        return self.rwkv_heads * self.head_dim

    @property
    def rwkv_cols(self):
        return 3 * self.rwkv_width + self.lora_w + self.lora_a + self.lora_g

    @property
    def o_gdn_qkv(self):
        return 0

    @property
    def o_gdn_z(self):
        return 3 * self.gdn_width

    @property
    def o_ssm_z(self):
        return self.o_gdn_z + self.gdn_width

    @property
    def o_ssm_xbc(self):
        return self.o_ssm_z + self.ssm_width

    @property
    def o_rwkv(self):
        return self.o_ssm_xbc + self.ssm_width + 2 * self.ssm_bc

    @property
    def o_small(self):
        return self.o_rwkv + self.rwkv_cols

    @property
    def proj_cols(self):
        return self.o_small + LANE

    @property
    def mix_width(self):
        return self.gdn_width + self.ssm_width + self.rwkv_width


def _rms(x, w):
    return x * lax.rsqrt(jnp.mean(x * x, axis=-1, keepdims=True) + NORM_EPS) * w


def _softplus(x):
    return jnp.maximum(x, 0.0) + jnp.log1p(jnp.exp(-jnp.abs(x)))


def _silu(x):
    return x * jax.nn.sigmoid(x)


def _mm(a, b):
    return jnp.dot(a.astype(BF16), b.astype(BF16), preferred_element_type=F32)


def _mm_nt(a, b):
    return lax.dot_general(a.astype(BF16), b.astype(BF16), (((1,), (1,)), ((), ())),
                           preferred_element_type=F32)


def _split_mm(x, m):
    hi = x.astype(BF16)
    lo = (x - hi.astype(F32)).astype(BF16)
    m = m.astype(BF16)
    return jnp.dot(hi, m, preferred_element_type=F32) + jnp.dot(lo, m, preferred_element_type=F32)


def _cumsum_rows(tri, x):
    hi = x.astype(BF16)
    lo = (x - hi.astype(F32)).astype(BF16)
    return jnp.dot(tri, hi, preferred_element_type=F32) + jnp.dot(tri, lo, preferred_element_type=F32)


def _inv_one_minus_steps(n, eye_f):
    c = n.shape[0]
    t = eye_f + n
    p = _mm(n, n)
    yield
    for _ in range(int(math.log2(c)) - 2):
        step = _mm(t, p)
        p_next = _mm(p, p)
        yield
        t = t + step
        p = p_next
    step = _mm(t, p)
    yield
    return t + step


def _run_interleaved(tasks):
    tasks = list(tasks)
    while tasks:
        alive = []
        for task in tasks:
            try:
                spawned = next(task)
            except StopIteration:
                continue
            alive.append(task)
            if spawned:
                alive.extend(spawned)
        tasks = alive


class _State:
    def __init__(self, ref, index):
        self.ref, self.index, self.version = ref, index, 0

    def read(self, slot):
        assert self.version == slot, "chunk slot reads a state the previous slot has not written yet"
        return self.ref[self.index]

    def write(self, slot, value):
        assert self.version == slot
        self.ref[self.index] = value
        self.version += 1


def _conv_chunk(buf_ref, x, w_ref, conv_w):
    c = x.shape[0]
    buf_ref[HIST:HIST + c, :] = x
    y = x * w_ref[conv_w - 1:conv_w, :]
    for j in range(conv_w - 1):
        lo = HIST - (conv_w - 1) + j
        y = y + buf_ref[lo:lo + c, :] * w_ref[j:j + 1, :]
    tail = buf_ref[HIST + c - (conv_w - 1):HIST + c, :]
    buf_ref[HIST - (conv_w - 1):HIST, :] = tail
    return y


def _per_set(fn, x, nset):
    rows = x.shape[0] // nset
    return jnp.concatenate([fn(s, x[s * rows:(s + 1) * rows]) for s in range(nset)], axis=0)


def _tile_bounds(parts, rows):
    bounds, lo = [], 0
    for p in parts:
        assert p.shape[0] % rows == 0
        bounds.append((lo, lo + p.shape[0] // rows))
        lo = bounds[-1][1]
    return bounds


def _part_specs(parts, rows):
    return [pl.BlockSpec((rows, p.shape[1]), lambda i, lo=lo, hi=hi: (jnp.clip(i - lo, 0, hi - lo - 1), 0))
            for p, (lo, hi) in zip(parts, _tile_bounds(parts, rows))]


def _read_part(refs, bounds):
    i = pl.program_id(0)
    x = refs[-1][...]
    for ref, (_, hi) in reversed(list(zip(refs[:-1], bounds[:-1]))):
        x = jnp.where(i < hi, ref[...], x)
    return x


def _write_part(refs, bounds, value):
    i = pl.program_id(0)
    if len(refs) == 1:
        refs[0][...] = value
        return
    for ref, (lo, hi) in zip(refs, bounds):
        @pl.when(jnp.logical_and(i >= lo, i < hi))
        def _(ref=ref):
            ref[...] = value


def _proj_body(*refs, bounds, segments, small_cols):
    x_refs, (nw_ref, w_ref, o_ref, w_scr) = refs[:len(bounds)], refs[len(bounds):]

    @pl.when(pl.program_id(0) == 0)
    def _regroup():
        dst = 0
        for src, width in segments:
            w_scr[dst:dst + width, :] = w_ref[src:src + width, :].astype(BF16)
            dst += width
        small = jnp.concatenate([w_ref[src:src + width, :] for src, width in small_cols], axis=0)
        small = jnp.concatenate([small, jnp.zeros((LANE - small.shape[0], small.shape[1]), F32)], axis=0)
        w_scr[dst:dst + LANE, :] = small.astype(BF16)

    h = _rms(_read_part(x_refs, bounds), nw_ref[...])
    o_ref[...] = _mm_nt(h, w_scr[...])


def _proj_call(cfg, x_parts, norm_w, w_in, layer):
    d = x_parts[0].shape[1]
    n = sum(p.shape[0] for p in x_parts)
    rows = PROJ_ROWS
    gw, sw = cfg.gdn_width, cfg.ssm_width
    gdn_cols = 4 * gw + 2 * cfg.gdn_heads
    ssm_cols = 2 * sw + 2 * cfg.ssm_bc + cfg.ssm_heads
    assert w_in.shape[1] == gdn_cols + ssm_cols + cfg.rwkv_cols
    segments = ((0, 4 * gw), (gdn_cols, 2 * sw + 2 * cfg.ssm_bc), (gdn_cols + ssm_cols, cfg.rwkv_cols))
    small_cols = ((4 * gw, 2 * cfg.gdn_heads), (gdn_cols + 2 * sw + 2 * cfg.ssm_bc, cfg.ssm_heads))
    cols = cfg.proj_cols
    return pl.pallas_call(
        functools.partial(_proj_body, bounds=_tile_bounds(x_parts, rows), segments=segments, small_cols=small_cols),
        grid=(n // rows,),
        in_specs=_part_specs(x_parts, rows) + [
            pl.BlockSpec((1, d), lambda i: (0, 0)),
            pl.BlockSpec((None,) + w_in.shape[1:], lambda i: (layer, 0, 0), pipeline_mode=pl.Buffered(1)),
        ],
        out_specs=pl.BlockSpec((rows, cols), lambda i: (i, 0)),
        out_shape=jax.ShapeDtypeStruct((n, cols), F32),
        scratch_shapes=[pltpu.VMEM((cols, d), BF16)],
        compiler_params=pltpu.CompilerParams(
            dimension_semantics=("arbitrary",), vmem_limit_bytes=VMEM_LIMIT_BYTES),
        name="norm_proj",
    )(*x_parts, norm_w.reshape(1, d), w_in)


def _ffn_body(*refs, in_bounds, mix_bounds, out_bounds, final):
    nx, nm = len(in_bounds), len(mix_bounds)
    x_refs, mix_refs, refs = refs[:nx], refs[nx:nx + nm], refs[nx + nm:]
    (wo_ref, n2_ref, wg_ref, wu_ref, wd_ref, fn_ref), o_refs = refs[:6], refs[6:]
    mix = _read_part(mix_refs, mix_bounds)
    x = _read_part(x_refs, in_bounds) + jnp.dot(mix, wo_ref[...], preferred_element_type=F32)
    h2 = _rms(x, n2_ref[...]).astype(BF16)
    hidden = wg_ref.shape[1]
    acc = x
    for c0 in range(0, hidden, FFN_COLS):
        g = jnp.dot(h2, wg_ref[:, c0:c0 + FFN_COLS], preferred_element_type=F32)
        u = jnp.dot(h2, wu_ref[:, c0:c0 + FFN_COLS], preferred_element_type=F32)
        ff = (_silu(g) * u).astype(BF16)
        acc = acc + jnp.dot(ff, wd_ref[c0:c0 + FFN_COLS, :], preferred_element_type=F32)
    if final:
        acc = _rms(acc, fn_ref[...])
    _write_part(o_refs, out_bounds, acc)


def _ffn_call(x_parts, mix_parts, wo, n2, wg, wu, wd, fn, layer, out_rows, final):
    d = x_parts[0].shape[1]
    n = sum(p.shape[0] for p in x_parts)
    hidden = wg.shape[2]
    rows = FFN_ROWS
    assert n % rows == 0 and hidden % FFN_COLS == 0 and sum(out_rows) == n
    const = lambda i: (0, 0)

    def layer_spec(w):
        return pl.BlockSpec((None,) + w.shape[1:], lambda i: (layer, 0, 0))

    out_shape = [jax.ShapeDtypeStruct((r, d), F32) for r in out_rows]
    return pl.pallas_call(
        functools.partial(_ffn_body, in_bounds=_tile_bounds(x_parts, rows),
                          mix_bounds=_tile_bounds(mix_parts, rows),
                          out_bounds=_tile_bounds(out_shape, rows), final=final),
        grid=(n // rows,),
        in_specs=_part_specs(x_parts, rows) + _part_specs(mix_parts, rows) + [
            layer_spec(wo),
            pl.BlockSpec((1, d), const),
            layer_spec(wg),
            layer_spec(wu),
            layer_spec(wd),
            pl.BlockSpec((1, d), const),
        ],
        out_specs=_part_specs(out_shape, rows),
        out_shape=out_shape,
        compiler_params=pltpu.CompilerParams(
            dimension_semantics=("arbitrary",), vmem_limit_bytes=VMEM_LIMIT_BYTES),
        name="outproj_ffn",
    )(*x_parts, *mix_parts, wo, n2.reshape(1, d), wg, wu, wd, fn.reshape(1, d))


def _gdn_head(cfg, h, slot, delay, rs, pre, state, norm_w, masks, outs):
    eye_f, causal_bias, offdiag_f, _ = masks
    hd, gw = cfg.head_dim, cfg.gdn_width
    c = CHUNK
    g = rs.start // c
    for _ in range(delay):
        yield
    sl = slice(h * hd, (h + 1) * hd)
    q = pre["q"][rs, sl]
    k = pre["k"][rs, sl]
    kt = pre["kt"][g * gw + h * hd:g * gw + (h + 1) * hd, :]
    b = pre["beta"][rs, cfg.gdn_heads + h:cfg.gdn_heads + h + 1]
    kb = k * b
    vb = pre["v"][rs, sl] * b
    kq = _mm(jnp.concatenate([kb, q], axis=0), kt)
    yield
    gcol = pre["cum"][rs, h:h + 1]
    grow = pre["cum_t"][g * LANE + h:g * LANE + h + 1, :]
    dec = jnp.exp(gcol - grow + causal_bias)
    lower = kq[:c] * (dec * offdiag_f)
    attn = kq[c:] * dec
    t = yield from _inv_one_minus_steps(-lower, eye_f)
    eg = pre["eg"][rs, h:h + 1]
    u = _mm(t, vb)
    w = _mm(t, kb * eg)
    yield
    s = state.read(slot)
    wq_s = _mm(jnp.concatenate([w, q * eg], axis=0), s)
    yield
    g_last = gcol[c - 1:c, :]
    v_new = u - wq_s[:c]
    o = wq_s[c:] + _mm(attn, v_new)
    state.write(slot, s * jnp.exp(g_last) + _mm(kt * jnp.exp(g_last - grow), v_new))
    yield
    outs[h] = _rms(o, norm_w) * pre["gzs"][rs, sl]


def _ssd_head(cfg, h, slot, delay, rs, pre, state, d_row, masks, cb_cache, ys):
    _, causal_bias, _, _ = masks
    hd, ns, width = cfg.head_dim, cfg.ssm_state, cfg.ssm_width
    gh = cfg.gdn_heads
    c = CHUNK
    g = rs.start // c
    for _ in range(delay):
        yield
    grp = h // (cfg.ssm_heads // cfg.ssm_groups)
    bm = pre["xbc"][rs, width + grp * ns:width + (grp + 1) * ns]
    cm = pre["xbc"][rs, width + cfg.ssm_bc + grp * ns:width + cfg.ssm_bc + (grp + 1) * ns]
    if grp not in cb_cache:
        cb_cache[grp] = _mm_nt(cm, bm)
    lane = 2 * gh + h
    acol = pre["cum"][rs, lane:lane + 1]
    arow = pre["cum_t"][g * LANE + lane:g * LANE + lane + 1, :]
    a_last = acol[c - 1:c, :]
    x = pre["xbc"][rs, h * hd:(h + 1) * hd]
    xdt = x * pre["sp"][rs, lane:lane + 1]
    s = state.read(slot)
    y_off = _mm_nt(cm, s)
    dt_row = pre["sp_t"][g * LANE + lane:g * LANE + lane + 1, :]
    xt_dec = pre["xt"][g * width + h * hd:g * width + (h + 1) * hd, :] * (dt_row * jnp.exp(a_last - arow))
    state.write(slot, s * jnp.exp(a_last) + _mm(xt_dec, bm))
    yield
    lmat = jnp.exp(acol - arow + causal_bias)
    y_diag = _mm(cb_cache[grp] * lmat, xdt)
    yield
    ys[h] = y_diag + y_off * jnp.exp(acol) + d_row[:, h:h + 1] * x


def _rwkv_head(cfg, h, slot, delay, rs, pre, state, ln_w, ln_b, masks, incl2, outs):
    eye_f, _, _, strict2 = masks
    hd, rw = cfg.head_dim, cfg.rwkv_width
    c = CHUNK
    g = rs.start // c
    for _ in range(delay):
        yield
    sl = slice(h * hd, (h + 1) * hd)
    at = pre["at"][rs, sl]
    bt = pre["bt"][rs, sl]
    kt = pre["kt_r"][rs, sl]
    vh = pre["v_r"][rs, sl]
    pch = pre["pm"][rs.stop - 1:rs.stop, sl]
    ar = jnp.concatenate([at, pre["rt"][rs, sl]], axis=0)
    bk = jnp.concatenate([bt, kt], axis=0)
    cross = _mm_nt(ar, bk)
    vk = _mm(pre["vt_r"][g * rw + h * hd:g * rw + (h + 1) * hd, :], kt * pch)
    yield
    a_abk = jnp.where(strict2, cross[:c], 0.0)
    a_ab = a_abk[:, :c]
    aakv = _mm(a_abk[:, c:], vh)
    t = yield from _inv_one_minus_steps(a_ab, eye_f)
    s = state.read(slot)
    ar_s = _mm_nt(ar, s)
    yield
    u = _mm(t, ar_s[:c] + aakv)
    yield
    uv = jnp.concatenate([u, vh], axis=0)
    y = ar_s[c:] + _mm(jnp.where(incl2, cross[c:], 0.0), uv)
    ut = _mm_nt(eye_f, u)
    yield
    state.write(slot, s * pch + _mm(ut, bt * pch) + vk)
    yield
    mean = jnp.mean(y, axis=-1, keepdims=True)
    yc = y - mean
    var = jnp.mean(yc * yc, axis=-1, keepdims=True)
    yn = yc * lax.rsqrt(var + RWKV_GN_EPS) * ln_w[:, sl] + ln_b[:, sl]
    outs[h] = (yn + pre["bonus"][rs, sl] * vh) * pre["gate"][rs, sl]


def _prelude(cfg, nset, p_ref, bufs, prm, out):
    (gconv_w_ref, alog_ref, dtb_ref, sconv_w_ref, sconv_b_ref, mu_ref, w0_ref, wup_ref, a0_ref, aup_ref,
     gup_ref, kk_ref, ka_ref, rk_ref, gseg_ref, rseg_ref) = prm
    gbuf, sbuf, rbuf = bufs
    c = CHUNK
    rows = p_ref.shape[0]
    nslot = rows // c
    cw = cfg.conv_w
    gw, sw, rw = cfg.gdn_width, cfg.ssm_width, cfg.rwkv_width
    for _ in range(PRELUDE_DELAY):
        yield

    rr = lax.broadcasted_iota(jnp.int32, (rows, rows), 0)
    cc = lax.broadcasted_iota(jnp.int32, (rows, rows), 1)
    same_chunk = functools.reduce(jnp.logical_and, [(rr >= m * c) == (cc >= m * c) for m in range(1, nslot)],
                                  rr >= 0)
    tri = jnp.logical_and(same_chunk, rr >= cc).astype(BF16)

    small = p_ref[:, cfg.o_small:cfg.o_small + LANE]
    sp = _softplus(small + dtb_ref[...])
    cum = _cumsum_rows(tri, sp * (-jnp.exp(alog_ref[...])))

    def shift(s, pr):
        n = pr.shape[0]
        rbuf[s, HIST:HIST + n, :] = pr
        prev = rbuf[s, HIST - 1:HIST - 1 + n, :]
        rbuf[s, HIST - 1:HIST, :] = pr[n - 1:n, :]
        return pr + (prev - pr) * mu_ref[...]

    xm = _per_set(shift, p_ref[:, cfg.o_rwkv:cfg.o_rwkv + cfg.rwkv_cols], nset)
    c0 = 3 * rw
    c1 = c0 + cfg.lora_w
    c2 = c1 + cfg.lora_a
    r = xm[:, :rw]
    k = xm[:, rw:2 * rw]
    v = xm[:, 2 * rw:c0]
    lora_w = _mm(jnp.tanh(xm[:, c0:c1]), wup_ref[...])
    lora_a = _mm(xm[:, c1:c2], aup_ref[...])
    gate = _mm(jax.nn.sigmoid(xm[:, c2:]), gup_ref[...])

    qkv = _per_set(lambda s, x: _conv_chunk(gbuf.at[s], x, gconv_w_ref, cw),
                   p_ref[:, cfg.o_gdn_qkv:cfg.o_gdn_qkv + 3 * gw], nset)
    qkv = _silu(qkv)
    q_raw, k_raw = qkv[:, :gw], qkv[:, gw:2 * gw]
    ssq = _split_mm(jnp.concatenate([q_raw * q_raw, k_raw * k_raw], axis=0), gseg_ref[...])
    yield
    w_log = -_softplus(-(w0_ref[...] + lora_w)) - 0.5
    logw = -jnp.exp(w_log)
    rcum = _cumsum_rows(tri, logw)
    iclr = jax.nn.sigmoid(a0_ref[...] + lora_a)
    k2 = k * (1.0 + (iclr - 1.0) * ka_ref[...])
    kk_raw = k * kk_ref[...]
    kk_ssq = _split_mm(kk_raw * kk_raw, rseg_ref[...])
    bonus = _split_mm(r * k2 * rk_ref[...], rseg_ref[...])
    yield
    xbc = _per_set(lambda s, x: _conv_chunk(sbuf.at[s], x, sconv_w_ref, cw),
                   p_ref[:, cfg.o_ssm_xbc:cfg.o_ssm_xbc + sw + 2 * cfg.ssm_bc], nset)
    xbc = _silu(xbc + sconv_b_ref[...])
    k_all = k_raw * lax.rsqrt(ssq[rows:] + NORM_EPS)
    pm = jnp.exp(rcum)
    pinv = jnp.exp(-rcum)
    kkn = kk_raw * lax.rsqrt(kk_ssq + NORM_EPS)
    out.update(
        q=q_raw * lax.rsqrt(ssq[:rows] + NORM_EPS) * (cfg.head_dim ** -0.5), k=k_all, v=qkv[:, 2 * gw:],
        gzs=_silu(p_ref[:, cfg.o_gdn_z:cfg.o_gdn_z + gw]),
        cum=cum, sp=sp, beta=jax.nn.sigmoid(small), eg=jnp.exp(cum),
        xbc=xbc, szs=_silu(p_ref[:, cfg.o_ssm_z:cfg.o_ssm_z + sw]),
        at=-kkn * jnp.exp(rcum - logw), bt=kkn * iclr * pinv, rt=r * pm, kt_r=k2 * pinv, v_r=v, gate=gate,
        bonus=bonus, pm=pm,
        kt=jnp.concatenate([k_all[g * c:(g + 1) * c].T for g in range(nslot)], axis=0),
        cum_t=jnp.concatenate([cum[g * c:(g + 1) * c].T for g in range(nslot)], axis=0),
        sp_t=jnp.concatenate([sp[g * c:(g + 1) * c].T for g in range(nslot)], axis=0),
        xt=jnp.concatenate([xbc[g * c:(g + 1) * c, :sw].T for g in range(nslot)], axis=0),
        vt_r=jnp.concatenate([v[g * c:(g + 1) * c].T for g in range(nslot)], axis=0),
    )


def _prelude_shapes(cfg, rows):
    nslot = rows // CHUNK
    gw, sw, rw = cfg.gdn_width, cfg.ssm_width, cfg.rwkv_width
    shapes = {name: (rows, gw) for name in ("q", "k", "v", "gzs")}
    shapes.update({name: (rows, LANE) for name in ("cum", "sp", "beta", "eg")})
    shapes.update(xbc=(rows, sw + 2 * cfg.ssm_bc), szs=(rows, sw))
    shapes.update({name: (rows, rw) for name in ("at", "bt", "rt", "kt_r", "v_r", "gate", "bonus", "pm")})
    shapes.update(kt=(nslot * gw, CHUNK), cum_t=(nslot * LANE, CHUNK), sp_t=(nslot * LANE, CHUNK),
                  xt=(nslot * sw, CHUNK), vt_r=(nslot * rw, CHUNK))
    return shapes


def _param_layout(cfg):
    gw, sw, rw, hd = cfg.gdn_width, cfg.ssm_width, cfg.rwkv_width, cfg.head_dim
    vec_items = [("gconv_w", cfg.conv_w, 3 * gw), ("alog", 1, LANE), ("dtb", 1, LANE), ("gnorm", 1, hd),
                 ("sconv_w", cfg.conv_w, sw + 2 * cfg.ssm_bc), ("sconv_b", 1, sw + 2 * cfg.ssm_bc),
                 ("sd", 1, cfg.ssm_heads), ("snorm", 1, sw), ("mu", 1, cfg.rwkv_cols)]
    vec_items += [(name, 1, rw) for name in ("w0", "a0", "kk", "ka", "rk", "lnw", "lnb")]
    mat_items = [("wup", cfg.lora_w, rw), ("aup", cfg.lora_a, rw), ("gup", cfg.lora_g, rw),
                 ("gseg", gw, gw), ("rseg", rw, rw)]
    layouts = []
    for items in (vec_items, mat_items):
        layout, row = {}, 0
        for name, nrows, width in items:
            layout[name] = (row, nrows, width)
            row += -(-nrows // HIST) * HIST
        layouts.append(layout)
    return layouts


def _pack(layout, arrays):
    width = max(w for _, _, w in layout.values())
    parts = []
    for name, (_, nrows, w) in layout.items():
        a = arrays[name].astype(F32).reshape(nrows, w)
        parts.append(jnp.pad(a, ((0, -nrows % HIST), (0, width - w))))
    return jnp.concatenate(parts, axis=0)


def _mixer_body(p_ref, gdn0_ref, gdnc0_ref, ssm0_ref, ssmc0_ref, rwkv0_ref, shift0_ref, vec_ref, mat_ref,
                mix_ref, gdn_out_ref, gdnc_out_ref, ssm_out_ref, ssmc_out_ref, rwkv_out_ref, shift_out_ref,
                gdn_s, ssm_s, rwkv_s, gbuf, sbuf, rbuf, *pre_refs, cfg, nset, steps, nblocks, names):
    vec_layout, mat_layout = _param_layout(cfg)

    def view(ref, layout, name):
        row, nrows, width = layout[name]
        return ref.at[row:row + nrows, 0:width]

    (gconv_w_ref, alog_ref, dtb_ref, gnorm_ref, sconv_w_ref, sconv_b_ref, sd_ref, snorm_ref, mu_ref,
     w0_ref, a0_ref, kk_ref, ka_ref, rk_ref, lnw_ref, lnb_ref) = (view(vec_ref, vec_layout, n) for n in vec_layout)
    wup_ref, aup_ref, gup_ref, gseg_ref, rseg_ref = (view(mat_ref, mat_layout, n) for n in mat_layout)
    pre = dict(zip(names, pre_refs))
    c = CHUNK
    rows = p_ref.shape[0]
    nslot = rows // c
    per_set = nslot // nset
    slot_plan = [(s, j) for s in range(nset) for j in range(per_set)]
    cw = cfg.conv_w
    gh, sh = cfg.gdn_heads, cfg.ssm_heads
    t = pl.program_id(0)
    p_block = jnp.minimum(t, nblocks - 1)
    c_block = jnp.maximum(t - 1, 0)

    @pl.when(t == 0)
    def _clear():
        for ref in (gdn_s, ssm_s, rwkv_s) + tuple(pre_refs):
            ref[...] = jnp.zeros(ref.shape, ref.dtype)

    @pl.when(p_block % steps == 0)
    def _load_history():
        gbuf[:, HIST - (cw - 1):HIST, :] = gdnc0_ref[...]
        sbuf[:, HIST - (cw - 1):HIST, :] = ssmc0_ref[...]
        rbuf[:, HIST - 1:HIST, :] = shift0_ref[...]

    @pl.when(jnp.logical_and(t >= 1, c_block % steps == 0))
    def _load_state():
        gdn_s[...] = gdn0_ref[...]
        ssm_s[...] = ssm0_ref[...]
        rwkv_s[...] = rwkv0_ref[...]

    ri = lax.broadcasted_iota(jnp.int32, (c, c), 0)
    ci = lax.broadcasted_iota(jnp.int32, (c, c), 1)
    eye_f = (ri == ci).astype(F32)
    offdiag_f = 1.0 - eye_f
    causal_bias = jnp.where(ri >= ci, 0.0, MASKED_EXPONENT)
    ri2 = lax.broadcasted_iota(jnp.int32, (c, 2 * c), 0)
    ci2 = lax.broadcasted_iota(jnp.int32, (c, 2 * c), 1)
    ci2 = jnp.where(ci2 >= c, ci2 - c, ci2)
    strict2 = ri2 > ci2
    incl2 = ri2 >= ci2
    masks = (eye_f, causal_bias, offdiag_f, strict2)

    gnorm, sd, ln_w, ln_b = gnorm_ref[...], sd_ref[...], lnw_ref[...], lnb_ref[...]
    gdn_states = [[_State(gdn_s, (s, h)) for h in range(gh)] for s in range(nset)]
    ssm_states = [[_State(ssm_s, (s, h)) for h in range(sh)] for s in range(nset)]
    rwkv_states = [[_State(rwkv_s, (s, h)) for h in range(cfg.rwkv_heads)] for s in range(nset)]
    gdn_o = [[None] * gh for _ in slot_plan]
    ssd_y = [[None] * sh for _ in slot_plan]
    rwkv_o = [[None] * cfg.rwkv_heads for _ in slot_plan]

    new_pre = {}
    prm = (gconv_w_ref, alog_ref, dtb_ref, sconv_w_ref, sconv_b_ref, mu_ref, w0_ref, wup_ref, a0_ref, aup_ref,
           gup_ref, kk_ref, ka_ref, rk_ref, gseg_ref, rseg_ref)
    tasks = []
    for g, (s, j) in enumerate(slot_plan):
        rs = slice(g * c, (g + 1) * c)
        delay = j * SLOT_DELAY
        cb_cache = {}
        tasks += [_ssd_head(cfg, h, j, delay, rs, pre, ssm_states[s][h], sd, masks, cb_cache, ssd_y[g])
                  for h in range(sh)]
        tasks += [_gdn_head(cfg, h, j, delay, rs, pre, gdn_states[s][h], gnorm, masks, gdn_o[g]) for h in range(gh)]
        tasks += [_rwkv_head(cfg, h, j, delay, rs, pre, rwkv_states[s][h], ln_w, ln_b, masks, incl2, rwkv_o[g])
                  for h in range(cfg.rwkv_heads)]
    tasks.append(_prelude(cfg, nset, p_ref, (gbuf, sbuf, rbuf), prm, new_pre))
    _run_interleaved(tasks)

    ng = cfg.ssm_groups
    gw = cfg.ssm_width // ng
    snorm = snorm_ref[...]
    for g in range(nslot):
        rs = slice(g * c, (g + 1) * c)
        ssd_o = []
        for grp in range(ng):
            yg = jnp.concatenate(ssd_y[g][grp * (sh // ng):(grp + 1) * (sh // ng)], axis=-1)
            yg = yg * pre["szs"][rs, grp * gw:(grp + 1) * gw]
            yg = yg * lax.rsqrt(jnp.mean(yg * yg, axis=-1, keepdims=True) + NORM_EPS)
            ssd_o.append(yg * snorm[:, grp * gw:(grp + 1) * gw])
        mix_ref[rs, :] = jnp.concatenate(gdn_o[g] + ssd_o + rwkv_o[g], axis=-1).astype(mix_ref.dtype)

    gdn_out_ref[...] = gdn_s[...]
    ssm_out_ref[...] = ssm_s[...]
    rwkv_out_ref[...] = rwkv_s[...]
    gdnc_out_ref[...] = gbuf[:, HIST - (cw - 1):HIST, :]
    ssmc_out_ref[...] = sbuf[:, HIST - (cw - 1):HIST, :]
    shift_out_ref[...] = rbuf[:, HIST - 1:HIST, :]
    for name in names:
        pre[name][...] = new_pre[name]


def _mixer_call(cfg, proj, row0, init, layer, chunks_per_seq, vecs, mats):
    nseq = init[0].shape[1]
    if chunks_per_seq % SLOTS == 0:
        nset, per_set = 1, SLOTS
    elif chunks_per_seq == 1 and nseq % SLOTS == 0:
        nset, per_set = SLOTS, 1
    else:
        nset, per_set = 1, 1
    rows = nset * per_set * CHUNK
    steps = chunks_per_seq // per_set
    assert row0 % rows == 0 and nseq % nset == 0
    blk0 = row0 // rows
    nblocks = (nseq // nset) * steps

    def p_block(t):
        return jnp.minimum(t, nblocks - 1)

    def c_block(t):
        return jnp.maximum(t - 1, 0)

    def in_spec(st, block_of):
        nd = st.ndim
        return pl.BlockSpec((None, nset) + st.shape[2:], lambda t: (layer, block_of(t) // steps) + (0,) * (nd - 2))

    def out_spec(st, block_of):
        nd = st.ndim - 1
        return pl.BlockSpec((nset,) + st.shape[2:], lambda t: (block_of(t) // steps,) + (0,) * (nd - 1))

    gdn0, gdnc0, ssm0, ssmc0, rwkv0, shift0 = init
    side = (c_block, p_block, c_block, p_block, c_block, p_block)
    in_specs = ([pl.BlockSpec((rows, proj.shape[1]), lambda t: (blk0 + p_block(t), 0))]
                + [in_spec(st, blk) for st, blk in zip(init, side)]
                + [pl.BlockSpec(p.shape, lambda t: (0, 0)) for p in (vecs, mats)])
    out_specs = ([pl.BlockSpec((rows, cfg.mix_width), lambda t: (c_block(t), 0))]
                 + [out_spec(st, blk) for st, blk in zip(init, side)])
    out_shape = [jax.ShapeDtypeStruct((nseq * chunks_per_seq * CHUNK, cfg.mix_width), BF16)] + [
        jax.ShapeDtypeStruct(st.shape[1:], F32) for st in init]
    shapes = _prelude_shapes(cfg, rows)
    names = tuple(shapes)
    scratch = [
        pltpu.VMEM((nset,) + gdn0.shape[2:], F32),
        pltpu.VMEM((nset,) + ssm0.shape[2:], F32),
        pltpu.VMEM((nset,) + rwkv0.shape[2:], F32),
        pltpu.VMEM((nset, HIST + per_set * CHUNK, gdnc0.shape[3]), F32),
        pltpu.VMEM((nset, HIST + per_set * CHUNK, ssmc0.shape[3]), F32),
        pltpu.VMEM((nset, HIST + per_set * CHUNK, shift0.shape[3]), F32),
    ] + [pltpu.VMEM(shapes[name], F32) for name in names]
    return pl.pallas_call(
        functools.partial(_mixer_body, cfg=cfg, nset=nset, steps=steps, nblocks=nblocks, names=names),
        grid=(nblocks + 1,),
        in_specs=in_specs,
        out_specs=out_specs,
        out_shape=out_shape,
        scratch_shapes=scratch,
        compiler_params=pltpu.CompilerParams(
            dimension_semantics=("arbitrary",), vmem_limit_bytes=VMEM_LIMIT_BYTES),
        name="mixers",
    )(proj, *init, vecs, mats)


def _pad_lanes(v, width=LANE):
    v = v.reshape(1, -1)
    return jnp.pad(v, ((0, 0), (0, width - v.shape[1])))


def kernel(x_prompt, x_sample, state_gdn, state_gdn_conv, state_ssm, state_ssm_conv, state_rwkv, state_rwkv_shift, norm1_w, w_in, gdn_conv_w, gdn_A_log, gdn_dt_bias, gdn_norm_w, ssm_conv_w, ssm_conv_b, ssm_A_log, ssm_dt_bias, ssm_D, ssm_norm_w, rwkv_mu, rwkv_w0, rwkv_w_up, rwkv_a0, rwkv_a_up, rwkv_g_up, rwkv_k_k, rwkv_k_a, rwkv_r_k, rwkv_ln_w, rwkv_ln_b, w_out, norm2_w, ffn_w_gate, ffn_w_up, ffn_w_down, final_norm_w):
    depth = w_in.shape[0]
    nbp, tp, d = x_prompt.shape
    nbs, ts, _ = x_sample.shape
    hd = state_gdn.shape[-1]
    ssm_width = state_ssm.shape[2] * hd
    cfg = Cfg(
        d_model=d, head_dim=hd, conv_w=gdn_conv_w.shape[1],
        gdn_heads=state_gdn.shape[2], ssm_heads=state_ssm.shape[2],
        ssm_groups=(ssm_conv_w.shape[2] - ssm_width) // (2 * state_ssm.shape[-1]),
        ssm_state=state_ssm.shape[-1], rwkv_heads=state_rwkv.shape[2],
        lora_w=rwkv_w_up.shape[1], lora_a=rwkv_a_up.shape[1], lora_g=rwkv_g_up.shape[1])
    assert tp % CHUNK == 0 and ts % CHUNK == 0 and tp >= cfg.conv_w and ts >= cfg.conv_w
    assert 2 * cfg.gdn_heads + cfg.ssm_heads <= LANE
    gw, sw = cfg.gdn_width, cfg.ssm_width
    gdn_cols = 4 * gw + 2 * cfg.gdn_heads
    ssm_cols = 2 * sw + 2 * cfg.ssm_bc + cfg.ssm_heads
    assert w_in.shape[2] == gdn_cols + ssm_cols + cfg.rwkv_cols

    x_parts = [x_prompt.reshape(nbp * tp, d), x_sample.reshape(nbs * ts, d)]

    w_in_t = jnp.swapaxes(w_in, 1, 2)

    def small_row(l, gdn_first, gdn_second, ssm_part):
        return _pad_lanes(jnp.concatenate([gdn_first[l], gdn_second, ssm_part[l]]))

    zeros_g = jnp.zeros((cfg.gdn_heads,), F32)
    head_of = jnp.arange(gw) // hd
    gdn_seg = (head_of[:, None] == head_of[None, :]).astype(BF16)
    head_of = jnp.arange(cfg.rwkv_width) // hd
    rwkv_seg = (head_of[:, None] == head_of[None, :]).astype(BF16)
    wo_b, wg_b, wu_b, wd_b = (w.astype(BF16) for w in (w_out, ffn_w_gate, ffn_w_up, ffn_w_down))
    sample_states = (state_gdn, state_gdn_conv, state_ssm, state_ssm_conv, state_rwkv, state_rwkv_shift)

    p_init = tuple(jnp.zeros((1, nbp) + st.shape[2:], F32) for st in sample_states)
    s_init = tuple(st.astype(F32) for st in sample_states)
    p_states = [[] for _ in sample_states]
    s_states = [[] for _ in sample_states]
    for l in range(depth):
        proj = _proj_call(cfg, x_parts, norm1_w[l], w_in_t, l)
        params = dict(
            gconv_w=gdn_conv_w[l], alog=small_row(l, gdn_A_log, zeros_g, ssm_A_log),
            dtb=small_row(l, gdn_dt_bias, zeros_g, ssm_dt_bias), gnorm=gdn_norm_w[l],
            sconv_w=ssm_conv_w[l], sconv_b=ssm_conv_b[l], sd=ssm_D[l], snorm=ssm_norm_w[l],
            mu=rwkv_mu[l], w0=rwkv_w0[l], a0=rwkv_a0[l], kk=rwkv_k_k[l], ka=rwkv_k_a[l], rk=rwkv_r_k[l],
            lnw=rwkv_ln_w[l], lnb=rwkv_ln_b[l],
            wup=rwkv_w_up[l], aup=rwkv_a_up[l], gup=rwkv_g_up[l], gseg=gdn_seg, rseg=rwkv_seg)
        vecs, mats = (_pack(layout, params) for layout in _param_layout(cfg))
        mix_p, *p_new = _mixer_call(cfg, proj, 0, p_init, 0, tp // CHUNK, vecs, mats)
        mix_s, *s_new = _mixer_call(cfg, proj, nbp * tp, s_init, l, ts // CHUNK, vecs, mats)
        final = l == depth - 1
        out_rows = [nbp * tp, nbs * ts] if final else [nbp * tp + nbs * ts]
        x_parts = _ffn_call(x_parts, [mix_p, mix_s], wo_b, norm2_w[l], wg_b, wu_b, wd_b, final_norm_w, l,
                            out_rows, final)
        for acc, st in zip(p_states, p_new):
            acc.append(st)
        for acc, st in zip(s_states, s_new):
            acc.append(st)

    y_prompt = x_parts[0].reshape(nbp, tp, d)
    y_sample = x_parts[1].reshape(nbs, ts, d)
    return (y_prompt, y_sample, *(jnp.stack(st) for st in p_states), *(jnp.stack(st) for st in s_states))
```

```python
import functools
import math
from typing import NamedTuple

import jax
import jax.numpy as jnp
from jax import lax
from jax.experimental import pallas as pl
from jax.experimental.pallas import tpu as pltpu

F32 = jnp.float32
BF16 = jnp.bfloat16

CHUNK = 64
SLOTS = 2
PRELUDE_DELAY = 5
SLOT_DELAY = 4
NORM_EPS = 1e-6
RWKV_GN_EPS = 64e-5
MASKED_EXPONENT = -1e30
LANE = 128
HIST = 8
VMEM_LIMIT_BYTES = 56 * 1024 * 1024
PROJ_ROWS = 512
FFN_ROWS = 512
FFN_COLS = 256


class Cfg(NamedTuple):
    d_model: int
    head_dim: int
    conv_w: int
    gdn_heads: int
    ssm_heads: int
    ssm_groups: int
    ssm_state: int
    rwkv_heads: int
    lora_w: int
    lora_a: int
    lora_g: int

    @property
    def gdn_width(self):
        return self.gdn_heads * self.head_dim

    @property
    def ssm_width(self):
        return self.ssm_heads * self.head_dim

    @property
    def ssm_bc(self):
        return self.ssm_groups * self.ssm_state

    @property
    def rwkv_width(self):
        return self.rwkv_heads * self.head_dim

    @property
    def rwkv_cols(self):
        return 3 * self.rwkv_width + self.lora_w + self.lora_a + self.lora_g

    @property
    def o_gdn_qkv(self):
        return 0

    @property
    def o_gdn_z(self):
        return 3 * self.gdn_width

    @property
    def o_ssm_z(self):
        return self.o_gdn_z + self.gdn_width

    @property
    def o_ssm_xbc(self):
        return self.o_ssm_z + self.ssm_width

    @property
    def o_rwkv(self):
        return self.o_ssm_xbc + self.ssm_width + 2 * self.ssm_bc

    @property
    def o_small(self):
        return self.o_rwkv + self.rwkv_cols

    @property
    def proj_cols(self):
        return self.o_small + LANE

    @property
    def mix_width(self):
        return self.gdn_width + self.ssm_width + self.rwkv_width


def _rms(x, w):
    return x * lax.rsqrt(jnp.mean(x * x, axis=-1, keepdims=True) + NORM_EPS) * w


def _softplus(x):
    return jnp.maximum(x, 0.0) + jnp.log1p(jnp.exp(-jnp.abs(x)))


def _silu(x):
    return x * jax.nn.sigmoid(x)


def _mm(a, b):
    return jnp.dot(a.astype(BF16), b.astype(BF16), preferred_element_type=F32)


def _mm_nt(a, b):
    return lax.dot_general(a.astype(BF16), b.astype(BF16), (((1,), (1,)), ((), ())),
                           preferred_element_type=F32)


def _split_mm(x, m):
    hi = x.astype(BF16)
    lo = (x - hi.astype(F32)).astype(BF16)
    m = m.astype(BF16)
    return jnp.dot(hi, m, preferred_element_type=F32) + jnp.dot(lo, m, preferred_element_type=F32)


def _cumsum_rows(tri, x):
    hi = x.astype(BF16)
    lo = (x - hi.astype(F32)).astype(BF16)
    return jnp.dot(tri, hi, preferred_element_type=F32) + jnp.dot(tri, lo, preferred_element_type=F32)


def _inv_one_minus_steps(n, eye_f):
    c = n.shape[0]
    t = eye_f + n
    p = _mm(n, n)
    yield
    for _ in range(int(math.log2(c)) - 2):
        step = _mm(t, p)
        p_next = _mm(p, p)
        yield
        t = t + step
        p = p_next
    step = _mm(t, p)
    yield
    return t + step


def _run_interleaved(tasks):
    tasks = list(tasks)
    while tasks:
        alive = []
        for task in tasks:
            try:
                spawned = next(task)
            except StopIteration:
                continue
            alive.append(task)
            if spawned:
                alive.extend(spawned)
        tasks = alive


class _State:
    def __init__(self, ref, index):
        self.ref, self.index, self.version = ref, index, 0

    def read(self, slot):
        assert self.version == slot, "chunk slot reads a state the previous slot has not written yet"
        return self.ref[self.index]

    def write(self, slot, value):
        assert self.version == slot
        self.ref[self.index] = value
        self.version += 1


def _conv_chunk(buf_ref, x, w_ref, conv_w):
    c = x.shape[0]
    buf_ref[HIST:HIST + c, :] = x
    y = x * w_ref[conv_w - 1:conv_w, :]
    for j in range(conv_w - 1):
        lo = HIST - (conv_w - 1) + j
        y = y + buf_ref[lo:lo + c, :] * w_ref[j:j + 1, :]
    tail = buf_ref[HIST + c - (conv_w - 1):HIST + c, :]
    buf_ref[HIST - (conv_w - 1):HIST, :] = tail
    return y


def _per_set(fn, x, nset):
    rows = x.shape[0] // nset
    return jnp.concatenate([fn(s, x[s * rows:(s + 1) * rows]) for s in range(nset)], axis=0)


def _tile_bounds(parts, rows):
    bounds, lo = [], 0
    for p in parts:
        assert p.shape[0] % rows == 0
        bounds.append((lo, lo + p.shape[0] // rows))
        lo = bounds[-1][1]
    return bounds


def _part_specs(parts, rows):
    return [pl.BlockSpec((rows, p.shape[1]), lambda i, lo=lo, hi=hi: (jnp.clip(i - lo, 0, hi - lo - 1), 0))
            for p, (lo, hi) in zip(parts, _tile_bounds(parts, rows))]


def _read_part(refs, bounds):
    i = pl.program_id(0)
    x = refs[-1][...]
    for ref, (_, hi) in reversed(list(zip(refs[:-1], bounds[:-1]))):
        x = jnp.where(i < hi, ref[...], x)
    return x


def _write_part(refs, bounds, value):
    i = pl.program_id(0)
    if len(refs) == 1:
        refs[0][...] = value
        return
    for ref, (lo, hi) in zip(refs, bounds):
        @pl.when(jnp.logical_and(i >= lo, i < hi))
        def _(ref=ref):
            ref[...] = value


def _proj_body(*refs, bounds, segments, small_cols):
    x_refs, (nw_ref, w_ref, o_ref, w_scr) = refs[:len(bounds)], refs[len(bounds):]

    @pl.when(pl.program_id(0) == 0)
    def _regroup():
        dst = 0
        for src, width in segments:
            w_scr[dst:dst + width, :] = w_ref[src:src + width, :].astype(BF16)
            dst += width
        small = jnp.concatenate([w_ref[src:src + width, :] for src, width in small_cols], axis=0)
        small = jnp.concatenate([small, jnp.zeros((LANE - small.shape[0], small.shape[1]), F32)], axis=0)
        w_scr[dst:dst + LANE, :] = small.astype(BF16)

    h = _rms(_read_part(x_refs, bounds), nw_ref[...])
    o_ref[...] = _mm_nt(h, w_scr[...])


def _proj_call(cfg, x_parts, norm_w, w_in, layer):
    d = x_parts[0].shape[1]
    n = sum(p.shape[0] for p in x_parts)
    rows = PROJ_ROWS
    gw, sw = cfg.gdn_width, cfg.ssm_width
    gdn_cols = 4 * gw + 2 * cfg.gdn_heads
    ssm_cols = 2 * sw + 2 * cfg.ssm_bc + cfg.ssm_heads
    assert w_in.shape[1] == gdn_cols + ssm_cols + cfg.rwkv_cols
    segments = ((0, 4 * gw), (gdn_cols, 2 * sw + 2 * cfg.ssm_bc), (gdn_cols + ssm_cols, cfg.rwkv_cols))
    small_cols = ((4 * gw, 2 * cfg.gdn_heads), (gdn_cols + 2 * sw + 2 * cfg.ssm_bc, cfg.ssm_heads))
    cols = cfg.proj_cols
    return pl.pallas_call(
        functools.partial(_proj_body, bounds=_tile_bounds(x_parts, rows), segments=segments, small_cols=small_cols),
        grid=(n // rows,),
        in_specs=_part_specs(x_parts, rows) + [
            pl.BlockSpec((1, d), lambda i: (0, 0)),
            pl.BlockSpec((None,) + w_in.shape[1:], lambda i: (layer, 0, 0), pipeline_mode=pl.Buffered(1)),
        ],
        out_specs=pl.BlockSpec((rows, cols), lambda i: (i, 0)),
        out_shape=jax.ShapeDtypeStruct((n, cols), F32),
        scratch_shapes=[pltpu.VMEM((cols, d), BF16)],
        compiler_params=pltpu.CompilerParams(
            dimension_semantics=("arbitrary",), vmem_limit_bytes=VMEM_LIMIT_BYTES),
        name="norm_proj",
    )(*x_parts, norm_w.reshape(1, d), w_in)


def _ffn_body(*refs, in_bounds, mix_bounds, out_bounds, final):
    nx, nm = len(in_bounds), len(mix_bounds)
    x_refs, mix_refs, refs = refs[:nx], refs[nx:nx + nm], refs[nx + nm:]
    (wo_ref, n2_ref, wg_ref, wu_ref, wd_ref, fn_ref), o_refs = refs[:6], refs[6:]
    mix = _read_part(mix_refs, mix_bounds)
    x = _read_part(x_refs, in_bounds) + jnp.dot(mix, wo_ref[...], preferred_element_type=F32)
    h2 = _rms(x, n2_ref[...]).astype(BF16)
    hidden = wg_ref.shape[1]
    acc = x
    for c0 in range(0, hidden, FFN_COLS):
        g = jnp.dot(h2, wg_ref[:, c0:c0 + FFN_COLS], preferred_element_type=F32)
        u = jnp.dot(h2, wu_ref[:, c0:c0 + FFN_COLS], preferred_element_type=F32)
        ff = (_silu(g) * u).astype(BF16)
        acc = acc + jnp.dot(ff, wd_ref[c0:c0 + FFN_COLS, :], preferred_element_type=F32)
    if final:
        acc = _rms(acc, fn_ref[...])
    _write_part(o_refs, out_bounds, acc)


def _ffn_call(x_parts, mix_parts, wo, n2, wg, wu, wd, fn, layer, out_rows, final):
    d = x_parts[0].shape[1]
    n = sum(p.shape[0] for p in x_parts)
    hidden = wg.shape[2]
    rows = FFN_ROWS
    assert n % rows == 0 and hidden % FFN_COLS == 0 and sum(out_rows) == n
    const = lambda i: (0, 0)

    def layer_spec(w):
        return pl.BlockSpec((None,) + w.shape[1:], lambda i: (layer, 0, 0))

    out_shape = [jax.ShapeDtypeStruct((r, d), F32) for r in out_rows]
    return pl.pallas_call(
        functools.partial(_ffn_body, in_bounds=_tile_bounds(x_parts, rows),
                          mix_bounds=_tile_bounds(mix_parts, rows),
                          out_bounds=_tile_bounds(out_shape, rows), final=final),
        grid=(n // rows,),
        in_specs=_part_specs(x_parts, rows) + _part_specs(mix_parts, rows) + [
            layer_spec(wo),
            pl.BlockSpec((1, d), const),
            layer_spec(wg),
            layer_spec(wu),
            layer_spec(wd),
            pl.BlockSpec((1, d), const),
        ],
        out_specs=_part_specs(out_shape, rows),
        out_shape=out_shape,
        compiler_params=pltpu.CompilerParams(
            dimension_semantics=("arbitrary",), vmem_limit_bytes=VMEM_LIMIT_BYTES),
        name="outproj_ffn",
    )(*x_parts, *mix_parts, wo, n2.reshape(1, d), wg, wu, wd, fn.reshape(1, d))


def _gdn_head(cfg, h, slot, delay, rs, pre, state, norm_w, masks, outs):
    eye_f, causal_bias, offdiag_f, _ = masks
    hd, gw = cfg.head_dim, cfg.gdn_width
    c = CHUNK
    g = rs.start // c
    for _ in range(delay):
        yield
    sl = slice(h * hd, (h + 1) * hd)
    q = pre["q"][rs, sl]
    k = pre["k"][rs, sl]
    kt = pre["kt"][g * gw + h * hd:g * gw + (h + 1) * hd, :]
    b = pre["beta"][rs, cfg.gdn_heads + h:cfg.gdn_heads + h + 1]
    kb = k * b
    vb = pre["v"][rs, sl] * b
    kq = _mm(jnp.concatenate([kb, q], axis=0), kt)
    yield
    gcol = pre["cum"][rs, h:h + 1]
    grow = pre["cum_t"][g * LANE + h:g * LANE + h + 1, :]
    dec = jnp.exp(gcol - grow + causal_bias)
    lower = kq[:c] * (dec * offdiag_f)
    attn = kq[c:] * dec
    t = yield from _inv_one_minus_steps(-lower, eye_f)
    eg = pre["eg"][rs, h:h + 1]
    u = _mm(t, vb)
    w = _mm(t, kb * eg)
    yield
    s = state.read(slot)
    wq_s = _mm(jnp.concatenate([w, q * eg], axis=0), s)
    yield
    g_last = gcol[c - 1:c, :]
    v_new = u - wq_s[:c]
    o = wq_s[c:] + _mm(attn, v_new)
    state.write(slot, s * jnp.exp(g_last) + _mm(kt * jnp.exp(g_last - grow), v_new))
    yield
    outs[h] = _rms(o, norm_w) * pre["gzs"][rs, sl]


def _ssd_head(cfg, h, slot, delay, rs, pre, state, d_row, masks, cb_cache, ys):
    _, causal_bias, _, _ = masks
    hd, ns, width = cfg.head_dim, cfg.ssm_state, cfg.ssm_width
    gh = cfg.gdn_heads
    c = CHUNK
    g = rs.start // c
    for _ in range(delay):
        yield
    grp = h // (cfg.ssm_heads // cfg.ssm_groups)
    bm = pre["xbc"][rs, width + grp * ns:width + (grp + 1) * ns]
    cm = pre["xbc"][rs, width + cfg.ssm_bc + grp * ns:width + cfg.ssm_bc + (grp + 1) * ns]
    if grp not in cb_cache:
        cb_cache[grp] = _mm_nt(cm, bm)
    lane = 2 * gh + h
    acol = pre["cum"][rs, lane:lane + 1]
    arow = pre["cum_t"][g * LANE + lane:g * LANE + lane + 1, :]
    a_last = acol[c - 1:c, :]
    x = pre["xbc"][rs, h * hd:(h + 1) * hd]
    xdt = x * pre["sp"][rs, lane:lane + 1]
    s = state.read(slot)
    y_off = _mm_nt(cm, s)
    dt_row = pre["sp_t"][g * LANE + lane:g * LANE + lane + 1, :]
    xt_dec = pre["xt"][g * width + h * hd:g * width + (h + 1) * hd, :] * (dt_row * jnp.exp(a_last - arow))
    state.write(slot, s * jnp.exp(a_last) + _mm(xt_dec, bm))
    yield
    lmat = jnp.exp(acol - arow + causal_bias)
    y_diag = _mm(cb_cache[grp] * lmat, xdt)
    yield
    ys[h] = y_diag + y_off * jnp.exp(acol) + d_row[:, h:h + 1] * x


def _rwkv_head(cfg, h, slot, delay, rs, pre, state, ln_w, ln_b, masks, incl2, outs):
    eye_f, _, _, strict2 = masks
    hd, rw = cfg.head_dim, cfg.rwkv_width
    c = CHUNK
    g = rs.start // c
    for _ in range(delay):
        yield
    sl = slice(h * hd, (h + 1) * hd)
    at = pre["at"][rs, sl]
    bt = pre["bt"][rs, sl]
    kt = pre["kt_r"][rs, sl]
    vh = pre["v_r"][rs, sl]
    pch = pre["pm"][rs.stop - 1:rs.stop, sl]
    ar = jnp.concatenate([at, pre["rt"][rs, sl]], axis=0)
    bk = jnp.concatenate([bt, kt], axis=0)
    cross = _mm_nt(ar, bk)
    vk = _mm(pre["vt_r"][g * rw + h * hd:g * rw + (h + 1) * hd, :], kt * pch)
    yield
    a_abk = jnp.where(strict2, cross[:c], 0.0)
    a_ab = a_abk[:, :c]
    aakv = _mm(a_abk[:, c:], vh)
    t = yield from _inv_one_minus_steps(a_ab, eye_f)
    s = state.read(slot)
    ar_s = _mm_nt(ar, s)
    yield
    u = _mm(t, ar_s[:c] + aakv)
    yield
    uv = jnp.concatenate([u, vh], axis=0)
    y = ar_s[c:] + _mm(jnp.where(incl2, cross[c:], 0.0), uv)
    ut = _mm_nt(eye_f, u)
    yield
    state.write(slot, s * pch + _mm(ut, bt * pch) + vk)
    yield
    mean = jnp.mean(y, axis=-1, keepdims=True)
    yc = y - mean
    var = jnp.mean(yc * yc, axis=-1, keepdims=True)
    yn = yc * lax.rsqrt(var + RWKV_GN_EPS) * ln_w[:, sl] + ln_b[:, sl]
    outs[h] = (yn + pre["bonus"][rs, sl] * vh) * pre["gate"][rs, sl]


def _prelude(cfg, nset, p_ref, bufs, prm, out):
    (gconv_w_ref, alog_ref, dtb_ref, sconv_w_ref, sconv_b_ref, mu_ref, w0_ref, wup_ref, a0_ref, aup_ref,
     gup_ref, kk_ref, ka_ref, rk_ref, gseg_ref, rseg_ref) = prm
    gbuf, sbuf, rbuf = bufs
    c = CHUNK
    rows = p_ref.shape[0]
    nslot = rows // c
    cw = cfg.conv_w
    gw, sw, rw = cfg.gdn_width, cfg.ssm_width, cfg.rwkv_width
    for _ in range(PRELUDE_DELAY):
        yield

    rr = lax.broadcasted_iota(jnp.int32, (rows, rows), 0)
    cc = lax.broadcasted_iota(jnp.int32, (rows, rows), 1)
    same_chunk = functools.reduce(jnp.logical_and, [(rr >= m * c) == (cc >= m * c) for m in range(1, nslot)],
                                  rr >= 0)
    tri = jnp.logical_and(same_chunk, rr >= cc).astype(BF16)

    small = p_ref[:, cfg.o_small:cfg.o_small + LANE]
    sp = _softplus(small + dtb_ref[...])
    cum = _cumsum_rows(tri, sp * (-jnp.exp(alog_ref[...])))

    def shift(s, pr):
        n = pr.shape[0]
        rbuf[s, HIST:HIST + n, :] = pr
        prev = rbuf[s, HIST - 1:HIST - 1 + n, :]
        rbuf[s, HIST - 1:HIST, :] = pr[n - 1:n, :]
        return pr + (prev - pr) * mu_ref[...]

    xm = _per_set(shift, p_ref[:, cfg.o_rwkv:cfg.o_rwkv + cfg.rwkv_cols], nset)
    c0 = 3 * rw
    c1 = c0 + cfg.lora_w
    c2 = c1 + cfg.lora_a
    r = xm[:, :rw]
    k = xm[:, rw:2 * rw]
    v = xm[:, 2 * rw:c0]
    lora_w = _mm(jnp.tanh(xm[:, c0:c1]), wup_ref[...])
    lora_a = _mm(xm[:, c1:c2], aup_ref[...])
    gate = _mm(jax.nn.sigmoid(xm[:, c2:]), gup_ref[...])

    qkv = _per_set(lambda s, x: _conv_chunk(gbuf.at[s], x, gconv_w_ref, cw),
                   p_ref[:, cfg.o_gdn_qkv:cfg.o_gdn_qkv + 3 * gw], nset)
    qkv = _silu(qkv)
    q_raw, k_raw = qkv[:, :gw], qkv[:, gw:2 * gw]
    ssq = _split_mm(jnp.concatenate([q_raw * q_raw, k_raw * k_raw], axis=0), gseg_ref[...])
    yield
    w_log = -_softplus(-(w0_ref[...] + lora_w)) - 0.5
    logw = -jnp.exp(w_log)
    rcum = _cumsum_rows(tri, logw)
    iclr = jax.nn.sigmoid(a0_ref[...] + lora_a)
    k2 = k * (1.0 + (iclr - 1.0) * ka_ref[...])
    kk_raw = k * kk_ref[...]
    kk_ssq = _split_mm(kk_raw * kk_raw, rseg_ref[...])
    bonus = _split_mm(r * k2 * rk_ref[...], rseg_ref[...])
    yield
    xbc = _per_set(lambda s, x: _conv_chunk(sbuf.at[s], x, sconv_w_ref, cw),
                   p_ref[:, cfg.o_ssm_xbc:cfg.o_ssm_xbc + sw + 2 * cfg.ssm_bc], nset)
    xbc = _silu(xbc + sconv_b_ref[...])
    k_all = k_raw * lax.rsqrt(ssq[rows:] + NORM_EPS)
    pm = jnp.exp(rcum)
    pinv = jnp.exp(-rcum)
    kkn = kk_raw * lax.rsqrt(kk_ssq + NORM_EPS)
    out.update(
        q=q_raw * lax.rsqrt(ssq[:rows] + NORM_EPS) * (cfg.head_dim ** -0.5), k=k_all, v=qkv[:, 2 * gw:],
        gzs=_silu(p_ref[:, cfg.o_gdn_z:cfg.o_gdn_z + gw]),
        cum=cum, sp=sp, beta=jax.nn.sigmoid(small), eg=jnp.exp(cum),
        xbc=xbc, szs=_silu(p_ref[:, cfg.o_ssm_z:cfg.o_ssm_z + sw]),
        at=-kkn * jnp.exp(rcum - logw), bt=kkn * iclr * pinv, rt=r * pm, kt_r=k2 * pinv, v_r=v, gate=gate,
        bonus=bonus, pm=pm,
        kt=jnp.concatenate([k_all[g * c:(g + 1) * c].T for g in range(nslot)], axis=0),
        cum_t=jnp.concatenate([cum[g * c:(g + 1) * c].T for g in range(nslot)], axis=0),
        sp_t=jnp.concatenate([sp[g * c:(g + 1) * c].T for g in range(nslot)], axis=0),
        xt=jnp.concatenate([xbc[g * c:(g + 1) * c, :sw].T for g in range(nslot)], axis=0),
        vt_r=jnp.concatenate([v[g * c:(g + 1) * c].T for g in range(nslot)], axis=0),
    )


def _prelude_shapes(cfg, rows):
    nslot = rows // CHUNK
    gw, sw, rw = cfg.gdn_width, cfg.ssm_width, cfg.rwkv_width
    shapes = {name: (rows, gw) for name in ("q", "k", "v", "gzs")}
    shapes.update({name: (rows, LANE) for name in ("cum", "sp", "beta", "eg")})
    shapes.update(xbc=(rows, sw + 2 * cfg.ssm_bc), szs=(rows, sw))
    shapes.update({name: (rows, rw) for name in ("at", "bt", "rt", "kt_r", "v_r", "gate", "bonus", "pm")})
    shapes.update(kt=(nslot * gw, CHUNK), cum_t=(nslot * LANE, CHUNK), sp_t=(nslot * LANE, CHUNK),
                  xt=(nslot * sw, CHUNK), vt_r=(nslot * rw, CHUNK))
    return shapes


def _param_layout(cfg):
    gw, sw, rw, hd = cfg.gdn_width, cfg.ssm_width, cfg.rwkv_width, cfg.head_dim
    vec_items = [("gconv_w", cfg.conv_w, 3 * gw), ("alog", 1, LANE), ("dtb", 1, LANE), ("gnorm", 1, hd),
                 ("sconv_w", cfg.conv_w, sw + 2 * cfg.ssm_bc), ("sconv_b", 1, sw + 2 * cfg.ssm_bc),
                 ("sd", 1, cfg.ssm_heads), ("snorm", 1, sw), ("mu", 1, cfg.rwkv_cols)]
    vec_items += [(name, 1, rw) for name in ("w0", "a0", "kk", "ka", "rk", "lnw", "lnb")]
    mat_items = [("wup", cfg.lora_w, rw), ("aup", cfg.lora_a, rw), ("gup", cfg.lora_g, rw),
                 ("gseg", gw, gw), ("rseg", rw, rw)]
    layouts = []
    for items in (vec_items, mat_items):
        layout, row = {}, 0
        for name, nrows, width in items:
            layout[name] = (row, nrows, width)
            row += -(-nrows // HIST) * HIST
        layouts.append(layout)
    return layouts


def _pack(layout, arrays):
    width = max(w for _, _, w in layout.values())
    parts = []
    for name, (_, nrows, w) in layout.items():
        a = arrays[name].astype(F32).reshape(nrows, w)
        parts.append(jnp.pad(a, ((0, -nrows % HIST), (0, width - w))))
    return jnp.concatenate(parts, axis=0)


def _mixer_body(p_ref, gdn0_ref, gdnc0_ref, ssm0_ref, ssmc0_ref, rwkv0_ref, shift0_ref, vec_ref, mat_ref,
                mix_ref, gdn_out_ref, gdnc_out_ref, ssm_out_ref, ssmc_out_ref, rwkv_out_ref, shift_out_ref,
                gdn_s, ssm_s, rwkv_s, gbuf, sbuf, rbuf, *pre_refs, cfg, nset, steps, nblocks, names):
    vec_layout, mat_layout = _param_layout(cfg)

    def view(ref, layout, name):
        row, nrows, width = layout[name]
        return ref.at[row:row + nrows, 0:width]

    (gconv_w_ref, alog_ref, dtb_ref, gnorm_ref, sconv_w_ref, sconv_b_ref, sd_ref, snorm_ref, mu_ref,
     w0_ref, a0_ref, kk_ref, ka_ref, rk_ref, lnw_ref, lnb_ref) = (view(vec_ref, vec_layout, n) for n in vec_layout)
    wup_ref, aup_ref, gup_ref, gseg_ref, rseg_ref = (view(mat_ref, mat_layout, n) for n in mat_layout)
    pre = dict(zip(names, pre_refs))
    c = CHUNK
    rows = p_ref.shape[0]
    nslot = rows // c
    per_set = nslot // nset
    slot_plan = [(s, j) for s in range(nset) for j in range(per_set)]
    cw = cfg.conv_w
    gh, sh = cfg.gdn_heads, cfg.ssm_heads
    t = pl.program_id(0)
    p_block = jnp.minimum(t, nblocks - 1)
    c_block = jnp.maximum(t - 1, 0)

    @pl.when(t == 0)
    def _clear():
        for ref in (gdn_s, ssm_s, rwkv_s) + tuple(pre_refs):
            ref[...] = jnp.zeros(ref.shape, ref.dtype)

    @pl.when(p_block % steps == 0)
    def _load_history():
        gbuf[:, HIST - (cw - 1):HIST, :] = gdnc0_ref[...]
        sbuf[:, HIST - (cw - 1):HIST, :] = ssmc0_ref[...]
        rbuf[:, HIST - 1:HIST, :] = shift0_ref[...]

    @pl.when(jnp.logical_and(t >= 1, c_block % steps == 0))
    def _load_state():
        gdn_s[...] = gdn0_ref[...]
        ssm_s[...] = ssm0_ref[...]
        rwkv_s[...] = rwkv0_ref[...]

    ri = lax.broadcasted_iota(jnp.int32, (c, c), 0)
    ci = lax.broadcasted_iota(jnp.int32, (c, c), 1)
    eye_f = (ri == ci).astype(F32)
    offdiag_f = 1.0 - eye_f
    causal_bias = jnp.where(ri >= ci, 0.0, MASKED_EXPONENT)
    ri2 = lax.broadcasted_iota(jnp.int32, (c, 2 * c), 0)
    ci2 = lax.broadcasted_iota(jnp.int32, (c, 2 * c), 1)
    ci2 = jnp.where(ci2 >= c, ci2 - c, ci2)
    strict2 = ri2 > ci2
    incl2 = ri2 >= ci2
    masks = (eye_f, causal_bias, offdiag_f, strict2)

    gnorm, sd, ln_w, ln_b = gnorm_ref[...], sd_ref[...], lnw_ref[...], lnb_ref[...]
    gdn_states = [[_State(gdn_s, (s, h)) for h in range(gh)] for s in range(nset)]
    ssm_states = [[_State(ssm_s, (s, h)) for h in range(sh)] for s in range(nset)]
    rwkv_states = [[_State(rwkv_s, (s, h)) for h in range(cfg.rwkv_heads)] for s in range(nset)]
    gdn_o = [[None] * gh for _ in slot_plan]
    ssd_y = [[None] * sh for _ in slot_plan]
    rwkv_o = [[None] * cfg.rwkv_heads for _ in slot_plan]

    new_pre = {}
    prm = (gconv_w_ref, alog_ref, dtb_ref, sconv_w_ref, sconv_b_ref, mu_ref, w0_ref, wup_ref, a0_ref, aup_ref,
           gup_ref, kk_ref, ka_ref, rk_ref, gseg_ref, rseg_ref)
    tasks = []
    for g, (s, j) in enumerate(slot_plan):
        rs = slice(g * c, (g + 1) * c)
        delay = (j + s) * SLOT_DELAY
        cb_cache = {}
        tasks += [_ssd_head(cfg, h, j, delay, rs, pre, ssm_states[s][h], sd, masks, cb_cache, ssd_y[g])
                  for h in range(sh)]
        tasks += [_gdn_head(cfg, h, j, delay, rs, pre, gdn_states[s][h], gnorm, masks, gdn_o[g]) for h in range(gh)]
        tasks += [_rwkv_head(cfg, h, j, delay, rs, pre, rwkv_states[s][h], ln_w, ln_b, masks, incl2, rwkv_o[g])
                  for h in range(cfg.rwkv_heads)]
    tasks.append(_prelude(cfg, nset, p_ref, (gbuf, sbuf, rbuf), prm, new_pre))
    _run_interleaved(tasks)

    ng = cfg.ssm_groups
    gw = cfg.ssm_width // ng
    snorm = snorm_ref[...]
    for g in range(nslot):
        rs = slice(g * c, (g + 1) * c)
        ssd_o = []
        for grp in range(ng):
            yg = jnp.concatenate(ssd_y[g][grp * (sh // ng):(grp + 1) * (sh // ng)], axis=-1)
            yg = yg * pre["szs"][rs, grp * gw:(grp + 1) * gw]
            yg = yg * lax.rsqrt(jnp.mean(yg * yg, axis=-1, keepdims=True) + NORM_EPS)
            ssd_o.append(yg * snorm[:, grp * gw:(grp + 1) * gw])
        mix_ref[rs, :] = jnp.concatenate(gdn_o[g] + ssd_o + rwkv_o[g], axis=-1).astype(mix_ref.dtype)

    gdn_out_ref[...] = gdn_s[...]
    ssm_out_ref[...] = ssm_s[...]
    rwkv_out_ref[...] = rwkv_s[...]
    gdnc_out_ref[...] = gbuf[:, HIST - (cw - 1):HIST, :]
    ssmc_out_ref[...] = sbuf[:, HIST - (cw - 1):HIST, :]
    shift_out_ref[...] = rbuf[:, HIST - 1:HIST, :]
    for name in names:
        pre[name][...] = new_pre[name]


def _mixer_call(cfg, proj, row0, init, layer, chunks_per_seq, vecs, mats):
    nseq = init[0].shape[1]
    if chunks_per_seq % SLOTS == 0:
        nset, per_set = 1, SLOTS
    elif chunks_per_seq == 1 and nseq % SLOTS == 0:
        nset, per_set = SLOTS, 1
    else:
        nset, per_set = 1, 1
    rows = nset * per_set * CHUNK
    steps = chunks_per_seq // per_set
    assert row0 % rows == 0 and nseq % nset == 0
    blk0 = row0 // rows
    nblocks = (nseq // nset) * steps

    def p_block(t):
        return jnp.minimum(t, nblocks - 1)

    def c_block(t):
        return jnp.maximum(t - 1, 0)

    def in_spec(st, block_of):
        nd = st.ndim
        return pl.BlockSpec((None, nset) + st.shape[2:], lambda t: (layer, block_of(t) // steps) + (0,) * (nd - 2))

    def out_spec(st, block_of):
        nd = st.ndim - 1
        return pl.BlockSpec((nset,) + st.shape[2:], lambda t: (block_of(t) // steps,) + (0,) * (nd - 1))

    gdn0, gdnc0, ssm0, ssmc0, rwkv0, shift0 = init
    side = (c_block, p_block, c_block, p_block, c_block, p_block)
    in_specs = ([pl.BlockSpec((rows, proj.shape[1]), lambda t: (blk0 + p_block(t), 0))]
                + [in_spec(st, blk) for st, blk in zip(init, side)]
                + [pl.BlockSpec(p.shape, lambda t: (0, 0)) for p in (vecs, mats)])
    out_specs = ([pl.BlockSpec((rows, cfg.mix_width), lambda t: (c_block(t), 0))]
                 + [out_spec(st, blk) for st, blk in zip(init, side)])
    out_shape = [jax.ShapeDtypeStruct((nseq * chunks_per_seq * CHUNK, cfg.mix_width), BF16)] + [
        jax.ShapeDtypeStruct(st.shape[1:], F32) for st in init]
    shapes = _prelude_shapes(cfg, rows)
    names = tuple(shapes)
    scratch = [
        pltpu.VMEM((nset,) + gdn0.shape[2:], F32),
        pltpu.VMEM((nset,) + ssm0.shape[2:], F32),
        pltpu.VMEM((nset,) + rwkv0.shape[2:], F32),
        pltpu.VMEM((nset, HIST + per_set * CHUNK, gdnc0.shape[3]), F32),
        pltpu.VMEM((nset, HIST + per_set * CHUNK, ssmc0.shape[3]), F32),
        pltpu.VMEM((nset, HIST + per_set * CHUNK, shift0.shape[3]), F32),
    ] + [pltpu.VMEM(shapes[name], F32) for name in names]
    return pl.pallas_call(
        functools.partial(_mixer_body, cfg=cfg, nset=nset, steps=steps, nblocks=nblocks, names=names),
        grid=(nblocks + 1,),
        in_specs=in_specs,
        out_specs=out_specs,
        out_shape=out_shape,
        scratch_shapes=scratch,
        compiler_params=pltpu.CompilerParams(
            dimension_semantics=("arbitrary",), vmem_limit_bytes=VMEM_LIMIT_BYTES),
        name="mixers",
    )(proj, *init, vecs, mats)


def _pad_lanes(v, width=LANE):
    v = v.reshape(1, -1)
    return jnp.pad(v, ((0, 0), (0, width - v.shape[1])))


def kernel(x_prompt, x_sample, state_gdn, state_gdn_conv, state_ssm, state_ssm_conv, state_rwkv, state_rwkv_shift, norm1_w, w_in, gdn_conv_w, gdn_A_log, gdn_dt_bias, gdn_norm_w, ssm_conv_w, ssm_conv_b, ssm_A_log, ssm_dt_bias, ssm_D, ssm_norm_w, rwkv_mu, rwkv_w0, rwkv_w_up, rwkv_a0, rwkv_a_up, rwkv_g_up, rwkv_k_k, rwkv_k_a, rwkv_r_k, rwkv_ln_w, rwkv_ln_b, w_out, norm2_w, ffn_w_gate, ffn_w_up, ffn_w_down, final_norm_w):
    depth = w_in.shape[0]
    nbp, tp, d = x_prompt.shape
    nbs, ts, _ = x_sample.shape
    hd = state_gdn.shape[-1]
    ssm_width = state_ssm.shape[2] * hd
    cfg = Cfg(
        d_model=d, head_dim=hd, conv_w=gdn_conv_w.shape[1],
        gdn_heads=state_gdn.shape[2], ssm_heads=state_ssm.shape[2],
        ssm_groups=(ssm_conv_w.shape[2] - ssm_width) // (2 * state_ssm.shape[-1]),
        ssm_state=state_ssm.shape[-1], rwkv_heads=state_rwkv.shape[2],
        lora_w=rwkv_w_up.shape[1], lora_a=rwkv_a_up.shape[1], lora_g=rwkv_g_up.shape[1])
    assert tp % CHUNK == 0 and ts % CHUNK == 0 and tp >= cfg.conv_w and ts >= cfg.conv_w
    assert 2 * cfg.gdn_heads + cfg.ssm_heads <= LANE
    gw, sw = cfg.gdn_width, cfg.ssm_width
    gdn_cols = 4 * gw + 2 * cfg.gdn_heads
    ssm_cols = 2 * sw + 2 * cfg.ssm_bc + cfg.ssm_heads
    assert w_in.shape[2] == gdn_cols + ssm_cols + cfg.rwkv_cols

    x_parts = [x_prompt.reshape(nbp * tp, d), x_sample.reshape(nbs * ts, d)]

    w_in_t = jnp.swapaxes(w_in, 1, 2)

    def small_row(l, gdn_first, gdn_second, ssm_part):
        return _pad_lanes(jnp.concatenate([gdn_first[l], gdn_second, ssm_part[l]]))

    zeros_g = jnp.zeros((cfg.gdn_heads,), F32)
    head_of = jnp.arange(gw) // hd
    gdn_seg = (head_of[:, None] == head_of[None, :]).astype(BF16)
    head_of = jnp.arange(cfg.rwkv_width) // hd
    rwkv_seg = (head_of[:, None] == head_of[None, :]).astype(BF16)
    wo_b, wg_b, wu_b, wd_b = (w.astype(BF16) for w in (w_out, ffn_w_gate, ffn_w_up, ffn_w_down))
    sample_states = (state_gdn, state_gdn_conv, state_ssm, state_ssm_conv, state_rwkv, state_rwkv_shift)

    p_init = tuple(jnp.zeros((1, nbp) + st.shape[2:], F32) for st in sample_states)
    s_init = tuple(st.astype(F32) for st in sample_states)
    p_states = [[] for _ in sample_states]
    s_states = [[] for _ in sample_states]
    for l in range(depth):
        proj = _proj_call(cfg, x_parts, norm1_w[l], w_in_t, l)
        params = dict(
            gconv_w=gdn_conv_w[l], alog=small_row(l, gdn_A_log, zeros_g, ssm_A_log),
            dtb=small_row(l, gdn_dt_bias, zeros_g, ssm_dt_bias), gnorm=gdn_norm_w[l],
            sconv_w=ssm_conv_w[l], sconv_b=ssm_conv_b[l], sd=ssm_D[l], snorm=ssm_norm_w[l],
            mu=rwkv_mu[l], w0=rwkv_w0[l], a0=rwkv_a0[l], kk=rwkv_k_k[l], ka=rwkv_k_a[l], rk=rwkv_r_k[l],
            lnw=rwkv_ln_w[l], lnb=rwkv_ln_b[l],
            wup=rwkv_w_up[l], aup=rwkv_a_up[l], gup=rwkv_g_up[l], gseg=gdn_seg, rseg=rwkv_seg)
        vecs, mats = (_pack(layout, params) for layout in _param_layout(cfg))
        mix_p, *p_new = _mixer_call(cfg, proj, 0, p_init, 0, tp // CHUNK, vecs, mats)
        mix_s, *s_new = _mixer_call(cfg, proj, nbp * tp, s_init, l, ts // CHUNK, vecs, mats)
        final = l == depth - 1
        out_rows = [nbp * tp, nbs * ts] if final else [nbp * tp + nbs * ts]
        x_parts = _ffn_call(x_parts, [mix_p, mix_s], wo_b, norm2_w[l], wg_b, wu_b, wd_b, final_norm_w, l,
                            out_rows, final)
        for acc, st in zip(p_states, p_new):
            acc.append(st)
        for acc, st in zip(s_states, s_new):
            acc.append(st)

    y_prompt = x_parts[0].reshape(nbp, tp, d)
    y_sample = x_parts[1].reshape(nbs, ts, d)
    return (y_prompt, y_sample, *(jnp.stack(st) for st in p_states), *(jnp.stack(st) for st in s_states))
```

```python
import functools
import math
from typing import NamedTuple

import jax
import jax.numpy as jnp
from jax import lax
from jax.experimental import pallas as pl
from jax.experimental.pallas import tpu as pltpu

F32 = jnp.float32
BF16 = jnp.bfloat16

CHUNK = 64
SLOTS = 2
PRELUDE_DELAY = 5
SLOT_DELAY = 4
NORM_EPS = 1e-6
RWKV_GN_EPS = 64e-5
MASKED_EXPONENT = -1e30
LANE = 128
HIST = 8
VMEM_LIMIT_BYTES = 56 * 1024 * 1024
PROJ_ROWS = 512
FFN_ROWS = 512
FFN_COLS = 256


class Cfg(NamedTuple):
    d_model: int
    head_dim: int
    conv_w: int
    gdn_heads: int
    ssm_heads: int
    ssm_groups: int
    ssm_state: int
    rwkv_heads: int
    lora_w: int
    lora_a: int
    lora_g: int

    @property
    def gdn_width(self):
        return self.gdn_heads * self.head_dim

    @property
    def ssm_width(self):
        return self.ssm_heads * self.head_dim

    @property
    def ssm_bc(self):
        return self.ssm_groups * self.ssm_state

    @property
    def rwkv_width(self):
        return self.rwkv_heads * self.head_dim

    @property
    def rwkv_cols(self):
        return 3 * self.rwkv_width + self.lora_w + self.lora_a + self.lora_g

    @property
    def o_gdn_qkv(self):
        return 0

    @property
    def o_gdn_z(self):
        return 3 * self.gdn_width

    @property
    def o_ssm_z(self):
        return self.o_gdn_z + self.gdn_width

    @property
    def o_ssm_xbc(self):
        return self.o_ssm_z + self.ssm_width

    @property
    def o_rwkv(self):
        return self.o_ssm_xbc + self.ssm_width + 2 * self.ssm_bc

    @property
    def o_small(self):
        return self.o_rwkv + self.rwkv_cols

    @property
    def proj_cols(self):
        return self.o_small + LANE

    @property
    def mix_width(self):
        return self.gdn_width + self.ssm_width + self.rwkv_width


def _rms(x, w):
    return x * lax.rsqrt(jnp.mean(x * x, axis=-1, keepdims=True) + NORM_EPS) * w


def _softplus(x):
    return jnp.maximum(x, 0.0) + jnp.log1p(jnp.exp(-jnp.abs(x)))


def _silu(x):
    return x * jax.nn.sigmoid(x)


def _mm(a, b):
    return jnp.dot(a.astype(BF16), b.astype(BF16), preferred_element_type=F32)


def _mm_nt(a, b):
    return lax.dot_general(a.astype(BF16), b.astype(BF16), (((1,), (1,)), ((), ())),
                           preferred_element_type=F32)


def _split_mm(x, m):
    hi = x.astype(BF16)
    lo = (x - hi.astype(F32)).astype(BF16)
    m = m.astype(BF16)
    return jnp.dot(hi, m, preferred_element_type=F32) + jnp.dot(lo, m, preferred_element_type=F32)


def _cumsum_rows(tri, x):
    hi = x.astype(BF16)
    lo = (x - hi.astype(F32)).astype(BF16)
    return jnp.dot(tri, hi, preferred_element_type=F32) + jnp.dot(tri, lo, preferred_element_type=F32)


def _inv_one_minus_steps(n, eye_f):
    c = n.shape[0]
    t = eye_f + n
    p = _mm(n, n)
    yield
    for _ in range(int(math.log2(c)) - 2):
        step = _mm(t, p)
        p_next = _mm(p, p)
        yield
        t = t + step
        p = p_next
    step = _mm(t, p)
    yield
    return t + step


def _run_interleaved(tasks):
    tasks = list(tasks)
    while tasks:
        alive = []
        for task in tasks:
            try:
                spawned = next(task)
            except StopIteration:
                continue
            alive.append(task)
            if spawned:
                alive.extend(spawned)
        tasks = alive


class _State:
    def __init__(self, ref, index):
        self.ref, self.index, self.version = ref, index, 0

    def read(self, slot):
        assert self.version == slot, "chunk slot reads a state the previous slot has not written yet"
        return self.ref[self.index]

    def write(self, slot, value):
        assert self.version == slot
        self.ref[self.index] = value
        self.version += 1


def _conv_chunk(buf_ref, x, w_ref, conv_w):
    c = x.shape[0]
    buf_ref[HIST:HIST + c, :] = x
    y = x * w_ref[conv_w - 1:conv_w, :]
    for j in range(conv_w - 1):
        lo = HIST - (conv_w - 1) + j
        y = y + buf_ref[lo:lo + c, :] * w_ref[j:j + 1, :]
    tail = buf_ref[HIST + c - (conv_w - 1):HIST + c, :]
    buf_ref[HIST - (conv_w - 1):HIST, :] = tail
    return y


def _per_set(fn, x, nset):
    rows = x.shape[0] // nset
    return jnp.concatenate([fn(s, x[s * rows:(s + 1) * rows]) for s in range(nset)], axis=0)


def _tile_bounds(parts, rows):
    bounds, lo = [], 0
    for p in parts:
        assert p.shape[0] % rows == 0
        bounds.append((lo, lo + p.shape[0] // rows))
        lo = bounds[-1][1]
    return bounds


def _part_specs(parts, rows):
    return [pl.BlockSpec((rows, p.shape[1]), lambda i, lo=lo, hi=hi: (jnp.clip(i - lo, 0, hi - lo - 1), 0))
            for p, (lo, hi) in zip(parts, _tile_bounds(parts, rows))]


def _read_part(refs, bounds):
    i = pl.program_id(0)
    x = refs[-1][...]
    for ref, (_, hi) in reversed(list(zip(refs[:-1], bounds[:-1]))):
        x = jnp.where(i < hi, ref[...], x)
    return x


def _write_part(refs, bounds, value):
    i = pl.program_id(0)
    if len(refs) == 1:
        refs[0][...] = value
        return
    for ref, (lo, hi) in zip(refs, bounds):
        @pl.when(jnp.logical_and(i >= lo, i < hi))
        def _(ref=ref):
            ref[...] = value


def _proj_body(*refs, bounds, segments, small_cols):
    x_refs, (nw_ref, w_ref, o_ref, w_scr) = refs[:len(bounds)], refs[len(bounds):]

    @pl.when(pl.program_id(0) == 0)
    def _regroup():
        dst = 0
        for src, width in segments:
            w_scr[dst:dst + width, :] = w_ref[src:src + width, :].astype(BF16)
            dst += width
        small = jnp.concatenate([w_ref[src:src + width, :] for src, width in small_cols], axis=0)
        small = jnp.concatenate([small, jnp.zeros((LANE - small.shape[0], small.shape[1]), F32)], axis=0)
        w_scr[dst:dst + LANE, :] = small.astype(BF16)

    h = _rms(_read_part(x_refs, bounds), nw_ref[...])
    o_ref[...] = _mm_nt(h, w_scr[...])


def _proj_call(cfg, x_parts, norm_w, w_in, layer):
    d = x_parts[0].shape[1]
    n = sum(p.shape[0] for p in x_parts)
    rows = PROJ_ROWS
    gw, sw = cfg.gdn_width, cfg.ssm_width
    gdn_cols = 4 * gw + 2 * cfg.gdn_heads
    ssm_cols = 2 * sw + 2 * cfg.ssm_bc + cfg.ssm_heads
    assert w_in.shape[1] == gdn_cols + ssm_cols + cfg.rwkv_cols
    segments = ((0, 4 * gw), (gdn_cols, 2 * sw + 2 * cfg.ssm_bc), (gdn_cols + ssm_cols, cfg.rwkv_cols))
    small_cols = ((4 * gw, 2 * cfg.gdn_heads), (gdn_cols + 2 * sw + 2 * cfg.ssm_bc, cfg.ssm_heads))
    cols = cfg.proj_cols
    return pl.pallas_call(
        functools.partial(_proj_body, bounds=_tile_bounds(x_parts, rows), segments=segments, small_cols=small_cols),
        grid=(n // rows,),
        in_specs=_part_specs(x_parts, rows) + [
            pl.BlockSpec((1, d), lambda i: (0, 0)),
            pl.BlockSpec((None,) + w_in.shape[1:], lambda i: (layer, 0, 0), pipeline_mode=pl.Buffered(1)),
        ],
        out_specs=pl.BlockSpec((rows, cols), lambda i: (i, 0)),
        out_shape=jax.ShapeDtypeStruct((n, cols), F32),
        scratch_shapes=[pltpu.VMEM((cols, d), BF16)],
        compiler_params=pltpu.CompilerParams(
            dimension_semantics=("arbitrary",), vmem_limit_bytes=VMEM_LIMIT_BYTES),
        name="norm_proj",
    )(*x_parts, norm_w.reshape(1, d), w_in)


def _ffn_body(*refs, in_bounds, mix_bounds, out_bounds, final):
    nx, nm = len(in_bounds), len(mix_bounds)
    x_refs, mix_refs, refs = refs[:nx], refs[nx:nx + nm], refs[nx + nm:]
    (wo_ref, n2_ref, wg_ref, wu_ref, wd_ref, fn_ref), o_refs = refs[:6], refs[6:]
    mix = _read_part(mix_refs, mix_bounds)
    x = _read_part(x_refs, in_bounds) + jnp.dot(mix, wo_ref[...], preferred_element_type=F32)
    h2 = _rms(x, n2_ref[...]).astype(BF16)
    hidden = wg_ref.shape[1]
    acc = x
    for c0 in range(0, hidden, FFN_COLS):
        g = jnp.dot(h2, wg_ref[:, c0:c0 + FFN_COLS], preferred_element_type=F32)
        u = jnp.dot(h2, wu_ref[:, c0:c0 + FFN_COLS], preferred_element_type=F32)
        ff = (_silu(g) * u).astype(BF16)
        acc = acc + jnp.dot(ff, wd_ref[c0:c0 + FFN_COLS, :], preferred_element_type=F32)
    if final:
        acc = _rms(acc, fn_ref[...])
    _write_part(o_refs, out_bounds, acc)


def _ffn_call(x_parts, mix_parts, wo, n2, wg, wu, wd, fn, layer, out_rows, final):
    d = x_parts[0].shape[1]
    n = sum(p.shape[0] for p in x_parts)
    hidden = wg.shape[2]
    rows = FFN_ROWS
    assert n % rows == 0 and hidden % FFN_COLS == 0 and sum(out_rows) == n
    const = lambda i: (0, 0)

    def layer_spec(w):
        return pl.BlockSpec((None,) + w.shape[1:], lambda i: (layer, 0, 0))

    out_shape = [jax.ShapeDtypeStruct((r, d), F32) for r in out_rows]
    return pl.pallas_call(
        functools.partial(_ffn_body, in_bounds=_tile_bounds(x_parts, rows),
                          mix_bounds=_tile_bounds(mix_parts, rows),
                          out_bounds=_tile_bounds(out_shape, rows), final=final),
        grid=(n // rows,),
        in_specs=_part_specs(x_parts, rows) + _part_specs(mix_parts, rows) + [
            layer_spec(wo),
            pl.BlockSpec((1, d), const),
            layer_spec(wg),
            layer_spec(wu),
            layer_spec(wd),
            pl.BlockSpec((1, d), const),
        ],
        out_specs=_part_specs(out_shape, rows),
        out_shape=out_shape,
        compiler_params=pltpu.CompilerParams(
            dimension_semantics=("arbitrary",), vmem_limit_bytes=VMEM_LIMIT_BYTES),
        name="outproj_ffn",
    )(*x_parts, *mix_parts, wo, n2.reshape(1, d), wg, wu, wd, fn.reshape(1, d))


def _gdn_head(cfg, h, slot, delay, rs, pre, state, norm_w, masks, outs):
    eye_f, causal_bias, offdiag_f = masks[:3]
    hd, gw = cfg.head_dim, cfg.gdn_width
    c = CHUNK
    g = rs.start // c
    for _ in range(delay):
        yield
    sl = slice(h * hd, (h + 1) * hd)
    q = pre["q"][rs, sl]
    k = pre["k"][rs, sl]
    kt = pre["kt"][g * gw + h * hd:g * gw + (h + 1) * hd, :]
    b = pre["beta"][rs, cfg.gdn_heads + h:cfg.gdn_heads + h + 1]
    kb = k * b
    vb = pre["v"][rs, sl] * b
    kq = _mm(jnp.concatenate([kb, q], axis=0), kt)
    yield
    gcol = pre["cum"][rs, h:h + 1]
    grow = pre["cum_t"][g * LANE + h:g * LANE + h + 1, :]
    dec = jnp.exp(gcol - grow + causal_bias)
    lower = kq[:c] * (dec * offdiag_f)
    attn = kq[c:] * dec
    t = yield from _inv_one_minus_steps(-lower, eye_f)
    eg = pre["eg"][rs, h:h + 1]
    u = _mm(t, vb)
    w = _mm(t, kb * eg)
    yield
    s = state.read(slot)
    wq_s = _mm(jnp.concatenate([w, q * eg], axis=0), s)
    yield
    g_last = gcol[c - 1:c, :]
    v_new = u - wq_s[:c]
    o = wq_s[c:] + _mm(attn, v_new)
    state.write(slot, s * jnp.exp(g_last) + _mm(kt * jnp.exp(g_last - grow), v_new))
    yield
    outs[h] = _rms(o, norm_w) * pre["gzs"][rs, sl]


def _ssd_group(cfg, grp, slot, delay, rs, pre, states, d_lanes, masks, ys):
    _, _, _, _, bias_tiled, head_lanes = masks
    hd, ns, width = cfg.head_dim, cfg.ssm_state, cfg.ssm_width
    c = CHUNK
    g = rs.start // c
    hpg = cfg.ssm_heads // cfg.ssm_groups
    gwid = hpg * hd
    lanes = slice(grp * gwid, (grp + 1) * gwid)
    for _ in range(delay):
        yield
    bm = pre["xbc"][rs, width + grp * ns:width + (grp + 1) * ns]
    cm = pre["xbc"][rs, width + cfg.ssm_bc + grp * ns:width + cfg.ssm_bc + (grp + 1) * ns]
    xs = pre["xbc"][rs, lanes]
    acb = pre["acb"][rs, lanes]
    arow = pre["arow"][g:g + 1, lanes]
    xdt = xs * pre["dtb"][rs, lanes]
    cb = _mm_nt(cm, jnp.concatenate([bm] * hpg, axis=0))
    s_all = jnp.concatenate([states[grp * hpg + i].read(slot) for i in range(hpg)], axis=0)
    y_off = _mm_nt(cm, s_all)
    a_last = acb[c - 1:c, :]
    row_dec = pre["dtrow"][g:g + 1, lanes] * jnp.exp(a_last - arow)
    xt = pre["xt"][g * width + grp * gwid:g * width + (grp + 1) * gwid, :]
    xt_dec = jnp.concatenate([xt[i * hd:(i + 1) * hd] * row_dec[:, i * hd:(i + 1) * hd] for i in range(hpg)], axis=0)
    upd = _mm(xt_dec, bm)
    for i in range(hpg):
        old = s_all[i * hd:(i + 1) * hd]
        states[grp * hpg + i].write(slot, old * jnp.exp(a_last[:, i * hd:i * hd + 1]) + upd[i * hd:(i + 1) * hd])
    yield
    scores = cb * jnp.exp(acb - arow + bias_tiled)
    rhs = jnp.concatenate([(xdt * head_lanes[i:i + 1, :]).astype(BF16) for i in range(hpg)], axis=0)
    y_diag = _mm(scores, rhs)
    yield
    ys[grp] = y_diag + y_off * jnp.exp(acb) + d_lanes[:, lanes] * xs


def _rwkv_head(cfg, h, slot, delay, rs, pre, state, ln_w, ln_b, masks, incl2, outs):
    eye_f, strict2 = masks[0], masks[3]
    hd, rw = cfg.head_dim, cfg.rwkv_width
    c = CHUNK
    g = rs.start // c
    for _ in range(delay):
        yield
    sl = slice(h * hd, (h + 1) * hd)
    at = pre["at"][rs, sl]
    bt = pre["bt"][rs, sl]
    kt = pre["kt_r"][rs, sl]
    vh = pre["v_r"][rs, sl]
    pch = pre["pm"][rs.stop - 1:rs.stop, sl]
    ar = jnp.concatenate([at, pre["rt"][rs, sl]], axis=0)
    bk = jnp.concatenate([bt, kt], axis=0)
    cross = _mm_nt(ar, bk)
    vk = _mm(pre["vt_r"][g * rw + h * hd:g * rw + (h + 1) * hd, :], kt * pch)
    yield
    a_abk = jnp.where(strict2, cross[:c], 0.0)
    a_ab = a_abk[:, :c]
    aakv = _mm(a_abk[:, c:], vh)
    t = yield from _inv_one_minus_steps(a_ab, eye_f)
    s = state.read(slot)
    ar_s = _mm_nt(ar, s)
    yield
    u = _mm(t, ar_s[:c] + aakv)
    yield
    uv = jnp.concatenate([u, vh], axis=0)
    y = ar_s[c:] + _mm(jnp.where(incl2, cross[c:], 0.0), uv)
    ut = _mm_nt(eye_f, u)
    yield
    state.write(slot, s * pch + _mm(ut, bt * pch) + vk)
    yield
    mean = jnp.mean(y, axis=-1, keepdims=True)
    yc = y - mean
    var = jnp.mean(yc * yc, axis=-1, keepdims=True)
    yn = yc * lax.rsqrt(var + RWKV_GN_EPS) * ln_w[:, sl] + ln_b[:, sl]
    outs[h] = (yn + pre["bonus"][rs, sl] * vh) * pre["gate"][rs, sl]


def _prelude(cfg, nset, p_ref, bufs, prm, out):
    (gconv_w_ref, alog_ref, dtb_ref, sconv_w_ref, sconv_b_ref, mu_ref, w0_ref, wup_ref, a0_ref, aup_ref,
     gup_ref, kk_ref, ka_ref, rk_ref, gseg_ref, rseg_ref) = prm
    gbuf, sbuf, rbuf = bufs
    c = CHUNK
    rows = p_ref.shape[0]
    nslot = rows // c
    cw = cfg.conv_w
    gw, sw, rw = cfg.gdn_width, cfg.ssm_width, cfg.rwkv_width
    for _ in range(PRELUDE_DELAY):
        yield

    rr = lax.broadcasted_iota(jnp.int32, (rows, rows), 0)
    cc = lax.broadcasted_iota(jnp.int32, (rows, rows), 1)
    same_chunk = functools.reduce(jnp.logical_and, [(rr >= m * c) == (cc >= m * c) for m in range(1, nslot)],
                                  rr >= 0)
    tri = jnp.logical_and(same_chunk, rr >= cc).astype(BF16)

    small = p_ref[:, cfg.o_small:cfg.o_small + LANE]
    sp = _softplus(small + dtb_ref[...])
    cum = _cumsum_rows(tri, sp * (-jnp.exp(alog_ref[...])))

    def shift(s, pr):
        n = pr.shape[0]
        rbuf[s, HIST:HIST + n, :] = pr
        prev = rbuf[s, HIST - 1:HIST - 1 + n, :]
        rbuf[s, HIST - 1:HIST, :] = pr[n - 1:n, :]
        return pr + (prev - pr) * mu_ref[...]

    xm = _per_set(shift, p_ref[:, cfg.o_rwkv:cfg.o_rwkv + cfg.rwkv_cols], nset)
    c0 = 3 * rw
    c1 = c0 + cfg.lora_w
    c2 = c1 + cfg.lora_a
    r = xm[:, :rw]
    k = xm[:, rw:2 * rw]
    v = xm[:, 2 * rw:c0]
    lora_w = _mm(jnp.tanh(xm[:, c0:c1]), wup_ref[...])
    lora_a = _mm(xm[:, c1:c2], aup_ref[...])
    gate = _mm(jax.nn.sigmoid(xm[:, c2:]), gup_ref[...])

    qkv = _per_set(lambda s, x: _conv_chunk(gbuf.at[s], x, gconv_w_ref, cw),
                   p_ref[:, cfg.o_gdn_qkv:cfg.o_gdn_qkv + 3 * gw], nset)
    qkv = _silu(qkv)
    q_raw, k_raw = qkv[:, :gw], qkv[:, gw:2 * gw]
    ssq = _split_mm(jnp.concatenate([q_raw * q_raw, k_raw * k_raw], axis=0), gseg_ref[...])
    yield
    w_log = -_softplus(-(w0_ref[...] + lora_w)) - 0.5
    logw = -jnp.exp(w_log)
    rcum = _cumsum_rows(tri, logw)
    iclr = jax.nn.sigmoid(a0_ref[...] + lora_a)
    k2 = k * (1.0 + (iclr - 1.0) * ka_ref[...])
    kk_raw = k * kk_ref[...]
    kk_ssq = _split_mm(kk_raw * kk_raw, rseg_ref[...])
    bonus = _split_mm(r * k2 * rk_ref[...], rseg_ref[...])
    yield
    xbc = _per_set(lambda s, x: _conv_chunk(sbuf.at[s], x, sconv_w_ref, cw),
                   p_ref[:, cfg.o_ssm_xbc:cfg.o_ssm_xbc + sw + 2 * cfg.ssm_bc], nset)
    xbc = _silu(xbc + sconv_b_ref[...])
    k_all = k_raw * lax.rsqrt(ssq[rows:] + NORM_EPS)
    pm = jnp.exp(rcum)
    pinv = jnp.exp(-rcum)
    kkn = kk_raw * lax.rsqrt(kk_ssq + NORM_EPS)
    out.update(
        q=q_raw * lax.rsqrt(ssq[:rows] + NORM_EPS) * (cfg.head_dim ** -0.5), k=k_all, v=qkv[:, 2 * gw:],
        gzs=_silu(p_ref[:, cfg.o_gdn_z:cfg.o_gdn_z + gw]),
        cum=cum, sp=sp, beta=jax.nn.sigmoid(small), eg=jnp.exp(cum),
        xbc=xbc, szs=_silu(p_ref[:, cfg.o_ssm_z:cfg.o_ssm_z + sw]),
        at=-kkn * jnp.exp(rcum - logw), bt=kkn * iclr * pinv, rt=r * pm, kt_r=k2 * pinv, v_r=v, gate=gate,
        bonus=bonus, pm=pm,
        kt=jnp.concatenate([k_all[g * c:(g + 1) * c].T for g in range(nslot)], axis=0),
        cum_t=jnp.concatenate([cum[g * c:(g + 1) * c].T for g in range(nslot)], axis=0),
        sp_t=jnp.concatenate([sp[g * c:(g + 1) * c].T for g in range(nslot)], axis=0),
        xt=jnp.concatenate([xbc[g * c:(g + 1) * c, :sw].T for g in range(nslot)], axis=0),
        vt_r=jnp.concatenate([v[g * c:(g + 1) * c].T for g in range(nslot)], axis=0),
    )
    sh, gh, hd = cfg.ssm_heads, cfg.gdn_heads, cfg.head_dim
    er = lax.broadcasted_iota(jnp.int32, (LANE, sw), 0)
    ec = lax.broadcasted_iota(jnp.int32, (LANE, sw), 1)
    head_of_lane = functools.reduce(jnp.add, [(ec >= m * hd).astype(jnp.int32) for m in range(1, sh)])
    spread = (er == head_of_lane + 2 * gh).astype(BF16)
    both = _split_mm(jnp.concatenate([cum, sp], axis=0), spread)
    out.update(
        acb=both[:rows], dtb=both[rows:],
        arow=jnp.concatenate([jnp.concatenate([out["cum_t"][g * LANE + 2 * gh + h:g * LANE + 2 * gh + h + 1, :]
                                               for h in range(sh)], axis=1) for g in range(nslot)], axis=0),
        dtrow=jnp.concatenate([jnp.concatenate([out["sp_t"][g * LANE + 2 * gh + h:g * LANE + 2 * gh + h + 1, :]
                                                for h in range(sh)], axis=1) for g in range(nslot)], axis=0),
    )


def _prelude_shapes(cfg, rows):
    nslot = rows // CHUNK
    gw, sw, rw = cfg.gdn_width, cfg.ssm_width, cfg.rwkv_width
    shapes = {name: (rows, gw) for name in ("q", "k", "v", "gzs")}
    shapes.update({name: (rows, LANE) for name in ("cum", "sp", "beta", "eg")})
    shapes.update(xbc=(rows, sw + 2 * cfg.ssm_bc), szs=(rows, sw))
    shapes.update({name: (rows, rw) for name in ("at", "bt", "rt", "kt_r", "v_r", "gate", "bonus", "pm")})
    shapes.update(kt=(nslot * gw, CHUNK), cum_t=(nslot * LANE, CHUNK), sp_t=(nslot * LANE, CHUNK),
                  xt=(nslot * sw, CHUNK), vt_r=(nslot * rw, CHUNK))
    shapes.update(acb=(rows, sw), dtb=(rows, sw), arow=(nslot, sw), dtrow=(nslot, sw))
    return shapes


def _param_layout(cfg):
    gw, sw, rw, hd = cfg.gdn_width, cfg.ssm_width, cfg.rwkv_width, cfg.head_dim
    vec_items = [("gconv_w", cfg.conv_w, 3 * gw), ("alog", 1, LANE), ("dtb", 1, LANE), ("gnorm", 1, hd),
                 ("sconv_w", cfg.conv_w, sw + 2 * cfg.ssm_bc), ("sconv_b", 1, sw + 2 * cfg.ssm_bc),
                 ("sd", 1, sw), ("snorm", 1, sw), ("mu", 1, cfg.rwkv_cols)]
    vec_items += [(name, 1, rw) for name in ("w0", "a0", "kk", "ka", "rk", "lnw", "lnb")]
    mat_items = [("wup", cfg.lora_w, rw), ("aup", cfg.lora_a, rw), ("gup", cfg.lora_g, rw),
                 ("gseg", gw, gw), ("rseg", rw, rw)]
    layouts = []
    for items in (vec_items, mat_items):
        layout, row = {}, 0
        for name, nrows, width in items:
            layout[name] = (row, nrows, width)
            row += -(-nrows // HIST) * HIST
        layouts.append(layout)
    return layouts


def _pack(layout, arrays):
    width = max(w for _, _, w in layout.values())
    parts = []
    for name, (_, nrows, w) in layout.items():
        a = arrays[name].astype(F32).reshape(nrows, w)
        parts.append(jnp.pad(a, ((0, -nrows % HIST), (0, width - w))))
    return jnp.concatenate(parts, axis=0)


def _mixer_body(p_ref, gdn0_ref, gdnc0_ref, ssm0_ref, ssmc0_ref, rwkv0_ref, shift0_ref, vec_ref, mat_ref,
                mix_ref, gdn_out_ref, gdnc_out_ref, ssm_out_ref, ssmc_out_ref, rwkv_out_ref, shift_out_ref,
                gdn_s, ssm_s, rwkv_s, gbuf, sbuf, rbuf, *pre_refs, cfg, nset, steps, nblocks, names):
    vec_layout, mat_layout = _param_layout(cfg)

    def view(ref, layout, name):
        row, nrows, width = layout[name]
        return ref.at[row:row + nrows, 0:width]

    (gconv_w_ref, alog_ref, dtb_ref, gnorm_ref, sconv_w_ref, sconv_b_ref, sd_ref, snorm_ref, mu_ref,
     w0_ref, a0_ref, kk_ref, ka_ref, rk_ref, lnw_ref, lnb_ref) = (view(vec_ref, vec_layout, n) for n in vec_layout)
    wup_ref, aup_ref, gup_ref, gseg_ref, rseg_ref = (view(mat_ref, mat_layout, n) for n in mat_layout)
    pre = dict(zip(names, pre_refs))
    c = CHUNK
    rows = p_ref.shape[0]
    nslot = rows // c
    per_set = nslot // nset
    slot_plan = [(s, j) for s in range(nset) for j in range(per_set)]
    cw = cfg.conv_w
    gh, sh = cfg.gdn_heads, cfg.ssm_heads
    t = pl.program_id(0)
    p_block = jnp.minimum(t, nblocks - 1)
    c_block = jnp.maximum(t - 1, 0)

    @pl.when(t == 0)
    def _clear():
        for ref in (gdn_s, ssm_s, rwkv_s) + tuple(pre_refs):
            ref[...] = jnp.zeros(ref.shape, ref.dtype)

    @pl.when(p_block % steps == 0)
    def _load_history():
        gbuf[:, HIST - (cw - 1):HIST, :] = gdnc0_ref[...]
        sbuf[:, HIST - (cw - 1):HIST, :] = ssmc0_ref[...]
        rbuf[:, HIST - 1:HIST, :] = shift0_ref[...]

    @pl.when(jnp.logical_and(t >= 1, c_block % steps == 0))
    def _load_state():
        gdn_s[...] = gdn0_ref[...]
        ssm_s[...] = ssm0_ref[...]
        rwkv_s[...] = rwkv0_ref[...]

    ri = lax.broadcasted_iota(jnp.int32, (c, c), 0)
    ci = lax.broadcasted_iota(jnp.int32, (c, c), 1)
    eye_f = (ri == ci).astype(F32)
    offdiag_f = 1.0 - eye_f
    causal_bias = jnp.where(ri >= ci, 0.0, MASKED_EXPONENT)
    ri2 = lax.broadcasted_iota(jnp.int32, (c, 2 * c), 0)
    ci2 = lax.broadcasted_iota(jnp.int32, (c, 2 * c), 1)
    ci2 = jnp.where(ci2 >= c, ci2 - c, ci2)
    strict2 = ri2 > ci2
    incl2 = ri2 >= ci2
    hpg = cfg.ssm_heads // cfg.ssm_groups
    gwid = hpg * cfg.head_dim
    rt_ = lax.broadcasted_iota(jnp.int32, (c, gwid), 0)
    ct_ = lax.broadcasted_iota(jnp.int32, (c, gwid), 1)
    ct_ = ct_ - cfg.head_dim * functools.reduce(jnp.add, [(ct_ >= m * cfg.head_dim).astype(jnp.int32)
                                                           for m in range(1, hpg)])
    bias_tiled = jnp.where(rt_ >= ct_, 0.0, MASKED_EXPONENT)
    rb = lax.broadcasted_iota(jnp.int32, (HIST, gwid), 0)
    cbk = lax.broadcasted_iota(jnp.int32, (HIST, gwid), 1)
    lane_head = functools.reduce(jnp.add, [(cbk >= m * cfg.head_dim).astype(jnp.int32) for m in range(1, hpg)])
    head_lanes = (lane_head == rb).astype(F32)
    masks = (eye_f, causal_bias, offdiag_f, strict2, bias_tiled, head_lanes)

    gnorm, sd, ln_w, ln_b = gnorm_ref[...], sd_ref[...], lnw_ref[...], lnb_ref[...]
    gdn_states = [[_State(gdn_s, (s, h)) for h in range(gh)] for s in range(nset)]
    ssm_states = [[_State(ssm_s, (s, h)) for h in range(sh)] for s in range(nset)]
    rwkv_states = [[_State(rwkv_s, (s, h)) for h in range(cfg.rwkv_heads)] for s in range(nset)]
    gdn_o = [[None] * gh for _ in slot_plan]
    ssd_y = [[None] * cfg.ssm_groups for _ in slot_plan]
    rwkv_o = [[None] * cfg.rwkv_heads for _ in slot_plan]

    new_pre = {}
    prm = (gconv_w_ref, alog_ref, dtb_ref, sconv_w_ref, sconv_b_ref, mu_ref, w0_ref, wup_ref, a0_ref, aup_ref,
           gup_ref, kk_ref, ka_ref, rk_ref, gseg_ref, rseg_ref)
    tasks = []
    for g, (s, j) in enumerate(slot_plan):
        rs = slice(g * c, (g + 1) * c)
        delay = (j + s) * SLOT_DELAY
        tasks += [_ssd_group(cfg, grp, j, delay + 2 * grp, rs, pre, ssm_states[s], sd, masks, ssd_y[g])
                  for grp in range(cfg.ssm_groups)]
        tasks += [_gdn_head(cfg, h, j, delay, rs, pre, gdn_states[s][h], gnorm, masks, gdn_o[g]) for h in range(gh)]
        tasks += [_rwkv_head(cfg, h, j, delay, rs, pre, rwkv_states[s][h], ln_w, ln_b, masks, incl2, rwkv_o[g])
                  for h in range(cfg.rwkv_heads)]
    tasks.append(_prelude(cfg, nset, p_ref, (gbuf, sbuf, rbuf), prm, new_pre))
    _run_interleaved(tasks)

    ng = cfg.ssm_groups
    gw = cfg.ssm_width // ng
    snorm = snorm_ref[...]
    for g in range(nslot):
        rs = slice(g * c, (g + 1) * c)
        ssd_o = []
        for grp in range(ng):
            yg = ssd_y[g][grp]
            yg = yg * pre["szs"][rs, grp * gw:(grp + 1) * gw]
            yg = yg * lax.rsqrt(jnp.mean(yg * yg, axis=-1, keepdims=True) + NORM_EPS)
            ssd_o.append(yg * snorm[:, grp * gw:(grp + 1) * gw])
        mix_ref[rs, :] = jnp.concatenate(gdn_o[g] + ssd_o + rwkv_o[g], axis=-1).astype(mix_ref.dtype)

    gdn_out_ref[...] = gdn_s[...]
    ssm_out_ref[...] = ssm_s[...]
    rwkv_out_ref[...] = rwkv_s[...]
    gdnc_out_ref[...] = gbuf[:, HIST - (cw - 1):HIST, :]
    ssmc_out_ref[...] = sbuf[:, HIST - (cw - 1):HIST, :]
    shift_out_ref[...] = rbuf[:, HIST - 1:HIST, :]
    for name in names:
        pre[name][...] = new_pre[name]


def _mixer_call(cfg, proj, row0, init, layer, chunks_per_seq, vecs, mats):
    nseq = init[0].shape[1]
    if chunks_per_seq % SLOTS == 0:
        nset, per_set = 1, SLOTS
    elif chunks_per_seq == 1 and nseq % SLOTS == 0:
        nset, per_set = SLOTS, 1
    else:
        nset, per_set = 1, 1
    rows = nset * per_set * CHUNK
    steps = chunks_per_seq // per_set
    assert row0 % rows == 0 and nseq % nset == 0
    blk0 = row0 // rows
    nblocks = (nseq // nset) * steps

    def p_block(t):
        return jnp.minimum(t, nblocks - 1)

    def c_block(t):
        return jnp.maximum(t - 1, 0)

    def in_spec(st, block_of):
        nd = st.ndim
        return pl.BlockSpec((None, nset) + st.shape[2:], lambda t: (layer, block_of(t) // steps) + (0,) * (nd - 2))

    def out_spec(st, block_of):
        nd = st.ndim - 1
        return pl.BlockSpec((nset,) + st.shape[2:], lambda t: (block_of(t) // steps,) + (0,) * (nd - 1))

    gdn0, gdnc0, ssm0, ssmc0, rwkv0, shift0 = init
    side = (c_block, p_block, c_block, p_block, c_block, p_block)
    in_specs = ([pl.BlockSpec((rows, proj.shape[1]), lambda t: (blk0 + p_block(t), 0))]
                + [in_spec(st, blk) for st, blk in zip(init, side)]
                + [pl.BlockSpec(p.shape, lambda t: (0, 0)) for p in (vecs, mats)])
    out_specs = ([pl.BlockSpec((rows, cfg.mix_width), lambda t: (c_block(t), 0))]
                 + [out_spec(st, blk) for st, blk in zip(init, side)])
    out_shape = [jax.ShapeDtypeStruct((nseq * chunks_per_seq * CHUNK, cfg.mix_width), BF16)] + [
        jax.ShapeDtypeStruct(st.shape[1:], F32) for st in init]
    shapes = _prelude_shapes(cfg, rows)
    names = tuple(shapes)
    scratch = [
        pltpu.VMEM((nset,) + gdn0.shape[2:], F32),
        pltpu.VMEM((nset,) + ssm0.shape[2:], F32),
        pltpu.VMEM((nset,) + rwkv0.shape[2:], F32),
        pltpu.VMEM((nset, HIST + per_set * CHUNK, gdnc0.shape[3]), F32),
        pltpu.VMEM((nset, HIST + per_set * CHUNK, ssmc0.shape[3]), F32),
        pltpu.VMEM((nset, HIST + per_set * CHUNK, shift0.shape[3]), F32),
    ] + [pltpu.VMEM(shapes[name], F32) for name in names]
    return pl.pallas_call(
        functools.partial(_mixer_body, cfg=cfg, nset=nset, steps=steps, nblocks=nblocks, names=names),
        grid=(nblocks + 1,),
        in_specs=in_specs,
        out_specs=out_specs,
        out_shape=out_shape,
        scratch_shapes=scratch,
        compiler_params=pltpu.CompilerParams(
            dimension_semantics=("arbitrary",), vmem_limit_bytes=VMEM_LIMIT_BYTES),
        name="mixers",
    )(proj, *init, vecs, mats)


def _pad_lanes(v, width=LANE):
    v = v.reshape(1, -1)
    return jnp.pad(v, ((0, 0), (0, width - v.shape[1])))


def kernel(x_prompt, x_sample, state_gdn, state_gdn_conv, state_ssm, state_ssm_conv, state_rwkv, state_rwkv_shift, norm1_w, w_in, gdn_conv_w, gdn_A_log, gdn_dt_bias, gdn_norm_w, ssm_conv_w, ssm_conv_b, ssm_A_log, ssm_dt_bias, ssm_D, ssm_norm_w, rwkv_mu, rwkv_w0, rwkv_w_up, rwkv_a0, rwkv_a_up, rwkv_g_up, rwkv_k_k, rwkv_k_a, rwkv_r_k, rwkv_ln_w, rwkv_ln_b, w_out, norm2_w, ffn_w_gate, ffn_w_up, ffn_w_down, final_norm_w):
    depth = w_in.shape[0]
    nbp, tp, d = x_prompt.shape
    nbs, ts, _ = x_sample.shape
    hd = state_gdn.shape[-1]
    ssm_width = state_ssm.shape[2] * hd
    cfg = Cfg(
        d_model=d, head_dim=hd, conv_w=gdn_conv_w.shape[1],
        gdn_heads=state_gdn.shape[2], ssm_heads=state_ssm.shape[2],
        ssm_groups=(ssm_conv_w.shape[2] - ssm_width) // (2 * state_ssm.shape[-1]),
        ssm_state=state_ssm.shape[-1], rwkv_heads=state_rwkv.shape[2],
        lora_w=rwkv_w_up.shape[1], lora_a=rwkv_a_up.shape[1], lora_g=rwkv_g_up.shape[1])
    assert tp % CHUNK == 0 and ts % CHUNK == 0 and tp >= cfg.conv_w and ts >= cfg.conv_w
    assert 2 * cfg.gdn_heads + cfg.ssm_heads <= LANE
    gw, sw = cfg.gdn_width, cfg.ssm_width
    gdn_cols = 4 * gw + 2 * cfg.gdn_heads
    ssm_cols = 2 * sw + 2 * cfg.ssm_bc + cfg.ssm_heads
    assert w_in.shape[2] == gdn_cols + ssm_cols + cfg.rwkv_cols

    x_parts = [x_prompt.reshape(nbp * tp, d), x_sample.reshape(nbs * ts, d)]

    w_in_t = jnp.swapaxes(w_in, 1, 2)

    def small_row(l, gdn_first, gdn_second, ssm_part):
        return _pad_lanes(jnp.concatenate([gdn_first[l], gdn_second, ssm_part[l]]))

    zeros_g = jnp.zeros((cfg.gdn_heads,), F32)
    head_of = jnp.arange(gw) // hd
    gdn_seg = (head_of[:, None] == head_of[None, :]).astype(BF16)
    head_of = jnp.arange(cfg.rwkv_width) // hd
    rwkv_seg = (head_of[:, None] == head_of[None, :]).astype(BF16)
    wo_b, wg_b, wu_b, wd_b = (w.astype(BF16) for w in (w_out, ffn_w_gate, ffn_w_up, ffn_w_down))
    sample_states = (state_gdn, state_gdn_conv, state_ssm, state_ssm_conv, state_rwkv, state_rwkv_shift)

    p_init = tuple(jnp.zeros((1, nbp) + st.shape[2:], F32) for st in sample_states)
    s_init = tuple(st.astype(F32) for st in sample_states)
    p_states = [[] for _ in sample_states]
    s_states = [[] for _ in sample_states]
    for l in range(depth):
        proj = _proj_call(cfg, x_parts, norm1_w[l], w_in_t, l)
        params = dict(
            gconv_w=gdn_conv_w[l], alog=small_row(l, gdn_A_log, zeros_g, ssm_A_log),
            dtb=small_row(l, gdn_dt_bias, zeros_g, ssm_dt_bias), gnorm=gdn_norm_w[l],
            sconv_w=ssm_conv_w[l], sconv_b=ssm_conv_b[l], sd=jnp.repeat(ssm_D[l], hd), snorm=ssm_norm_w[l],
            mu=rwkv_mu[l], w0=rwkv_w0[l], a0=rwkv_a0[l], kk=rwkv_k_k[l], ka=rwkv_k_a[l], rk=rwkv_r_k[l],
            lnw=rwkv_ln_w[l], lnb=rwkv_ln_b[l],
            wup=rwkv_w_up[l], aup=rwkv_a_up[l], gup=rwkv_g_up[l], gseg=gdn_seg, rseg=rwkv_seg)
        vecs, mats = (_pack(layout, params) for layout in _param_layout(cfg))
        mix_p, *p_new = _mixer_call(cfg, proj, 0, p_init, 0, tp // CHUNK, vecs, mats)
        mix_s, *s_new = _mixer_call(cfg, proj, nbp * tp, s_init, l, ts // CHUNK, vecs, mats)
        final = l == depth - 1
        out_rows = [nbp * tp, nbs * ts] if final else [nbp * tp + nbs * ts]
        x_parts = _ffn_call(x_parts, [mix_p, mix_s], wo_b, norm2_w[l], wg_b, wu_b, wd_b, final_norm_w, l,
                            out_rows, final)
        for acc, st in zip(p_states, p_new):
            acc.append(st)
        for acc, st in zip(s_states, s_new):
            acc.append(st)

    y_prompt = x_parts[0].reshape(nbp, tp, d)
    y_sample = x_parts[1].reshape(nbs, ts, d)
    return (y_prompt, y_sample, *(jnp.stack(st) for st in p_states), *(jnp.stack(st) for st in s_states))
```

```python
import functools
import math
from typing import NamedTuple

import jax
import jax.numpy as jnp
from jax import lax
from jax.experimental import pallas as pl
from jax.experimental.pallas import tpu as pltpu

F32 = jnp.float32
BF16 = jnp.bfloat16

CHUNK = 64
SLOTS = 2
PRELUDE_DELAY = 5
SLOT_DELAY = 4
NORM_EPS = 1e-6
RWKV_GN_EPS = 64e-5
MASKED_EXPONENT = -1e30
LANE = 128
HIST = 8
VMEM_LIMIT_BYTES = 56 * 1024 * 1024
PROJ_ROWS = 512
FFN_ROWS = 512
FFN_COLS = 256


class Cfg(NamedTuple):
    d_model: int
    head_dim: int
    conv_w: int
    gdn_heads: int
    ssm_heads: int
    ssm_groups: int
    ssm_state: int
    rwkv_heads: int
    lora_w: int
    lora_a: int
    lora_g: int

    @property
    def gdn_width(self):
        return self.gdn_heads * self.head_dim

    @property
    def ssm_width(self):
        return self.ssm_heads * self.head_dim

    @property
    def ssm_bc(self):
        return self.ssm_groups * self.ssm_state

    @property
    def rwkv_width(self):
        return self.rwkv_heads * self.head_dim

    @property
    def rwkv_cols(self):
        return 3 * self.rwkv_width + self.lora_w + self.lora_a + self.lora_g

    @property
    def o_gdn_qkv(self):
        return 0

    @property
    def o_gdn_z(self):
        return 3 * self.gdn_width

    @property
    def o_ssm_z(self):
        return self.o_gdn_z + self.gdn_width

    @property
    def o_ssm_xbc(self):
        return self.o_ssm_z + self.ssm_width

    @property
    def o_rwkv(self):
        return self.o_ssm_xbc + self.ssm_width + 2 * self.ssm_bc

    @property
    def o_small(self):
        return self.o_rwkv + self.rwkv_cols

    @property
    def proj_cols(self):
        return self.o_small + LANE

    @property
    def mix_width(self):
        return self.gdn_width + self.ssm_width + self.rwkv_width


def _rms(x, w):
    return x * lax.rsqrt(jnp.mean(x * x, axis=-1, keepdims=True) + NORM_EPS) * w


def _softplus(x):
    return jnp.maximum(x, 0.0) + jnp.log1p(jnp.exp(-jnp.abs(x)))


def _silu(x):
    return x * jax.nn.sigmoid(x)


def _mm(a, b):
    return jnp.dot(a.astype(BF16), b.astype(BF16), preferred_element_type=F32)


def _mm_nt(a, b):
    return lax.dot_general(a.astype(BF16), b.astype(BF16), (((1,), (1,)), ((), ())),
                           preferred_element_type=F32)


def _split_mm(x, m):
    hi = x.astype(BF16)
    lo = (x - hi.astype(F32)).astype(BF16)
    m = m.astype(BF16)
    return jnp.dot(hi, m, preferred_element_type=F32) + jnp.dot(lo, m, preferred_element_type=F32)


def _cumsum_rows(tri, x):
    hi = x.astype(BF16)
    lo = (x - hi.astype(F32)).astype(BF16)
    return jnp.dot(tri, hi, preferred_element_type=F32) + jnp.dot(tri, lo, preferred_element_type=F32)


def _inv_one_minus_steps(n, eye_f):
    c = n.shape[0]
    t = eye_f + n
    p = _mm(n, n)
    yield
    for _ in range(int(math.log2(c)) - 2):
        step = _mm(t, p)
        p_next = _mm(p, p)
        yield
        t = t + step
        p = p_next
    step = _mm(t, p)
    yield
    return t + step


def _run_interleaved(tasks):
    tasks = list(tasks)
    while tasks:
        alive = []
        for task in tasks:
            try:
                spawned = next(task)
            except StopIteration:
                continue
            alive.append(task)
            if spawned:
                alive.extend(spawned)
        tasks = alive


class _State:
    def __init__(self, ref, index):
        self.ref, self.index, self.version = ref, index, 0

    def read(self, slot):
        assert self.version == slot, "chunk slot reads a state the previous slot has not written yet"
        return self.ref[self.index]

    def write(self, slot, value):
        assert self.version == slot
        self.ref[self.index] = value
        self.version += 1


def _conv_chunk(buf_ref, x, w_ref, conv_w):
    c = x.shape[0]
    buf_ref[HIST:HIST + c, :] = x
    y = x * w_ref[conv_w - 1:conv_w, :]
    for j in range(conv_w - 1):
        lo = HIST - (conv_w - 1) + j
        y = y + buf_ref[lo:lo + c, :] * w_ref[j:j + 1, :]
    tail = buf_ref[HIST + c - (conv_w - 1):HIST + c, :]
    buf_ref[HIST - (conv_w - 1):HIST, :] = tail
    return y


def _per_set(fn, x, nset):
    rows = x.shape[0] // nset
    return jnp.concatenate([fn(s, x[s * rows:(s + 1) * rows]) for s in range(nset)], axis=0)


def _tile_bounds(parts, rows):
    bounds, lo = [], 0
    for p in parts:
        assert p.shape[0] % rows == 0
        bounds.append((lo, lo + p.shape[0] // rows))
        lo = bounds[-1][1]
    return bounds


def _part_specs(parts, rows):
    return [pl.BlockSpec((rows, p.shape[1]), lambda i, lo=lo, hi=hi: (jnp.clip(i - lo, 0, hi - lo - 1), 0))
            for p, (lo, hi) in zip(parts, _tile_bounds(parts, rows))]


def _read_part(refs, bounds):
    i = pl.program_id(0)
    x = refs[-1][...]
    for ref, (_, hi) in reversed(list(zip(refs[:-1], bounds[:-1]))):
        x = jnp.where(i < hi, ref[...], x)
    return x


def _write_part(refs, bounds, value):
    i = pl.program_id(0)
    if len(refs) == 1:
        refs[0][...] = value
        return
    for ref, (lo, hi) in zip(refs, bounds):
        @pl.when(jnp.logical_and(i >= lo, i < hi))
        def _(ref=ref):
            ref[...] = value


def _proj_body(*refs, bounds, segments, small_cols):
    x_refs, (nw_ref, w_ref, o_ref, w_scr) = refs[:len(bounds)], refs[len(bounds):]

    @pl.when(pl.program_id(0) == 0)
    def _regroup():
        dst = 0
        for src, width in segments:
            w_scr[dst:dst + width, :] = w_ref[src:src + width, :].astype(BF16)
            dst += width
        small = jnp.concatenate([w_ref[src:src + width, :] for src, width in small_cols], axis=0)
        small = jnp.concatenate([small, jnp.zeros((LANE - small.shape[0], small.shape[1]), F32)], axis=0)
        w_scr[dst:dst + LANE, :] = small.astype(BF16)

    h = _rms(_read_part(x_refs, bounds), nw_ref[...])
    o_ref[...] = _mm_nt(h, w_scr[...])


def _proj_call(cfg, x_parts, norm_w, w_in, layer):
    d = x_parts[0].shape[1]
    n = sum(p.shape[0] for p in x_parts)
    rows = PROJ_ROWS
    gw, sw = cfg.gdn_width, cfg.ssm_width
    gdn_cols = 4 * gw + 2 * cfg.gdn_heads
    ssm_cols = 2 * sw + 2 * cfg.ssm_bc + cfg.ssm_heads
    assert w_in.shape[1] == gdn_cols + ssm_cols + cfg.rwkv_cols
    segments = ((0, 4 * gw), (gdn_cols, 2 * sw + 2 * cfg.ssm_bc), (gdn_cols + ssm_cols, cfg.rwkv_cols))
    small_cols = ((4 * gw, 2 * cfg.gdn_heads), (gdn_cols + 2 * sw + 2 * cfg.ssm_bc, cfg.ssm_heads))
    cols = cfg.proj_cols
    return pl.pallas_call(
        functools.partial(_proj_body, bounds=_tile_bounds(x_parts, rows), segments=segments, small_cols=small_cols),
        grid=(n // rows,),
        in_specs=_part_specs(x_parts, rows) + [
            pl.BlockSpec((1, d), lambda i: (0, 0)),
            pl.BlockSpec((None,) + w_in.shape[1:], lambda i: (layer, 0, 0), pipeline_mode=pl.Buffered(1)),
        ],
        out_specs=pl.BlockSpec((rows, cols), lambda i: (i, 0)),
        out_shape=jax.ShapeDtypeStruct((n, cols), F32),
        scratch_shapes=[pltpu.VMEM((cols, d), BF16)],
        compiler_params=pltpu.CompilerParams(
            dimension_semantics=("arbitrary",), vmem_limit_bytes=VMEM_LIMIT_BYTES),
        name="norm_proj",
    )(*x_parts, norm_w.reshape(1, d), w_in)


def _ffn_body(*refs, in_bounds, mix_bounds, out_bounds, final):
    nx, nm = len(in_bounds), len(mix_bounds)
    x_refs, mix_refs, refs = refs[:nx], refs[nx:nx + nm], refs[nx + nm:]
    (wo_ref, n2_ref, wg_ref, wu_ref, wd_ref, fn_ref), o_refs = refs[:6], refs[6:]
    mix = _read_part(mix_refs, mix_bounds)
    x = _read_part(x_refs, in_bounds) + jnp.dot(mix, wo_ref[...], preferred_element_type=F32)
    h2 = _rms(x, n2_ref[...]).astype(BF16)
    hidden = wg_ref.shape[1]
    acc = x
    for c0 in range(0, hidden, FFN_COLS):
        g = jnp.dot(h2, wg_ref[:, c0:c0 + FFN_COLS], preferred_element_type=F32)
        u = jnp.dot(h2, wu_ref[:, c0:c0 + FFN_COLS], preferred_element_type=F32)
        ff = (_silu(g) * u).astype(BF16)
        acc = acc + jnp.dot(ff, wd_ref[c0:c0 + FFN_COLS, :], preferred_element_type=F32)
    if final:
        acc = _rms(acc, fn_ref[...])
    _write_part(o_refs, out_bounds, acc)


def _ffn_call(x_parts, mix_parts, wo, n2, wg, wu, wd, fn, layer, out_rows, final):
    d = x_parts[0].shape[1]
    n = sum(p.shape[0] for p in x_parts)
    hidden = wg.shape[2]
    rows = FFN_ROWS
    assert n % rows == 0 and hidden % FFN_COLS == 0 and sum(out_rows) == n
    const = lambda i: (0, 0)

    def layer_spec(w):
        return pl.BlockSpec((None,) + w.shape[1:], lambda i: (layer, 0, 0))

    out_shape = [jax.ShapeDtypeStruct((r, d), F32) for r in out_rows]
    return pl.pallas_call(
        functools.partial(_ffn_body, in_bounds=_tile_bounds(x_parts, rows),
                          mix_bounds=_tile_bounds(mix_parts, rows),
                          out_bounds=_tile_bounds(out_shape, rows), final=final),
        grid=(n // rows,),
        in_specs=_part_specs(x_parts, rows) + _part_specs(mix_parts, rows) + [
            layer_spec(wo),
            pl.BlockSpec((1, d), const),
            layer_spec(wg),
            layer_spec(wu),
            layer_spec(wd),
            pl.BlockSpec((1, d), const),
        ],
        out_specs=_part_specs(out_shape, rows),
        out_shape=out_shape,
        compiler_params=pltpu.CompilerParams(
            dimension_semantics=("arbitrary",), vmem_limit_bytes=VMEM_LIMIT_BYTES),
        name="outproj_ffn",
    )(*x_parts, *mix_parts, wo, n2.reshape(1, d), wg, wu, wd, fn.reshape(1, d))


def _gdn_head(cfg, h, slot, delay, rs, pre, state, norm_w, masks, outs):
    eye_f, causal_bias, offdiag_f, _ = masks
    hd, gw = cfg.head_dim, cfg.gdn_width
    c = CHUNK
    g = rs.start // c
    for _ in range(delay):
        yield
    sl = slice(h * hd, (h + 1) * hd)
    q = pre["q"][rs, sl]
    k = pre["k"][rs, sl]
    kt = pre["kt"][g * gw + h * hd:g * gw + (h + 1) * hd, :]
    b = pre["beta"][rs, cfg.gdn_heads + h:cfg.gdn_heads + h + 1]
    kb = k * b
    vb = pre["v"][rs, sl] * b
    kq = _mm(jnp.concatenate([kb, q], axis=0), kt)
    yield
    gcol = pre["cum"][rs, h:h + 1]
    grow = pre["cum_t"][g * LANE + h:g * LANE + h + 1, :]
    dec = jnp.exp(gcol - grow + causal_bias)
    lower = kq[:c] * (dec * offdiag_f)
    attn = kq[c:] * dec
    t = yield from _inv_one_minus_steps(-lower, eye_f)
    eg = pre["eg"][rs, h:h + 1]
    u = _mm(t, vb)
    w = _mm(t, kb * eg)
    yield
    s = state.read(slot)
    wq_s = _mm(jnp.concatenate([w, q * eg], axis=0), s)
    yield
    g_last = gcol[c - 1:c, :]
    v_new = u - wq_s[:c]
    o = wq_s[c:] + _mm(attn, v_new)
    state.write(slot, s * jnp.exp(g_last) + _mm(kt * jnp.exp(g_last - grow), v_new))
    yield
    outs[h] = _rms(o, norm_w) * pre["gzs"][rs, sl]


def _ssd_head(cfg, h, slot, delay, rs, pre, state, d_row, masks, cb_cache, ys):
    _, causal_bias, _, _ = masks
    hd, ns, width = cfg.head_dim, cfg.ssm_state, cfg.ssm_width
    gh = cfg.gdn_heads
    c = CHUNK
    g = rs.start // c
    for _ in range(delay):
        yield
    grp = h // (cfg.ssm_heads // cfg.ssm_groups)
    bm = pre["xbc"][rs, width + grp * ns:width + (grp + 1) * ns]
    cm = pre["xbc"][rs, width + cfg.ssm_bc + grp * ns:width + cfg.ssm_bc + (grp + 1) * ns]
    if grp not in cb_cache:
        cb_cache[grp] = _mm_nt(cm, bm)
    lane = 2 * gh + h
    acol = pre["cum"][rs, lane:lane + 1]
    arow = pre["cum_t"][g * LANE + lane:g * LANE + lane + 1, :]
    a_last = acol[c - 1:c, :]
    x = pre["xbc"][rs, h * hd:(h + 1) * hd]
    xdt = x * pre["sp"][rs, lane:lane + 1]
    s = state.read(slot)
    y_off = _mm_nt(cm, s)
    dt_row = pre["sp_t"][g * LANE + lane:g * LANE + lane + 1, :]
    xt_dec = pre["xt"][g * width + h * hd:g * width + (h + 1) * hd, :] * (dt_row * jnp.exp(a_last - arow))
    state.write(slot, s * jnp.exp(a_last) + _mm(xt_dec, bm))
    yield
    lmat = jnp.exp(acol - arow + causal_bias)
    y_diag = _mm(cb_cache[grp] * lmat, xdt)
    yield
    ys[h] = y_diag + y_off * jnp.exp(acol) + d_row[:, h:h + 1] * x


def _rwkv_head(cfg, h, slot, delay, rs, pre, state, ln_w, ln_b, masks, incl2, outs):
    eye_f, _, _, strict2 = masks
    hd, rw = cfg.head_dim, cfg.rwkv_width
    c = CHUNK
    g = rs.start // c
    for _ in range(delay):
        yield
    sl = slice(h * hd, (h + 1) * hd)
    at = pre["at"][rs, sl]
    bt = pre["bt"][rs, sl]
    kt = pre["kt_r"][rs, sl]
    vh = pre["v_r"][rs, sl]
    pch = pre["pm"][rs.stop - 1:rs.stop, sl]
    ar = jnp.concatenate([at, pre["rt"][rs, sl]], axis=0)
    bk = jnp.concatenate([bt, kt], axis=0)
    cross = _mm_nt(ar, bk)
    vk = _mm(pre["vt_r"][g * rw + h * hd:g * rw + (h + 1) * hd, :], kt * pch)
    yield
    a_abk = jnp.where(strict2, cross[:c], 0.0)
    a_ab = a_abk[:, :c]
    aakv = _mm(a_abk[:, c:], vh)
    t = yield from _inv_one_minus_steps(a_ab, eye_f)
    s = state.read(slot)
    ar_s = _mm_nt(ar, s)
    yield
    u = _mm(t, ar_s[:c] + aakv)
    yield
    uv = jnp.concatenate([u, vh], axis=0)
    y = ar_s[c:] + _mm(jnp.where(incl2, cross[c:], 0.0), uv)
    ut = _mm_nt(eye_f, u)
    yield
    state.write(slot, s * pch + _mm(ut, bt * pch) + vk)
    yield
    mean = jnp.mean(y, axis=-1, keepdims=True)
    yc = y - mean
    var = jnp.mean(yc * yc, axis=-1, keepdims=True)
    yn = yc * lax.rsqrt(var + RWKV_GN_EPS) * ln_w[:, sl] + ln_b[:, sl]
    outs[h] = (yn + pre["bonus"][rs, sl] * vh) * pre["gate"][rs, sl]


def _prelude(cfg, nset, p_ref, bufs, prm, out):
    (gconv_w_ref, alog_ref, dtb_ref, sconv_w_ref, sconv_b_ref, mu_ref, w0_ref, wup_ref, a0_ref, aup_ref,
     gup_ref, kk_ref, ka_ref, rk_ref, gseg_ref, rseg_ref) = prm
    gbuf, sbuf, rbuf = bufs
    c = CHUNK
    rows = p_ref.shape[0]
    nslot = rows // c
    cw = cfg.conv_w
    gw, sw, rw = cfg.gdn_width, cfg.ssm_width, cfg.rwkv_width
    for _ in range(PRELUDE_DELAY):
        yield

    rr = lax.broadcasted_iota(jnp.int32, (rows, rows), 0)
    cc = lax.broadcasted_iota(jnp.int32, (rows, rows), 1)
    same_chunk = functools.reduce(jnp.logical_and, [(rr >= m * c) == (cc >= m * c) for m in range(1, nslot)],
                                  rr >= 0)
    tri = jnp.logical_and(same_chunk, rr >= cc).astype(BF16)

    small = p_ref[:, cfg.o_small:cfg.o_small + LANE]
    sp = _softplus(small + dtb_ref[...])
    cum = _cumsum_rows(tri, sp * (-jnp.exp(alog_ref[...])))

    def shift(s, pr):
        n = pr.shape[0]
        rbuf[s, HIST:HIST + n, :] = pr
        prev = rbuf[s, HIST - 1:HIST - 1 + n, :]
        rbuf[s, HIST - 1:HIST, :] = pr[n - 1:n, :]
        return pr + (prev - pr) * mu_ref[...]

    xm = _per_set(shift, p_ref[:, cfg.o_rwkv:cfg.o_rwkv + cfg.rwkv_cols], nset)
    c0 = 3 * rw
    c1 = c0 + cfg.lora_w
    c2 = c1 + cfg.lora_a
    r = xm[:, :rw]
    k = xm[:, rw:2 * rw]
    v = xm[:, 2 * rw:c0]
    lora_w = _mm(jnp.tanh(xm[:, c0:c1]), wup_ref[...])
    lora_a = _mm(xm[:, c1:c2], aup_ref[...])
    gate = _mm(jax.nn.sigmoid(xm[:, c2:]), gup_ref[...])

    qkv = _per_set(lambda s, x: _conv_chunk(gbuf.at[s], x, gconv_w_ref, cw),
                   p_ref[:, cfg.o_gdn_qkv:cfg.o_gdn_qkv + 3 * gw], nset)
    qkv = _silu(qkv)
    q_raw, k_raw = qkv[:, :gw], qkv[:, gw:2 * gw]
    ssq = _split_mm(jnp.concatenate([q_raw * q_raw, k_raw * k_raw], axis=0), gseg_ref[...])
    yield
    w_log = -_softplus(-(w0_ref[...] + lora_w)) - 0.5
    logw = -jnp.exp(w_log)
    rcum = _cumsum_rows(tri, logw)
    iclr = jax.nn.sigmoid(a0_ref[...] + lora_a)
    k2 = k * (1.0 + (iclr - 1.0) * ka_ref[...])
    kk_raw = k * kk_ref[...]
    kk_ssq = _split_mm(kk_raw * kk_raw, rseg_ref[...])
    bonus = _split_mm(r * k2 * rk_ref[...], rseg_ref[...])
    yield
    xbc = _per_set(lambda s, x: _conv_chunk(sbuf.at[s], x, sconv_w_ref, cw),
                   p_ref[:, cfg.o_ssm_xbc:cfg.o_ssm_xbc + sw + 2 * cfg.ssm_bc], nset)
    xbc = _silu(xbc + sconv_b_ref[...])
    k_all = k_raw * lax.rsqrt(ssq[rows:] + NORM_EPS)
    pm = jnp.exp(rcum)
    pinv = jnp.exp(-rcum)
    kkn = kk_raw * lax.rsqrt(kk_ssq + NORM_EPS)
    out.update(
        q=q_raw * lax.rsqrt(ssq[:rows] + NORM_EPS) * (cfg.head_dim ** -0.5), k=k_all, v=qkv[:, 2 * gw:],
        gzs=_silu(p_ref[:, cfg.o_gdn_z:cfg.o_gdn_z + gw]),
        cum=cum, sp=sp, beta=jax.nn.sigmoid(small), eg=jnp.exp(cum),
        xbc=xbc, szs=_silu(p_ref[:, cfg.o_ssm_z:cfg.o_ssm_z + sw]),
        at=-kkn * jnp.exp(rcum - logw), bt=kkn * iclr * pinv, rt=r * pm, kt_r=k2 * pinv, v_r=v, gate=gate,
        bonus=bonus, pm=pm,
        kt=jnp.concatenate([k_all[g * c:(g + 1) * c].T for g in range(nslot)], axis=0),
        cum_t=jnp.concatenate([cum[g * c:(g + 1) * c].T for g in range(nslot)], axis=0),
        sp_t=jnp.concatenate([sp[g * c:(g + 1) * c].T for g in range(nslot)], axis=0),
        xt=jnp.concatenate([xbc[g * c:(g + 1) * c, :sw].T for g in range(nslot)], axis=0),
        vt_r=jnp.concatenate([v[g * c:(g + 1) * c].T for g in range(nslot)], axis=0),
    )


def _prelude_shapes(cfg, rows):
    nslot = rows // CHUNK
    gw, sw, rw = cfg.gdn_width, cfg.ssm_width, cfg.rwkv_width
    shapes = {name: (rows, gw) for name in ("q", "k", "v", "gzs")}
    shapes.update({name: (rows, LANE) for name in ("cum", "sp", "beta", "eg")})
    shapes.update(xbc=(rows, sw + 2 * cfg.ssm_bc), szs=(rows, sw))
    shapes.update({name: (rows, rw) for name in ("at", "bt", "rt", "kt_r", "v_r", "gate", "bonus", "pm")})
    shapes.update(kt=(nslot * gw, CHUNK), cum_t=(nslot * LANE, CHUNK), sp_t=(nslot * LANE, CHUNK),
                  xt=(nslot * sw, CHUNK), vt_r=(nslot * rw, CHUNK))
    return shapes


def _param_layout(cfg):
    gw, sw, rw, hd = cfg.gdn_width, cfg.ssm_width, cfg.rwkv_width, cfg.head_dim
    vec_items = [("gconv_w", cfg.conv_w, 3 * gw), ("alog", 1, LANE), ("dtb", 1, LANE), ("gnorm", 1, hd),
                 ("sconv_w", cfg.conv_w, sw + 2 * cfg.ssm_bc), ("sconv_b", 1, sw + 2 * cfg.ssm_bc),
                 ("sd", 1, cfg.ssm_heads), ("snorm", 1, sw), ("mu", 1, cfg.rwkv_cols)]
    vec_items += [(name, 1, rw) for name in ("w0", "a0", "kk", "ka", "rk", "lnw", "lnb")]
    mat_items = [("wup", cfg.lora_w, rw), ("aup", cfg.lora_a, rw), ("gup", cfg.lora_g, rw),
                 ("gseg", gw, gw), ("rseg", rw, rw)]
    layouts = []
    for items in (vec_items, mat_items):
        layout, row = {}, 0
        for name, nrows, width in items:
            layout[name] = (row, nrows, width)
            row += -(-nrows // HIST) * HIST
        layouts.append(layout)
    return layouts


def _pack(layout, arrays):
    width = max(w for _, _, w in layout.values())
    parts = []
    for name, (_, nrows, w) in layout.items():
        a = arrays[name].astype(F32).reshape(nrows, w)
        parts.append(jnp.pad(a, ((0, -nrows % HIST), (0, width - w))))
    return jnp.concatenate(parts, axis=0)


def _mixer_body(p_ref, gdn0_ref, gdnc0_ref, ssm0_ref, ssmc0_ref, rwkv0_ref, shift0_ref, vec_ref, mat_ref, *rest,
                cfg, nset, steps, nblocks, names, nprev):
    prev_refs, rest = rest[:nprev], rest[nprev:]
    (mix_ref, gdn_out_ref, gdnc_out_ref, ssm_out_ref, ssmc_out_ref, rwkv_out_ref, shift_out_ref,
     gdn_s, ssm_s, rwkv_s, gbuf, sbuf, rbuf) = rest[:13]
    pre_refs = rest[13:]
    vec_layout, mat_layout = _param_layout(cfg)

    def view(ref, layout, name):
        row, nrows, width = layout[name]
        return ref.at[row:row + nrows, 0:width]

    (gconv_w_ref, alog_ref, dtb_ref, gnorm_ref, sconv_w_ref, sconv_b_ref, sd_ref, snorm_ref, mu_ref,
     w0_ref, a0_ref, kk_ref, ka_ref, rk_ref, lnw_ref, lnb_ref) = (view(vec_ref, vec_layout, n) for n in vec_layout)
    wup_ref, aup_ref, gup_ref, gseg_ref, rseg_ref = (view(mat_ref, mat_layout, n) for n in mat_layout)
    pre = dict(zip(names, pre_refs))
    c = CHUNK
    rows = p_ref.shape[0]
    nslot = rows // c
    per_set = nslot // nset
    slot_plan = [(s, j) for s in range(nset) for j in range(per_set)]
    cw = cfg.conv_w
    gh, sh = cfg.gdn_heads, cfg.ssm_heads
    t = pl.program_id(0)
    p_block = jnp.minimum(t, nblocks - 1)
    c_block = jnp.maximum(t - 1, 0)

    @pl.when(t == 0)
    def _clear():
        for ref in (gdn_s, ssm_s, rwkv_s) + tuple(pre_refs):
            ref[...] = jnp.zeros(ref.shape, ref.dtype)

    @pl.when(p_block % steps == 0)
    def _load_history():
        gbuf[:, HIST - (cw - 1):HIST, :] = gdnc0_ref[...]
        sbuf[:, HIST - (cw - 1):HIST, :] = ssmc0_ref[...]
        rbuf[:, HIST - 1:HIST, :] = shift0_ref[...]

    @pl.when(jnp.logical_and(t >= 1, c_block % steps == 0))
    def _load_state():
        gdn_s[...] = gdn0_ref[...]
        ssm_s[...] = ssm0_ref[...]
        rwkv_s[...] = rwkv0_ref[...]

    ri = lax.broadcasted_iota(jnp.int32, (c, c), 0)
    ci = lax.broadcasted_iota(jnp.int32, (c, c), 1)
    eye_f = (ri == ci).astype(F32)
    offdiag_f = 1.0 - eye_f
    causal_bias = jnp.where(ri >= ci, 0.0, MASKED_EXPONENT)
    ri2 = lax.broadcasted_iota(jnp.int32, (c, 2 * c), 0)
    ci2 = lax.broadcasted_iota(jnp.int32, (c, 2 * c), 1)
    ci2 = jnp.where(ci2 >= c, ci2 - c, ci2)
    strict2 = ri2 > ci2
    incl2 = ri2 >= ci2
    masks = (eye_f, causal_bias, offdiag_f, strict2)

    gnorm, sd, ln_w, ln_b = gnorm_ref[...], sd_ref[...], lnw_ref[...], lnb_ref[...]
    gdn_states = [[_State(gdn_s, (s, h)) for h in range(gh)] for s in range(nset)]
    ssm_states = [[_State(ssm_s, (s, h)) for h in range(sh)] for s in range(nset)]
    rwkv_states = [[_State(rwkv_s, (s, h)) for h in range(cfg.rwkv_heads)] for s in range(nset)]
    gdn_o = [[None] * gh for _ in slot_plan]
    ssd_y = [[None] * sh for _ in slot_plan]
    rwkv_o = [[None] * cfg.rwkv_heads for _ in slot_plan]

    new_pre = {}
    prm = (gconv_w_ref, alog_ref, dtb_ref, sconv_w_ref, sconv_b_ref, mu_ref, w0_ref, wup_ref, a0_ref, aup_ref,
           gup_ref, kk_ref, ka_ref, rk_ref, gseg_ref, rseg_ref)
    tasks = []
    for g, (s, j) in enumerate(slot_plan):
        rs = slice(g * c, (g + 1) * c)
        delay = (j + s) * SLOT_DELAY
        cb_cache = {}
        tasks += [_ssd_head(cfg, h, j, delay, rs, pre, ssm_states[s][h], sd, masks, cb_cache, ssd_y[g])
                  for h in range(sh)]
        tasks += [_gdn_head(cfg, h, j, delay, rs, pre, gdn_states[s][h], gnorm, masks, gdn_o[g]) for h in range(gh)]
        tasks += [_rwkv_head(cfg, h, j, delay, rs, pre, rwkv_states[s][h], ln_w, ln_b, masks, incl2, rwkv_o[g])
                  for h in range(cfg.rwkv_heads)]
    tasks.append(_prelude(cfg, nset, p_ref, (gbuf, sbuf, rbuf), prm, new_pre))
    _run_interleaved(tasks)

    ng = cfg.ssm_groups
    gw = cfg.ssm_width // ng
    snorm = snorm_ref[...]
    for g in range(nslot):
        rs = slice(g * c, (g + 1) * c)
        ssd_o = []
        for grp in range(ng):
            yg = jnp.concatenate(ssd_y[g][grp * (sh // ng):(grp + 1) * (sh // ng)], axis=-1)
            yg = yg * pre["szs"][rs, grp * gw:(grp + 1) * gw]
            yg = yg * lax.rsqrt(jnp.mean(yg * yg, axis=-1, keepdims=True) + NORM_EPS)
            ssd_o.append(yg * snorm[:, grp * gw:(grp + 1) * gw])
        mix_ref[rs, :] = jnp.concatenate(gdn_o[g] + ssd_o + rwkv_o[g], axis=-1).astype(mix_ref.dtype)

    new_states = (gdn_s[...], gbuf[:, HIST - (cw - 1):HIST, :], ssm_s[...], sbuf[:, HIST - (cw - 1):HIST, :],
                  rwkv_s[...], rbuf[:, HIST - 1:HIST, :])
    out_refs = (gdn_out_ref, gdnc_out_ref, ssm_out_ref, ssmc_out_ref, rwkv_out_ref, shift_out_ref)
    for k, (out_ref, value) in enumerate(zip(out_refs, new_states)):
        layers = out_ref.shape[0]
        if nprev:
            out_ref[0:layers - 1] = prev_refs[k][...]
        out_ref[layers - 1] = value
    for name in names:
        pre[name][...] = new_pre[name]


def _mixer_call(cfg, proj, row0, init, layer, chunks_per_seq, vecs, mats, prev):
    nseq = init[0].shape[1]
    if chunks_per_seq % SLOTS == 0:
        nset, per_set = 1, SLOTS
    elif chunks_per_seq == 1 and nseq % SLOTS == 0:
        nset, per_set = SLOTS, 1
    else:
        nset, per_set = 1, 1
    rows = nset * per_set * CHUNK
    steps = chunks_per_seq // per_set
    assert row0 % rows == 0 and nseq % nset == 0
    blk0 = row0 // rows
    nblocks = (nseq // nset) * steps

    def p_block(t):
        return jnp.minimum(t, nblocks - 1)

    def c_block(t):
        return jnp.maximum(t - 1, 0)

    def in_spec(st, block_of):
        nd = st.ndim
        return pl.BlockSpec((None, nset) + st.shape[2:], lambda t: (layer, block_of(t) // steps) + (0,) * (nd - 2))

    def stacked_spec(st, layers, block_of):
        nd = st.ndim
        return pl.BlockSpec((layers, nset) + st.shape[2:], lambda t: (0, block_of(t) // steps) + (0,) * (nd - 2))

    gdn0, gdnc0, ssm0, ssmc0, rwkv0, shift0 = init
    side = (c_block, p_block, c_block, p_block, c_block, p_block)
    in_specs = ([pl.BlockSpec((rows, proj.shape[1]), lambda t: (blk0 + p_block(t), 0))]
                + [in_spec(st, blk) for st, blk in zip(init, side)]
                + [pl.BlockSpec(p.shape, lambda t: (0, 0)) for p in (vecs, mats)]
                + [stacked_spec(st, pv.shape[0], blk) for st, pv, blk in zip(init, prev, side)])
    nlay = (prev[0].shape[0] if prev else 0) + 1
    out_specs = ([pl.BlockSpec((rows, cfg.mix_width), lambda t: (c_block(t), 0))]
                 + [stacked_spec(st, nlay, blk) for st, blk in zip(init, side)])
    out_shape = [jax.ShapeDtypeStruct((nseq * chunks_per_seq * CHUNK, cfg.mix_width), BF16)] + [
        jax.ShapeDtypeStruct((nlay,) + st.shape[1:], F32) for st in init]
    shapes = _prelude_shapes(cfg, rows)
    names = tuple(shapes)
    scratch = [
        pltpu.VMEM((nset,) + gdn0.shape[2:], F32),
        pltpu.VMEM((nset,) + ssm0.shape[2:], F32),
        pltpu.VMEM((nset,) + rwkv0.shape[2:], F32),
        pltpu.VMEM((nset, HIST + per_set * CHUNK, gdnc0.shape[3]), F32),
        pltpu.VMEM((nset, HIST + per_set * CHUNK, ssmc0.shape[3]), F32),
        pltpu.VMEM((nset, HIST + per_set * CHUNK, shift0.shape[3]), F32),
    ] + [pltpu.VMEM(shapes[name], F32) for name in names]
    return pl.pallas_call(
        functools.partial(_mixer_body, cfg=cfg, nset=nset, steps=steps, nblocks=nblocks, names=names,
                          nprev=len(prev)),
        grid=(nblocks + 1,),
        in_specs=in_specs,
        out_specs=out_specs,
        out_shape=out_shape,
        scratch_shapes=scratch,
        compiler_params=pltpu.CompilerParams(
            dimension_semantics=("arbitrary",), vmem_limit_bytes=VMEM_LIMIT_BYTES),
        name="mixers",
    )(proj, *init, vecs, mats, *prev)


def _pad_lanes(v, width=LANE):
    v = v.reshape(1, -1)
    return jnp.pad(v, ((0, 0), (0, width - v.shape[1])))


def kernel(x_prompt, x_sample, state_gdn, state_gdn_conv, state_ssm, state_ssm_conv, state_rwkv, state_rwkv_shift, norm1_w, w_in, gdn_conv_w, gdn_A_log, gdn_dt_bias, gdn_norm_w, ssm_conv_w, ssm_conv_b, ssm_A_log, ssm_dt_bias, ssm_D, ssm_norm_w, rwkv_mu, rwkv_w0, rwkv_w_up, rwkv_a0, rwkv_a_up, rwkv_g_up, rwkv_k_k, rwkv_k_a, rwkv_r_k, rwkv_ln_w, rwkv_ln_b, w_out, norm2_w, ffn_w_gate, ffn_w_up, ffn_w_down, final_norm_w):
    depth = w_in.shape[0]
    nbp, tp, d = x_prompt.shape
    nbs, ts, _ = x_sample.shape
    hd = state_gdn.shape[-1]
    ssm_width = state_ssm.shape[2] * hd
    cfg = Cfg(
        d_model=d, head_dim=hd, conv_w=gdn_conv_w.shape[1],
        gdn_heads=state_gdn.shape[2], ssm_heads=state_ssm.shape[2],
        ssm_groups=(ssm_conv_w.shape[2] - ssm_width) // (2 * state_ssm.shape[-1]),
        ssm_state=state_ssm.shape[-1], rwkv_heads=state_rwkv.shape[2],
        lora_w=rwkv_w_up.shape[1], lora_a=rwkv_a_up.shape[1], lora_g=rwkv_g_up.shape[1])
    assert tp % CHUNK == 0 and ts % CHUNK == 0 and tp >= cfg.conv_w and ts >= cfg.conv_w
    assert 2 * cfg.gdn_heads + cfg.ssm_heads <= LANE
    gw, sw = cfg.gdn_width, cfg.ssm_width
    gdn_cols = 4 * gw + 2 * cfg.gdn_heads
    ssm_cols = 2 * sw + 2 * cfg.ssm_bc + cfg.ssm_heads
    assert w_in.shape[2] == gdn_cols + ssm_cols + cfg.rwkv_cols

    x_parts = [x_prompt.reshape(nbp * tp, d), x_sample.reshape(nbs * ts, d)]

    w_in_t = jnp.swapaxes(w_in, 1, 2)

    def small_row(l, gdn_first, gdn_second, ssm_part):
        return _pad_lanes(jnp.concatenate([gdn_first[l], gdn_second, ssm_part[l]]))

    zeros_g = jnp.zeros((cfg.gdn_heads,), F32)
    head_of = jnp.arange(gw) // hd
    gdn_seg = (head_of[:, None] == head_of[None, :]).astype(BF16)
    head_of = jnp.arange(cfg.rwkv_width) // hd
    rwkv_seg = (head_of[:, None] == head_of[None, :]).astype(BF16)
    wo_b, wg_b, wu_b, wd_b = (w.astype(BF16) for w in (w_out, ffn_w_gate, ffn_w_up, ffn_w_down))
    sample_states = (state_gdn, state_gdn_conv, state_ssm, state_ssm_conv, state_rwkv, state_rwkv_shift)

    p_init = tuple(jnp.zeros((1, nbp) + st.shape[2:], F32) for st in sample_states)
    s_init = tuple(st.astype(F32) for st in sample_states)
    p_states, s_states = [], []
    for l in range(depth):
        proj = _proj_call(cfg, x_parts, norm1_w[l], w_in_t, l)
        params = dict(
            gconv_w=gdn_conv_w[l], alog=small_row(l, gdn_A_log, zeros_g, ssm_A_log),
            dtb=small_row(l, gdn_dt_bias, zeros_g, ssm_dt_bias), gnorm=gdn_norm_w[l],
            sconv_w=ssm_conv_w[l], sconv_b=ssm_conv_b[l], sd=ssm_D[l], snorm=ssm_norm_w[l],
            mu=rwkv_mu[l], w0=rwkv_w0[l], a0=rwkv_a0[l], kk=rwkv_k_k[l], ka=rwkv_k_a[l], rk=rwkv_r_k[l],
            lnw=rwkv_ln_w[l], lnb=rwkv_ln_b[l],
            wup=rwkv_w_up[l], aup=rwkv_a_up[l], gup=rwkv_g_up[l], gseg=gdn_seg, rseg=rwkv_seg)
        vecs, mats = (_pack(layout, params) for layout in _param_layout(cfg))
        mix_p, *p_states = _mixer_call(cfg, proj, 0, p_init, 0, tp // CHUNK, vecs, mats, p_states)
        mix_s, *s_states = _mixer_call(cfg, proj, nbp * tp, s_init, l, ts // CHUNK, vecs, mats, s_states)
        final = l == depth - 1
        out_rows = [nbp * tp, nbs * ts] if final else [nbp * tp + nbs * ts]
        x_parts = _ffn_call(x_parts, [mix_p, mix_s], wo_b, norm2_w[l], wg_b, wu_b, wd_b, final_norm_w, l,
                            out_rows, final)

    y_prompt = x_parts[0].reshape(nbp, tp, d)
    y_sample = x_parts[1].reshape(nbs, ts, d)
    return (y_prompt, y_sample, *p_states, *s_states)
```
